```python
import math
import jax, jax.numpy as jnp
from jax import lax
import numpy as np

D_MODEL = 2048
BATCH = 4
SEQ = 4096
DEPTH = 2

GRID_W = 64
CTX_LEN = 256
D_MIX = D_MODEL
RET_HEADS = 4
RET_DK = 128
RET_DV = 128
RET_W = RET_HEADS * RET_DV
ATT_HEADS = 4
ATT_KV_HEADS = 2
ATT_HD = 128
ATT_W = ATT_HEADS * ATT_HD
WINDOW = 128
ATT_BLOCK = 128
SSD_HEADS = 16
SSD_HD = 64
SSD_W = SSD_HEADS * SSD_HD
SSD_GROUPS = 2
SSD_STATE = 128
SSD_CONV = 3
CHUNK = 128
D_FF = 5632
FFN_RES = 0.5
ROPE_BASE = 10000.0
NORM_EPS = 1e-6
N_MOD = 9

RET_QK_W = RET_HEADS * RET_DK
RET_COLS = 2 * RET_QK_W + 2 * RET_W
ATT_KV_W = ATT_KV_HEADS * ATT_HD
ATT_COLS = ATT_W + 2 * ATT_KV_W
SSD_BC_W = SSD_GROUPS * SSD_STATE
SSD_CONV_CH = SSD_W + 2 * SSD_BC_W
SSD_COLS = SSD_W + SSD_CONV_CH + 2 * SSD_HEADS
IN_COLS = RET_COLS + ATT_COLS + SSD_COLS

kernel_name = "hybrid_retention_swa_ssd_macaron_dit"

F32 = jnp.float32


def rmsnorm(x, w):
    xf = x.astype(F32)
    y = xf * lax.rsqrt(jnp.mean(xf * xf, axis=-1, keepdims=True) + NORM_EPS)
    return (y * w.astype(F32)).astype(x.dtype)


def adaln(cond, w, b):
    return (jax.nn.silu(cond) @ w + b).reshape(cond.shape[0], N_MOD, D_MODEL)


def modulated_norm(x, mod, i, w):
    h = rmsnorm(x, w)
    return h * (1.0 + mod[:, 3 * i + 1, None, :]) + mod[:, 3 * i, None, :]


def gated_residual(x, y, mod, i, w, weight):
    return x + weight * mod[:, 3 * i + 2, None, :] * rmsnorm(y, w)


def swiglu(h, w_gu, w_down):
    g, u = jnp.split(h @ w_gu, 2, axis=-1)
    return (jax.nn.silu(g) * u) @ w_down


def ffn_sublayer(x, mod, i, nw, w_gu, w_down):
    h = modulated_norm(x, mod, i, nw[2 * i])
    return gated_residual(x, swiglu(h, w_gu, w_down), mod, i, nw[2 * i + 1], FFN_RES)


def flip_t(a):
    return jnp.flip(a, axis=1)


def rope_half(x, ang):
    cos = jnp.cos(ang)[:, None, :]
    sin = jnp.sin(ang)[:, None, :]
    x1, x2 = jnp.split(x.astype(F32), 2, axis=-1)
    return jnp.concatenate([x1 * cos - x2 * sin, x1 * sin + x2 * cos], axis=-1)


def axial_rope(x, row, col):
    half = x.shape[-1] // 2
    freqs = ROPE_BASE ** (-jnp.arange(0, half, 2, dtype=F32) / half)
    xr = rope_half(x[..., :half], row.astype(F32)[:, None] * freqs[None, :])
    xc = rope_half(x[..., half:], col.astype(F32)[:, None] * freqs[None, :])
    return jnp.concatenate([xr, xc], axis=-1).astype(x.dtype)


def retention_scan(q, k, v, log_g, s0):
    b, t, h, dk = q.shape
    dv = v.shape[-1]
    n = t // CHUNK
    qc = q.reshape(b, n, CHUNK, h, dk)
    kc = k.reshape(b, n, CHUNK, h, dk)
    vc = v.reshape(b, n, CHUNK, h, dv)
    idx = jnp.arange(CHUNK, dtype=F32)
    rel = idx[:, None] - idx[None, :]
    dmask = jnp.where(rel >= 0, jnp.exp(log_g[:, None, None] * jnp.maximum(rel, 0.0)), 0.0)
    inner = jnp.einsum('bnihd,bnjhd->bnhij', qc, kc) * dmask
    y_intra = jnp.einsum('bnhij,bnjhe->bnihe', inner, vc)
    k_decay = jnp.exp(log_g[:, None] * (CHUNK - 1.0 - idx)[None, :])
    contrib = jnp.einsum('bnjhd,hj,bnjhe->bnhde', kc, k_decay, vc)
    chunk_decay = jnp.exp(log_g * CHUNK)[None, :, None, None]

    def step(s, u):
        return s * chunk_decay + u, s

    s_final, s_prev = lax.scan(step, s0, jnp.moveaxis(contrib, 1, 0))
    s_prev = jnp.moveaxis(s_prev, 0, 1)
    q_decay = jnp.exp(log_g[:, None] * (idx + 1.0)[None, :])
    y_cross = jnp.einsum('bnihd,hi,bnhde->bnihe', qc, q_decay, s_prev)
    return (y_intra + y_cross).reshape(b, t, h, dv), s_final


def retention_scan_rev(q, k, v, log_g, s0):
    y, s = retention_scan(flip_t(q), flip_t(k), flip_t(v), log_g, s0)
    return flip_t(y), s


def retention_heads(p):
    b, t = p.shape[:2]
    q = p[..., :RET_QK_W].reshape(b, t, RET_HEADS, RET_DK).astype(F32) * RET_DK ** -0.5
    k = p[..., RET_QK_W:2 * RET_QK_W].reshape(b, t, RET_HEADS, RET_DK).astype(F32)
    v = p[..., 2 * RET_QK_W:2 * RET_QK_W + RET_W].reshape(b, t, RET_HEADS, RET_DV).astype(F32)
    g = p[..., 2 * RET_QK_W + RET_W:]
    return q, k, v, g


def retention_out(y, g, norm_w, dtype):
    b, t = y.shape[:2]
    mu = jnp.mean(y, axis=-1, keepdims=True)
    var = jnp.mean(jnp.square(y - mu), axis=-1, keepdims=True)
    yn = ((y - mu) * lax.rsqrt(var + NORM_EPS)).reshape(b, t, RET_W) * norm_w.astype(F32)
    return (yn * jax.nn.silu(g.astype(F32))).astype(dtype)


def retention_group(pl, pc, log_decay, norm_w, ctx_out):
    lg = -jnp.abs(log_decay.astype(F32))
    ql, kl, vl, gl = retention_heads(pl)
    qc, kc, vc, gc = retention_heads(pc)
    s0 = jnp.zeros((pc.shape[0], RET_HEADS, RET_DK, RET_DV), F32)
    yc_f, s_f = retention_scan(qc, kc, vc, lg[0], s0)
    yc_b, s_b = retention_scan_rev(qc, kc, vc, lg[1], s0)
    yl_f, _ = retention_scan(ql, kl, vl, lg[0], s_f)
    yl_b, _ = retention_scan_rev(ql, kl, vl, lg[1], s_b)
    out_l = retention_out(yl_f + yl_b, gl, norm_w, pl.dtype)
    out_c = retention_out(yc_f + yc_b, gc, norm_w, pc.dtype) if ctx_out else None
    return out_l, out_c


def window_gqa_group(pl, pc, sink, row, col, ctx_out):
    b, t = pl.shape[:2]
    n_ctx = pc.shape[1]
    grp = ATT_HEADS // ATT_KV_HEADS
    scale = ATT_HD ** -0.5
    ql = axial_rope(pl[..., :ATT_W].reshape(b, t, ATT_HEADS, ATT_HD), row, col)
    kl = axial_rope(pl[..., ATT_W:ATT_W + ATT_KV_W].reshape(b, t, ATT_KV_HEADS, ATT_HD), row, col)
    vl = pl[..., ATT_W + ATT_KV_W:].reshape(b, t, ATT_KV_HEADS, ATT_HD)
    kc = pc[..., ATT_W:ATT_W + ATT_KV_W].reshape(b, n_ctx, ATT_KV_HEADS, ATT_HD)
    vc = pc[..., ATT_W + ATT_KV_W:].reshape(b, n_ctx, ATT_KV_HEADS, ATT_HD)
    sink = sink.astype(F32).reshape(ATT_KV_HEADS, grp)

    nb = t // ATT_BLOCK
    qb = ql.reshape(b, nb, ATT_BLOCK, ATT_KV_HEADS, grp, ATT_HD)
    pad = ((0, 0), (ATT_BLOCK, ATT_BLOCK), (0, 0), (0, 0))
    kp = jnp.pad(kl, pad).reshape(b, nb + 2, ATT_BLOCK, ATT_KV_HEADS, ATT_HD)
    vp = jnp.pad(vl, pad).reshape(b, nb + 2, ATT_BLOCK, ATT_KV_HEADS, ATT_HD)
    kw = jnp.concatenate([kp[:, :-2], kp[:, 1:-1], kp[:, 2:]], axis=2)
    vw = jnp.concatenate([vp[:, :-2], vp[:, 1:-1], vp[:, 2:]], axis=2)
    s_loc = jnp.einsum('bnihgd,bnjhd->bnhgij', qb, kw, preferred_element_type=F32) * scale
    s_cx = jnp.einsum('bnihgd,bjhd->bnhgij', qb, kc, preferred_element_type=F32) * scale
    blk = jnp.arange(nb)
    qpos = blk[:, None] * ATT_BLOCK + jnp.arange(ATT_BLOCK)[None, :]
    kpos = (blk[:, None] - 1) * ATT_BLOCK + jnp.arange(3 * ATT_BLOCK)[None, :]
    valid = ((jnp.abs(kpos[:, None, :] - qpos[:, :, None]) <= WINDOW)
             & (kpos[:, None, :] >= 0) & (kpos[:, None, :] < t))
    s_loc = jnp.where(valid[None, :, None, None], s_loc, -jnp.inf)
    sink_col = jnp.broadcast_to(sink[None, None, :, :, None, None], s_loc.shape[:-1] + (1,))
    p = jax.nn.softmax(jnp.concatenate([s_loc, s_cx, sink_col], axis=-1), axis=-1)
    w3 = 3 * ATT_BLOCK
    o = (jnp.einsum('bnhgij,bnjhd->bnihgd', p[..., :w3].astype(vl.dtype), vw)
         + jnp.einsum('bnhgij,bjhd->bnihgd', p[..., w3:w3 + n_ctx].astype(vl.dtype), vc))
    out_l = o.reshape(b, t, ATT_W)

    out_c = None
    if ctx_out:
        qc = pc[..., :ATT_W].reshape(b, n_ctx, ATT_KV_HEADS, grp, ATT_HD)
        sc = jnp.einsum('bihgd,bjhd->bhgij', qc, kc, preferred_element_type=F32) * scale
        sink_c = jnp.broadcast_to(sink[None, :, :, None, None], sc.shape[:-1] + (1,))
        pcx = jax.nn.softmax(jnp.concatenate([sc, sink_c], axis=-1), axis=-1)
        oc = jnp.einsum('bhgij,bjhd->bihgd', pcx[..., :n_ctx].astype(vc.dtype), vc)
        out_c = oc.reshape(b, n_ctx, ATT_W)
    return out_l, out_c


def dwconv_silu(u, w, bias):
    k = w.shape[0]
    y = lax.conv_general_dilated(u, w[:, None, :], window_strides=(1,),
                                 padding=((k // 2, k // 2),),
                                 dimension_numbers=('NWC', 'WIO', 'NWC'),
                                 feature_group_count=u.shape[-1])
    return jax.nn.silu(y + bias)


def ssd_scan(x, dt, a, bm, cm, s0):
    b, t, h, p = x.shape
    g, n = bm.shape[2], bm.shape[3]
    r = h // g
    nc = t // CHUNK
    xc = (x * dt[..., None]).reshape(b, nc, CHUNK, g, r, p)
    la = (dt * a[None, None, :]).reshape(b, nc, CHUNK, g, r)
    acum = jnp.cumsum(la, axis=2)
    bc = bm.reshape(b, nc, CHUNK, g, n)
    cc = cm.reshape(b, nc, CHUNK, g, n)
    tri = jnp.tril(jnp.ones((CHUNK, CHUNK), dtype=bool))[None, None, :, :, None, None]
    diff = acum[:, :, :, None] - acum[:, :, None, :]
    lmat = jnp.exp(jnp.where(tri, diff, -jnp.inf))
    cb = jnp.einsum('bclgn,bcsgn->bclsg', cc, bc)
    y_diag = jnp.einsum('bclsgr,bcsgrp->bclgrp', cb[..., None] * lmat, xc)
    decay_states = jnp.exp(acum[:, :, -1:] - acum)
    states = jnp.einsum('bclgn,bclgrp->bcgrpn', bc, xc * decay_states[..., None])
    chunk_decay = jnp.exp(acum[:, :, -1])

    def step(s, inp):
        u, dcy = inp
        return s * dcy[..., None, None] + u, s

    s_final, s_prev = lax.scan(step, s0, (jnp.moveaxis(states, 1, 0), jnp.moveaxis(chunk_decay, 1, 0)))
    s_prev = jnp.moveaxis(s_prev, 0, 1)
    y_off = jnp.einsum('bclgn,bcgrpn->bclgrp', cc, s_prev) * jnp.exp(acum)[..., None]
    return (y_diag + y_off).reshape(b, t, h, p), s_final


def ssd_scan_rev(x, dt, a, bm, cm, s0):
    y, s = ssd_scan(flip_t(x), flip_t(dt), a, flip_t(bm), flip_t(cm), s0)
    return flip_t(y), s


def ssd_prep(p, conv_w, conv_b, dt_bias):
    b, t = p.shape[:2]
    z = p[..., :SSD_W]
    xbc = dwconv_silu(p[..., SSD_W:SSD_W + SSD_CONV_CH], conv_w, conv_b)
    xs = xbc[..., :SSD_W].reshape(b, t, SSD_HEADS, SSD_HD).astype(F32)
    bm = xbc[..., SSD_W:SSD_W + SSD_BC_W].reshape(b, t, SSD_GROUPS, SSD_STATE).astype(F32)
    cm = xbc[..., SSD_W + SSD_BC_W:].reshape(b, t, SSD_GROUPS, SSD_STATE).astype(F32)
    dt = jax.nn.softplus(p[..., SSD_W + SSD_CONV_CH:].reshape(b, t, 2, SSD_HEADS).astype(F32)
                         + dt_bias.astype(F32))
    return z, xs, bm, cm, dt


def ssd_out(y, xs, z, d_skip, norm_w):
    b, t = y.shape[:2]
    y = (y + d_skip.astype(F32)[:, None] * xs).reshape(b, t, SSD_W) * jax.nn.silu(z.astype(F32))
    return rmsnorm(y, norm_w).astype(z.dtype)


def ssd_group(pl, pc, conv_w, conv_b, a_log, dt_bias, d_skip, norm_w, ctx_out):
    a = -jnp.exp(a_log.astype(F32))
    zl, xl, bl, cl, dtl = ssd_prep(pl, conv_w, conv_b, dt_bias)
    zc, xc, bc, cc, dtc = ssd_prep(pc, conv_w, conv_b, dt_bias)
    s0 = jnp.zeros((pc.shape[0], SSD_GROUPS, SSD_HEADS // SSD_GROUPS, SSD_HD, SSD_STATE), F32)
    yc_f, s_f = ssd_scan(xc, dtc[:, :, 0], a[0], bc, cc, s0)
    yc_b, s_b = ssd_scan_rev(xc, dtc[:, :, 1], a[1], bc, cc, s0)
    yl_f, _ = ssd_scan(xl, dtl[:, :, 0], a[0], bl, cl, s_f)
    yl_b, _ = ssd_scan_rev(xl, dtl[:, :, 1], a[1], bl, cl, s_b)
    out_l = ssd_out(yl_f + yl_b, xl, zl, d_skip, norm_w)
    out_c = ssd_out(yc_f + yc_b, xc, zc, d_skip, norm_w) if ctx_out else None
    return out_l, out_c


def parallel_mixer(hl, hc, w_in, w_out, ret_log_decay, ret_norm_w, attn_sink,
                   conv_w, conv_b, a_log, dt_bias, d_skip, ssd_norm_w, row, col, ctx_out):
    pl = hl @ w_in
    pc = hc @ w_in
    c1, c2 = RET_COLS, RET_COLS + ATT_COLS
    ra_l, ra_c = retention_group(pl[..., :c1], pc[..., :c1], ret_log_decay, ret_norm_w, ctx_out)
    at_l, at_c = window_gqa_group(pl[..., c1:c2], pc[..., c1:c2], attn_sink, row, col, ctx_out)
    ss_l, ss_c = ssd_group(pl[..., c2:], pc[..., c2:], conv_w, conv_b, a_log, dt_bias,
                           d_skip, ssd_norm_w, ctx_out)
    yl = jnp.concatenate([ra_l, at_l, ss_l], axis=-1) @ w_out
    yc = jnp.concatenate([ra_c, at_c, ss_c], axis=-1) @ w_out if ctx_out else None
    return yl, yc


def setup_inputs(seed: int = 0) -> dict:
    key = jax.random.key(seed)
    ks = jax.random.split(key, 24)

    def nrm(k, shape, s):
        return jax.random.normal(k, shape, F32) * s

    L = DEPTH
    ret_base = jnp.asarray(np.log1p(-2.0 ** (-5.0 - np.arange(RET_HEADS))), dtype=F32)
    dt0 = jnp.exp(jax.random.uniform(ks[19], (L, 2, SSD_HEADS), F32, math.log(1e-3), math.log(1e-1)))
    return {
        "x": nrm(ks[0], (BATCH, SEQ, D_MODEL), 1.0),
        "c": nrm(ks[1], (BATCH, D_MODEL), 1.0),
        "ctx": nrm(ks[2], (BATCH, CTX_LEN, D_MODEL), 1.0),
        "c_ctx": nrm(ks[3], (D_MODEL,), 1.0),
        "w_ada": nrm(ks[4], (L, D_MODEL, N_MOD * D_MODEL), 0.5 * D_MODEL ** -0.5),
        "b_ada": nrm(ks[5], (L, N_MOD * D_MODEL), 0.02),
        "norm_w": 1.0 + nrm(ks[6], (L, 6, D_MODEL), 0.05),
        "ffn1_gu": nrm(ks[7], (L, D_MODEL, 2 * D_FF), D_MODEL ** -0.5),
        "ffn1_down": nrm(ks[8], (L, D_FF, D_MODEL), D_FF ** -0.5),
        "ffn2_gu": nrm(ks[9], (L, D_MODEL, 2 * D_FF), D_MODEL ** -0.5),
        "ffn2_down": nrm(ks[10], (L, D_FF, D_MODEL), D_FF ** -0.5),
        "w_in": nrm(ks[11], (L, D_MODEL, IN_COLS), D_MODEL ** -0.5),
        "w_out": nrm(ks[12], (L, D_MIX, D_MODEL), D_MIX ** -0.5),
        "ret_log_decay": ret_base[None, None, :] * (1.0 + nrm(ks[13], (L, 2, RET_HEADS), 0.05)),
        "ret_norm_w": 1.0 + nrm(ks[14], (L, RET_W), 0.05),
        "attn_sink": nrm(ks[15], (L, ATT_HEADS), 0.5),
        "ssd_conv_w": nrm(ks[16], (L, SSD_CONV, SSD_CONV_CH), SSD_CONV ** -0.5),
        "ssd_conv_b": nrm(ks[17], (L, SSD_CONV_CH), 0.02),
        "ssd_a_log": jnp.log(jax.random.uniform(ks[18], (L, 2, SSD_HEADS), F32, 1.0, 16.0)),
        "ssd_dt_bias": dt0 + jnp.log(-jnp.expm1(-dt0)),
        "ssd_d": 1.0 + nrm(ks[20], (L, SSD_HEADS), 0.05),
        "ssd_norm_w": 1.0 + nrm(ks[21], (L, SSD_W), 0.05),
    }


def reference(x, c, ctx, c_ctx, w_ada, b_ada, norm_w, ffn1_gu, ffn1_down, ffn2_gu, ffn2_down,
              w_in, w_out, ret_log_decay, ret_norm_w, attn_sink, ssd_conv_w, ssd_conv_b,
              ssd_a_log, ssd_dt_bias, ssd_d, ssd_norm_w):
    t = x.shape[1]
    rows = t // GRID_W
    row = jnp.repeat(jnp.arange(rows), GRID_W)
    col = jnp.tile(jnp.arange(GRID_W), rows)
    xl, xc = x, ctx
    for layer in range(DEPTH):
        last = layer == DEPTH - 1
        nw = norm_w[layer]
        mod_l = adaln(c, w_ada[layer], b_ada[layer])
        mod_c = adaln(c_ctx[None, :], w_ada[layer], b_ada[layer])
        xl = ffn_sublayer(xl, mod_l, 0, nw, ffn1_gu[layer], ffn1_down[layer])
        xc = ffn_sublayer(xc, mod_c, 0, nw, ffn1_gu[layer], ffn1_down[layer])
        hl = modulated_norm(xl, mod_l, 1, nw[2])
        hc = modulated_norm(xc, mod_c, 1, nw[2])
        yl, yc = parallel_mixer(hl, hc, w_in[layer], w_out[layer], ret_log_decay[layer],
                                ret_norm_w[layer], attn_sink[layer], ssd_conv_w[layer],
                                ssd_conv_b[layer], ssd_a_log[layer], ssd_dt_bias[layer],
                                ssd_d[layer], ssd_norm_w[layer], row, col, not last)
        xl = gated_residual(xl, yl, mod_l, 1, nw[3], 1.0)
        xl = ffn_sublayer(xl, mod_l, 2, nw, ffn2_gu[layer], ffn2_down[layer])
        if not last:
            xc = gated_residual(xc, yc, mod_c, 1, nw[3], 1.0)
            xc = ffn_sublayer(xc, mod_c, 2, nw, ffn2_gu[layer], ffn2_down[layer])
    return xl
```

```python
import functools
import math

import jax
import jax.numpy as jnp
import numpy as np
from jax import lax
from jax.experimental import pallas as pl
from jax.experimental.pallas import tpu as pltpu

F32 = jnp.float32
BF16 = jnp.bfloat16

D_MODEL = 2048
GRID_W = 64
RET_HEADS = 4
RET_DK = 128
RET_W = 512
ATT_HEADS = 4
ATT_KV_HEADS = 2
ATT_HD = 128
ATT_W = 512
ATT_KV_W = 256
WINDOW = 128
SSD_HEADS = 16
SSD_HD = 64
SSD_W = 1024
SSD_GROUPS = 2
SSD_STATE = 128
SSD_BC_W = 256
CHUNK = 128
D_FF = 5632
FFN_RES = 0.5
ROPE_BASE = 10000.0
NORM_EPS = 1e-6
N_MOD = 9
IN_COLS = 5664
IN_COLS_PAD = 5760
DT_COL = 5632
NEG_BIG = -1e30

VMEM_LIMIT = 56 * 1024 * 1024


def _cparams(sem):
    return pltpu.CompilerParams(dimension_semantics=sem, vmem_limit_bytes=VMEM_LIMIT)


def _sigmoid(v):
    return 1.0 / (1.0 + jnp.exp(-v))


def _rms(v):
    return v * lax.rsqrt(jnp.mean(v * v, axis=-1, keepdims=True) + NORM_EPS)


def _dot(a, b):
    return jnp.dot(a, b, preferred_element_type=F32)


def _dot_nt(a, b):
    return lax.dot_general(a, b, (((1,), (1,)), ((), ())), preferred_element_type=F32)


def _dot_tn(a, b):
    return lax.dot_general(a, b, (((0,), (0,)), ((), ())), preferred_element_type=F32)


def _split3(v):
    hi = v.astype(BF16)
    r = v - hi.astype(F32)
    mid = r.astype(BF16)
    lo = (r - mid.astype(F32)).astype(BF16)
    return jnp.concatenate([hi, mid, lo], axis=1)


def _sum3(v, w):
    return v[:, :w] + v[:, w:2 * w] + v[:, 2 * w:3 * w]


def _adaln_kernel(c_ref, w_ref, b_ref, o_ref):
    cnd = c_ref[...]
    s = (cnd * _sigmoid(cnd)).astype(BF16)
    o_ref[...] = _dot(s, w_ref[0].astype(BF16)) + b_ref[0]


def _adaln(cond, w_ada, b_ada):
    depth, d, n = w_ada.shape
    tn = 1024
    nt = n // tn
    return pl.pallas_call(
        _adaln_kernel,
        grid=(depth, nt),
        in_specs=[
            pl.BlockSpec((8, d), lambda l, j: (0, 0)),
            pl.BlockSpec((1, d, tn), lambda l, j: (l, 0, j)),
            pl.BlockSpec((1, 1, tn), lambda l, j: (l, 0, j)),
        ],
        out_specs=pl.BlockSpec((8, tn), lambda l, j: (0, l * nt + j)),
        out_shape=jax.ShapeDtypeStruct((8, depth * n), F32),
        compiler_params=_cparams(("arbitrary", "arbitrary")),
        name="adaln",
    )(cond, w_ada, b_ada.reshape(depth, 1, n))


def _ffn_kernel(x_ref, mod_ref, nw_ref, wg_ref, wu_ref, wd_ref, o_ref, h_ref, acc_ref, *, sub):
    j = pl.program_id(1)

    @pl.when(j == 0)
    def _():
        h = _rms(x_ref[...]) * nw_ref[2 * sub:2 * sub + 1, :]
        h = h * (1.0 + mod_ref[0, 0, 1:2, :]) + mod_ref[0, 0, 0:1, :]
        h_ref[...] = h.astype(BF16)

    h = h_ref[...]
    g = _dot(h, wg_ref[...])
    u = _dot(h, wu_ref[...])
    a = (g * _sigmoid(g) * u).astype(BF16)
    d = _dot(a, wd_ref[...])

    @pl.when(j == 0)
    def _():
        acc_ref[...] = d

    @pl.when(j > 0)
    def _():
        acc_ref[...] += d

    @pl.when(j == pl.num_programs(1) - 1)
    def _():
        yn = _rms(acc_ref[...]) * nw_ref[2 * sub + 1:2 * sub + 2, :]
        o_ref[...] = x_ref[...] + FFN_RES * mod_ref[0, 0, 2:3, :] * yn


def _ffn(xs, rows_out, mod, nw, w_gu, w_down, *, layer, sub, tm, mod_row):
    d = xs.shape[1]
    tf = 512
    nf = D_FF // tf
    return pl.pallas_call(
        functools.partial(_ffn_kernel, sub=sub),
        grid=(rows_out // tm, nf),
        in_specs=[
            pl.BlockSpec((tm, d), lambda i, j: (i, 0)),
            pl.BlockSpec((1, 1, 3, d), lambda i, j: (layer * 3 + sub, mod_row(i), 0, 0)),
            pl.BlockSpec((6, d), lambda i, j: (0, 0)),
            pl.BlockSpec((d, tf), lambda i, j: (0, j)),
            pl.BlockSpec((d, tf), lambda i, j: (0, j + nf)),
            pl.BlockSpec((tf, d), lambda i, j: (j, 0)),
        ],
        out_specs=pl.BlockSpec((tm, d), lambda i, j: (i, 0)),
        out_shape=jax.ShapeDtypeStruct((rows_out, d), F32),
        scratch_shapes=[pltpu.VMEM((tm, d), BF16), pltpu.VMEM((tm, d), F32)],
        compiler_params=_cparams(("arbitrary", "arbitrary")),
        name=f"ffn{sub}",
    )(xs, mod, nw, w_gu, w_gu, w_down)


def _proj_in_kernel(x_ref, mod_ref, nw_ref, w_ref, o_ref, h_ref):
    @pl.when(pl.program_id(1) == 0)
    def _():
        h = _rms(x_ref[...]) * nw_ref[2:3, :]
        h = h * (1.0 + mod_ref[0, 0, 1:2, :]) + mod_ref[0, 0, 0:1, :]
        h_ref[...] = h.astype(BF16)

    o_ref[...] = _dot(h_ref[...], w_ref[...])


def _proj_in(xs, mod, nw, w_in, *, layer, tm, mod_row):
    rows, d = xs.shape
    n = w_in.shape[1]
    tn = 1152
    return pl.pallas_call(
        _proj_in_kernel,
        grid=(rows // tm, n // tn),
        in_specs=[
            pl.BlockSpec((tm, d), lambda i, j: (i, 0)),
            pl.BlockSpec((1, 1, 3, d), lambda i, j: (layer * 3 + 1, mod_row(i), 0, 0)),
            pl.BlockSpec((6, d), lambda i, j: (0, 0)),
            pl.BlockSpec((d, tn), lambda i, j: (0, j)),
        ],
        out_specs=pl.BlockSpec((tm, tn), lambda i, j: (i, j)),
        out_shape=jax.ShapeDtypeStruct((rows, n), F32),
        scratch_shapes=[pltpu.VMEM((tm, d), BF16)],
        compiler_params=_cparams(("arbitrary", "arbitrary")),
        name="proj_in",
    )(xs, mod, nw, w_in)


def _proj_out_kernel(x_ref, ra_ref, at_ref, ss_ref, w_ref, mod_ref, nw_ref, o_ref):
    y = _dot(ra_ref[...], w_ref[0:RET_W, :])
    y = y + _dot(at_ref[...], w_ref[RET_W:RET_W + ATT_W, :])
    y = y + _dot(ss_ref[...], w_ref[RET_W + ATT_W:, :])
    yn = _rms(y) * nw_ref[3:4, :]
    o_ref[...] = x_ref[...] + mod_ref[0, 0, 2:3, :] * yn


def _proj_out(xs, rows_out, ra, at, ss, w_out, mod, nw, *, layer, tm, mod_row):
    d = xs.shape[1]
    return pl.pallas_call(
        _proj_out_kernel,
        grid=(rows_out // tm,),
        in_specs=[
            pl.BlockSpec((tm, d), lambda i: (i, 0)),
            pl.BlockSpec((tm, RET_W), lambda i: (i, 0)),
            pl.BlockSpec((tm, ATT_W), lambda i: (i, 0)),
            pl.BlockSpec((tm, SSD_W), lambda i: (i, 0)),
            pl.BlockSpec((d, d), lambda i: (0, 0)),
            pl.BlockSpec((1, 1, 3, d), lambda i: (layer * 3 + 1, mod_row(i), 0, 0)),
            pl.BlockSpec((6, d), lambda i: (0, 0)),
        ],
        out_specs=pl.BlockSpec((tm, d), lambda i: (i, 0)),
        out_shape=jax.ShapeDtypeStruct((rows_out, d), F32),
        compiler_params=_cparams(("arbitrary",)),
        name="proj_out",
    )(xs, ra, at, ss, w_out, mod, nw)


class _Geom:
    def __init__(self, b, t, lc):
        self.b, self.t, self.lc = b, t, lc
        self.nlat = t // CHUNK
        self.nctx = lc // CHUNK
        self.nch = self.nlat + self.nctx
        self.rows = b * (t + lc)

    def row_block(self, bi, j):
        return jnp.where(j < self.nctx,
                         self.b * self.nlat + bi * self.nctx + j,
                         bi * self.nlat + (j - self.nctx))

    def bwd_chunk(self, s):
        return jnp.where(s < self.nctx, self.nctx - 1 - s, self.nch + self.nctx - 1 - s)

    def seg_first(self, j):
        return (j == 0) | (j == self.nctx)

    def seg_last(self, j):
        return (j == self.nctx - 1) | (j == self.nch - 1)


def _iota2(shape, dim):
    return lax.broadcasted_iota(jnp.int32, shape, dim)


def _ret_bstate_kernel(k_ref, v_ref, ld_ref, sb_ref, s_ref):
    @pl.when(pl.program_id(1) == 0)
    def _():
        s_ref[...] = jnp.zeros_like(s_ref)

    sb_ref[0, 0] = s_ref[...]
    jrow = _iota2((CHUNK, CHUNK), 0).astype(F32)
    for h in range(RET_HEADS):
        hs = slice(h * RET_DK, (h + 1) * RET_DK)
        lgb = -jnp.abs(ld_ref[1, h:h + 1, :])
        kd = (k_ref[:, hs] * jnp.exp(lgb * jrow)).astype(BF16)
        s_ref[h] = s_ref[h] * jnp.exp(lgb * float(CHUNK)) + _dot_tn(kd, v_ref[:, hs].astype(BF16))


def _ret_fwd_kernel(q_ref, k_ref, v_ref, g_ref, sb_ref, ld_ref, nw_ref, o_ref, s_ref):
    @pl.when(pl.program_id(1) == 0)
    def _():
        s_ref[...] = jnp.zeros_like(s_ref)

    irow = _iota2((CHUNK, CHUNK), 0).astype(F32)
    rel = irow - _iota2((CHUNK, CHUNK), 1).astype(F32)
    for h in range(RET_HEADS):
        hs = slice(h * RET_DK, (h + 1) * RET_DK)
        lgf = -jnp.abs(ld_ref[0, h:h + 1, :])
        lgb = -jnp.abs(ld_ref[1, h:h + 1, :])
        q = q_ref[:, hs] * (RET_DK ** -0.5)
        k = k_ref[:, hs]
        vb = v_ref[:, hs].astype(BF16)
        dmask = (jnp.where(rel >= 0, jnp.exp(lgf * jnp.maximum(rel, 0.0)), 0.0)
                 + jnp.where(rel <= 0, jnp.exp(lgb * jnp.maximum(-rel, 0.0)), 0.0))
        inner = _dot_nt(q.astype(BF16), k.astype(BF16)) * dmask
        y = _dot(inner.astype(BF16), vb)
        qf = (q * jnp.exp(lgf * (irow + 1.0))).astype(BF16)
        qb = (q * jnp.exp(lgb * (float(CHUNK) - irow))).astype(BF16)
        y = y + _dot(qf, s_ref[h].astype(BF16)) + _dot(qb, sb_ref[0, 0, h].astype(BF16))
        kd = (k * jnp.exp(lgf * (float(CHUNK) - 1.0 - irow))).astype(BF16)
        s_ref[h] = s_ref[h] * jnp.exp(lgf * float(CHUNK)) + _dot_tn(kd, vb)
        mu = jnp.mean(y, axis=-1, keepdims=True)
        yc = y - mu
        var = jnp.mean(yc * yc, axis=-1, keepdims=True)
        gate = g_ref[:, hs]
        yn = yc * lax.rsqrt(var + NORM_EPS) * nw_ref[:, hs]
        o_ref[:, hs] = (yn * (gate * _sigmoid(gate))).astype(BF16)


def _retention(p, geom, ld, nw):
    b, nch = geom.b, geom.nch
    ld_spec = pl.BlockSpec((2, RET_HEADS, 128), lambda bi, s: (0, 0, 0))
    state = pltpu.VMEM((RET_HEADS, RET_DK, RET_DK), F32)

    def blk(col, order):
        return pl.BlockSpec((CHUNK, RET_W), lambda bi, s: (geom.row_block(bi, order(s)), col))

    sb = pl.pallas_call(
        _ret_bstate_kernel,
        grid=(b, nch),
        in_specs=[blk(1, geom.bwd_chunk), blk(2, geom.bwd_chunk), ld_spec],
        out_specs=pl.BlockSpec((1, 1, RET_HEADS, RET_DK, RET_DK),
                               lambda bi, s: (bi, geom.bwd_chunk(s), 0, 0, 0)),
        out_shape=jax.ShapeDtypeStruct((b, nch, RET_HEADS, RET_DK, RET_DK), F32),
        scratch_shapes=[state],
        compiler_params=_cparams(("arbitrary", "arbitrary")),
        name="ret_bstate",
    )(p, p, ld)

    ident = lambda s: s
    return pl.pallas_call(
        _ret_fwd_kernel,
        grid=(b, nch),
        in_specs=[blk(0, ident), blk(1, ident), blk(2, ident), blk(3, ident),
                  pl.BlockSpec((1, 1, RET_HEADS, RET_DK, RET_DK), lambda bi, s: (bi, s, 0, 0, 0)),
                  ld_spec,
                  pl.BlockSpec((1, RET_W), lambda bi, s: (0, 0))],
        out_specs=pl.BlockSpec((CHUNK, RET_W), lambda bi, s: (geom.row_block(bi, s), 0)),
        out_shape=jax.ShapeDtypeStruct((geom.rows, RET_W), BF16),
        scratch_shapes=[state],
        compiler_params=_cparams(("arbitrary", "arbitrary")),
        name="ret_fwd",
    )(p, p, p, p, sb, ld, nw)


def _rope(x, cos, sin):
    lane = _iota2(x.shape, 1)
    swapped = jnp.where((lane // 32) % 2 == 0, pltpu.roll(x, 96, axis=1), pltpu.roll(x, 32, axis=1))
    return x * cos + swapped * sin


def _att_kernel(q_ref, kl_ref, vl_ref, kc_ref, vc_ref, cos_ref, sin_ref, sink_ref, o_ref,
                kr_ref, vb_ref, *, t):
    n = pl.program_id(1)
    nb = pl.num_programs(1)
    blk = CHUNK

    @pl.when(n == 0)
    def _():
        zeros = jnp.zeros((blk, ATT_KV_W), BF16)
        kr_ref[0:blk, :] = zeros
        kr_ref[t + blk:t + 2 * blk, :] = zeros
        vb_ref[0:blk, :] = zeros
        vb_ref[t + blk:t + 2 * blk, :] = zeros
        for h in range(ATT_KV_HEADS):
            hs = slice(h * ATT_HD, (h + 1) * ATT_HD)
            kr_ref[blk:t + blk, hs] = _rope(kl_ref[:, hs], cos_ref[...], sin_ref[...]).astype(BF16)
        vb_ref[blk:t + blk, :] = vl_ref[...].astype(BF16)

    row0 = pl.multiple_of(n * blk, blk)
    cos = cos_ref[pl.ds(row0, blk), :]
    sin = sin_ref[pl.ds(row0, blk), :]
    scale = ATT_HD ** -0.5

    qi = _iota2((2 * blk, 3 * blk), 0) % blk
    kj = _iota2((2 * blk, 3 * blk), 1)
    rel = kj - blk - qi
    valid = (jnp.abs(rel) <= WINDOW)
    valid = valid & ((kj >= blk) | (n > 0)) & ((kj < 2 * blk) | (n < nb - 1))
    first_head = _iota2((2 * blk, 1), 0) < blk

    for h in range(ATT_KV_HEADS):
        hs = slice(h * ATT_HD, (h + 1) * ATT_HD)
        q0 = _rope(q_ref[:, 2 * h * ATT_HD:(2 * h + 1) * ATT_HD], cos, sin)
        q1 = _rope(q_ref[:, (2 * h + 1) * ATT_HD:(2 * h + 2) * ATT_HD], cos, sin)
        q2 = jnp.concatenate([q0, q1], axis=0).astype(BF16)
        kw = kr_ref[pl.ds(row0, 3 * blk), hs]
        vw = vb_ref[pl.ds(row0, 3 * blk), hs]
        s_loc = jnp.where(valid, _dot_nt(q2, kw) * scale, NEG_BIG)
        s_cx = _dot_nt(q2, kc_ref[:, hs].astype(BF16)) * scale
        sink = jnp.where(first_head, sink_ref[2 * h], sink_ref[2 * h + 1])
        m = jnp.maximum(jnp.maximum(jnp.max(s_loc, axis=-1, keepdims=True),
                                    jnp.max(s_cx, axis=-1, keepdims=True)), sink)
        e_loc = jnp.exp(s_loc - m)
        e_cx = jnp.exp(s_cx - m)
        den = (jnp.sum(e_loc, axis=-1, keepdims=True) + jnp.sum(e_cx, axis=-1, keepdims=True)
               + jnp.exp(sink - m))
        o = _dot(e_loc.astype(BF16), vw) + _dot(e_cx.astype(BF16), vc_ref[:, hs].astype(BF16))
        o = o / den
        o_ref[:, 2 * h * ATT_HD:(2 * h + 1) * ATT_HD] = o[0:blk].astype(BF16)
        o_ref[:, (2 * h + 1) * ATT_HD:(2 * h + 2) * ATT_HD] = o[blk:2 * blk].astype(BF16)


def _att_ctx_kernel(q_ref, kc_ref, vc_ref, sink_ref, o_ref):
    lc = q_ref.shape[0]
    scale = ATT_HD ** -0.5
    first_head = _iota2((2 * lc, 1), 0) < lc
    for h in range(ATT_KV_HEADS):
        hs = slice(h * ATT_HD, (h + 1) * ATT_HD)
        q2 = jnp.concatenate([q_ref[:, 2 * h * ATT_HD:(2 * h + 1) * ATT_HD],
                              q_ref[:, (2 * h + 1) * ATT_HD:(2 * h + 2) * ATT_HD]], axis=0).astype(BF16)
        s = _dot_nt(q2, kc_ref[:, hs].astype(BF16)) * scale
        sink = jnp.where(first_head, sink_ref[2 * h], sink_ref[2 * h + 1])
        m = jnp.maximum(jnp.max(s, axis=-1, keepdims=True), sink)
        e = jnp.exp(s - m)
        den = jnp.sum(e, axis=-1, keepdims=True) + jnp.exp(sink - m)
        o = _dot(e.astype(BF16), vc_ref[:, hs].astype(BF16)) / den
        o_ref[:, 2 * h * ATT_HD:(2 * h + 1) * ATT_HD] = o[0:lc].astype(BF16)
        o_ref[:, (2 * h + 1) * ATT_HD:(2 * h + 2) * ATT_HD] = o[lc:2 * lc].astype(BF16)


def _attention(p, geom, sink, cos, sin, ctx_out):
    b, t, lc = geom.b, geom.t, geom.lc
    nb = t // CHUNK
    ctx0 = (b * t) // lc
    qcol = 2048 // ATT_W
    kcol = (2048 + ATT_W) // ATT_KV_W
    vcol = kcol + 1
    smem = pl.BlockSpec(memory_space=pltpu.SMEM)
    out_l = pl.pallas_call(
        functools.partial(_att_kernel, t=t),
        grid=(b, nb),
        in_specs=[
            pl.BlockSpec((CHUNK, ATT_W), lambda bi, n: (bi * nb + n, qcol)),
            pl.BlockSpec((t, ATT_KV_W), lambda bi, n: (bi, kcol)),
            pl.BlockSpec((t, ATT_KV_W), lambda bi, n: (bi, vcol)),
            pl.BlockSpec((lc, ATT_KV_W), lambda bi, n: (ctx0 + bi, kcol)),
            pl.BlockSpec((lc, ATT_KV_W), lambda bi, n: (ctx0 + bi, vcol)),
            pl.BlockSpec((t, ATT_HD), lambda bi, n: (0, 0)),
            pl.BlockSpec((t, ATT_HD), lambda bi, n: (0, 0)),
            smem,
        ],
        out_specs=pl.BlockSpec((CHUNK, ATT_W), lambda bi, n: (bi * nb + n, 0)),
        out_shape=jax.ShapeDtypeStruct((b * t, ATT_W), BF16),
        scratch_shapes=[pltpu.VMEM((t + 2 * CHUNK, ATT_KV_W), BF16),
                        pltpu.VMEM((t + 2 * CHUNK, ATT_KV_W), BF16)],
        compiler_params=_cparams(("arbitrary", "arbitrary")),
        name="att_lat",
    )(p, p, p, p, p, cos, sin, sink)
    if not ctx_out:
        return out_l, None
    out_c = pl.pallas_call(
        _att_ctx_kernel,
        grid=(b,),
        in_specs=[
            pl.BlockSpec((lc, ATT_W), lambda bi: (ctx0 + bi, qcol)),
            pl.BlockSpec((lc, ATT_KV_W), lambda bi: (ctx0 + bi, kcol)),
            pl.BlockSpec((lc, ATT_KV_W), lambda bi: (ctx0 + bi, vcol)),
            smem,
        ],
        out_specs=pl.BlockSpec((lc, ATT_W), lambda bi: (bi, 0)),
        out_shape=jax.ShapeDtypeStruct((b * lc, ATT_W), BF16),
        compiler_params=_cparams(("arbitrary",)),
        name="att_ctx",
    )(p, p, p, sink)
    return out_l, out_c


def _conv_silu(u, prev_row, next_row, w, bias):
    rows = _iota2(u.shape, 0)
    up = jnp.where(rows == 0, prev_row, pltpu.roll(u, 1, axis=0))
    un = jnp.where(rows == CHUNK - 1, next_row, pltpu.roll(u, CHUNK - 1, axis=0))
    y = w[0:1, :] * up + w[1:2, :] * u + w[2:3, :] * un + bias
    return y * _sigmoid(y)


def _softplus(v):
    return jnp.maximum(v, 0.0) + jnp.log1p(jnp.exp(-jnp.abs(v)))


def _ssd_dt_la(dt_ref, alog_ref, dtb_ref):
    lane = _iota2((1, 128), 1)
    a_row = jnp.where(lane < 2 * SSD_HEADS, -jnp.exp(alog_ref[...]), 0.0)
    dt = _softplus(dt_ref[...] + dtb_ref[...])
    return dt, dt * a_row


def _tri(upper):
    r = _iota2((CHUNK, CHUNK), 0)
    c = _iota2((CHUNK, CHUNK), 1)
    return jnp.where((c >= r) if upper else (c <= r), 1.0, 0.0).astype(BF16)


def _ssd_bstate_kernel(x_ref, xp_ref, xn_ref, bc_ref, bp_ref, bn_ref, dt_ref, cw_ref, cb_ref,
                       alog_ref, dtb_ref, exp_ref, sb_ref, s_ref, *, geom):
    s = pl.program_id(1)
    j = geom.bwd_chunk(s)

    @pl.when(s == 0)
    def _():
        s_ref[...] = jnp.zeros_like(s_ref)

    sb_ref[0, 0] = s_ref[...]
    first = geom.seg_first(j)
    last = geom.seg_last(j)
    xs = _conv_silu(x_ref[...],
                    jnp.where(first, 0.0, xp_ref[7:8, :]), jnp.where(last, 0.0, xn_ref[0:1, :]),
                    cw_ref[:, 0:SSD_W], cb_ref[:, 0:SSD_W])
    bm = _conv_silu(bc_ref[:, 0:SSD_BC_W],
                    jnp.where(first, 0.0, bp_ref[7:8, 0:SSD_BC_W]),
                    jnp.where(last, 0.0, bn_ref[0:1, 0:SSD_BC_W]),
                    cw_ref[:, SSD_W:SSD_W + SSD_BC_W], cb_ref[:, SSD_W:SSD_W + SSD_BC_W])
    dt, la = _ssd_dt_la(dt_ref, alog_ref, dtb_ref)
    rb = _sum3(_dot(_tri(True), _split3(la)), 128)
    ex = exp_ref[:, SSD_W:2 * SSD_W]
    rbx = _dot(_split3(rb), ex)
    dtx = _dot(_split3(dt), ex)
    xw = (xs * dtx * jnp.exp(rbx[0:1, :] - rbx)).astype(BF16)
    cdec = jnp.exp(rbx[0:1, :])
    gw = SSD_W // SSD_GROUPS
    for g in range(SSD_GROUPS):
        gs = slice(g * gw, (g + 1) * gw)
        bg = bm[:, g * SSD_STATE:(g + 1) * SSD_STATE].astype(BF16)
        s_ref[g] = s_ref[g] * cdec[:, gs] + _dot_tn(bg, xw[:, gs])


def _ssd_fwd_kernel(z_ref, x_ref, xp_ref, xn_ref, bc_ref, bp_ref, bn_ref, dt_ref, sb_ref, cw_ref, cb_ref,
                    alog_ref, dtb_ref, dsk_ref, nw_ref, exp_ref, expl_ref, o_ref, s_ref, *, geom):
    j = pl.program_id(1)

    @pl.when(j == 0)
    def _():
        s_ref[...] = jnp.zeros_like(s_ref)

    first = geom.seg_first(j)
    last = geom.seg_last(j)
    xs = _conv_silu(x_ref[...],
                    jnp.where(first, 0.0, xp_ref[7:8, :]), jnp.where(last, 0.0, xn_ref[0:1, :]),
                    cw_ref[:, 0:SSD_W], cb_ref[:, 0:SSD_W])
    bcm = _conv_silu(bc_ref[...],
                     jnp.where(first, 0.0, bp_ref[7:8, :]), jnp.where(last, 0.0, bn_ref[0:1, :]),
                     cw_ref[:, SSD_W:], cb_ref[:, SSD_W:])
    dt, la = _ssd_dt_la(dt_ref, alog_ref, dtb_ref)
    la3 = _split3(la)
    lane = _iota2((CHUNK, 128), 1)
    cum = jnp.where(lane < SSD_HEADS,
                    _sum3(_dot(_tri(False), la3), 128),
                    _sum3(_dot(_tri(True), la3), 128))
    cum_t = cum.T
    cum3 = _split3(cum)
    cumx = _dot(cum3, exp_ref[...])
    dtx = _dot(_split3(dt), exp_ref[...])
    xdt_f = xs * dtx[:, 0:SSD_W]
    xdt_b = xs * dtx[:, SSD_W:]

    r = _iota2((CHUNK, CHUNK), 0)
    c = _iota2((CHUNK, CHUNK), 1)
    lower = r >= c
    upper = c >= r
    lane64 = lane < SSD_HD
    gw = SSD_W // SSD_GROUPS
    hpg = SSD_HEADS // SSD_GROUPS
    y_parts = []
    cbs = []
    cgs = []
    for g in range(SSD_GROUPS):
        bg = bcm[:, g * SSD_STATE:(g + 1) * SSD_STATE].astype(BF16)
        cg = bcm[:, SSD_BC_W + g * SSD_STATE:SSD_BC_W + (g + 1) * SSD_STATE].astype(BF16)
        cgs.append((bg, cg))
        cbs.append(_dot_nt(cg, bg))
    for hp in range(SSD_HEADS // 2):
        g = (2 * hp) // hpg
        lhs = []
        rhs = []
        for e in range(2):
            h = 2 * hp + e
            col_f = _dot(cum3, expl_ref[:, h * 128:(h + 1) * 128])
            col_b = _dot(cum3, expl_ref[:, (SSD_HEADS + h) * 128:(SSD_HEADS + h + 1) * 128])
            lf = jnp.exp(jnp.where(lower, col_f - cum_t[h:h + 1, :], NEG_BIG))
            lb = jnp.exp(jnp.where(upper, col_b - cum_t[SSD_HEADS + h:SSD_HEADS + h + 1, :], NEG_BIG))
            lhs += [(cbs[g] * lf).astype(BF16), (cbs[g] * lb).astype(BF16)]
            keep = lane64 if e == 0 else jnp.logical_not(lane64)
            ps = slice(hp * 128, (hp + 1) * 128)
            rhs += [jnp.where(keep, xdt_f[:, ps], 0.0).astype(BF16),
                    jnp.where(keep, xdt_b[:, ps], 0.0).astype(BF16)]
        y_parts.append(_dot(jnp.concatenate(lhs, axis=1), jnp.concatenate(rhs, axis=0)))
    y = jnp.concatenate(y_parts, axis=1)

    e_f = jnp.exp(cumx[:, 0:SSD_W])
    e_b = jnp.exp(cumx[:, SSD_W:])
    yoff_f = jnp.concatenate([_dot(cgs[g][1], s_ref[g].astype(BF16)) for g in range(SSD_GROUPS)], axis=1)
    yoff_b = jnp.concatenate([_dot(cgs[g][1], sb_ref[0, 0, g].astype(BF16)) for g in range(SSD_GROUPS)], axis=1)
    y = y + yoff_f * e_f + yoff_b * e_b

    tot = cumx[CHUNK - 1:CHUNK, 0:SSD_W]
    xw = (xdt_f * jnp.exp(tot - cumx[:, 0:SSD_W])).astype(BF16)
    cdec = jnp.exp(tot)
    for g in range(SSD_GROUPS):
        gs = slice(g * gw, (g + 1) * gw)
        s_ref[g] = s_ref[g] * cdec[:, gs] + _dot_tn(cgs[g][0], xw[:, gs])

    z = z_ref[...]
    y = (y + dsk_ref[...] * xs) * (z * _sigmoid(z))
    o_ref[...] = (_rms(y) * nw_ref[...]).astype(BF16)


def _expand_consts():
    k = np.arange(128)
    ex = np.zeros((128, 2 * SSD_W), np.float32)
    kk = np.arange(2 * SSD_HEADS)
    for p in range(SSD_HD):
        ex[kk, kk * SSD_HD + p] = 1.0
    exl = np.zeros((128, 2 * SSD_HEADS * 128), np.float32)
    for q in range(128):
        exl[kk, kk * 128 + q] = 1.0
    del k
    ex = np.concatenate([ex, ex, ex], axis=0)
    exl = np.concatenate([exl, exl, exl], axis=0)
    return jnp.asarray(ex, BF16), jnp.asarray(exl, BF16)


def _ssd(p, geom, conv_w, conv_b, alog, dtb, dsk, nw):
    b, nch = geom.b, geom.nch
    ex, exl = _expand_consts()
    zcol = 3072 // SSD_W
    xcol = 4096 // SSD_W
    bccol = (4096 + SSD_W) // (2 * SSD_BC_W)
    dtcol = DT_COL // 128
    n8 = geom.rows // 8
    sub = CHUNK // 8
    gw = SSD_W // SSD_GROUPS

    def specs(order):
        rb = lambda bi, s: geom.row_block(bi, order(s))
        prev8 = lambda bi, s: jnp.maximum(rb(bi, s) * sub - 1, 0)
        next8 = lambda bi, s: jnp.minimum(rb(bi, s) * sub + sub, n8 - 1)
        return dict(
            z=pl.BlockSpec((CHUNK, SSD_W), lambda bi, s: (rb(bi, s), zcol)),
            x=pl.BlockSpec((CHUNK, SSD_W), lambda bi, s: (rb(bi, s), xcol)),
            xp=pl.BlockSpec((8, SSD_W), lambda bi, s: (prev8(bi, s), xcol)),
            xn=pl.BlockSpec((8, SSD_W), lambda bi, s: (next8(bi, s), xcol)),
            bc=pl.BlockSpec((CHUNK, 2 * SSD_BC_W), lambda bi, s: (rb(bi, s), bccol)),
            bp=pl.BlockSpec((8, 2 * SSD_BC_W), lambda bi, s: (prev8(bi, s), bccol)),
            bn=pl.BlockSpec((8, 2 * SSD_BC_W), lambda bi, s: (next8(bi, s), bccol)),
            dt=pl.BlockSpec((CHUNK, 128), lambda bi, s: (rb(bi, s), dtcol)),
            out=pl.BlockSpec((CHUNK, SSD_W), lambda bi, s: (rb(bi, s), 0)),
        )

    def const(shape):
        return pl.BlockSpec(shape, lambda bi, s: (0,) * len(shape))

    state = pltpu.VMEM((SSD_GROUPS, SSD_STATE, gw), F32)
    sp = specs(geom.bwd_chunk)
    sb = pl.pallas_call(
        functools.partial(_ssd_bstate_kernel, geom=geom),
        grid=(b, nch),
        in_specs=[sp["x"], sp["xp"], sp["xn"], sp["bc"], sp["bp"], sp["bn"], sp["dt"],
                  const(conv_w.shape), const(conv_b.shape), const(alog.shape), const(dtb.shape),
                  const(ex.shape)],
        out_specs=pl.BlockSpec((1, 1, SSD_GROUPS, SSD_STATE, gw),
                               lambda bi, s: (bi, geom.bwd_chunk(s), 0, 0, 0)),
        out_shape=jax.ShapeDtypeStruct((b, nch, SSD_GROUPS, SSD_STATE, gw), F32),
        scratch_shapes=[state],
        compiler_params=_cparams(("arbitrary", "arbitrary")),
        name="ssd_bstate",
    )(p, p, p, p, p, p, p, conv_w, conv_b, alog, dtb, ex)

    sp = specs(lambda s: s)
    return pl.pallas_call(
        functools.partial(_ssd_fwd_kernel, geom=geom),
        grid=(b, nch),
        in_specs=[sp["z"], sp["x"], sp["xp"], sp["xn"], sp["bc"], sp["bp"], sp["bn"], sp["dt"],
                  pl.BlockSpec((1, 1, SSD_GROUPS, SSD_STATE, gw), lambda bi, s: (bi, s, 0, 0, 0)),
                  const(conv_w.shape), const(conv_b.shape), const(alog.shape), const(dtb.shape),
                  const(dsk.shape), const(nw.shape), const(ex.shape), const(exl.shape)],
        out_specs=sp["out"],
        out_shape=jax.ShapeDtypeStruct((geom.rows, SSD_W), BF16),
        scratch_shapes=[state],
        compiler_params=_cparams(("arbitrary", "arbitrary")),
        name="ssd_fwd",
    )(p, p, p, p, p, p, p, p, sb, conv_w, conv_b, alog, dtb, dsk, nw, ex, exl)


def _rope_tables(t):
    half = ATT_HD // 2
    freqs = ROPE_BASE ** (-jnp.arange(0, half, 2, dtype=F32) / half)
    pos = jnp.arange(t)
    ang_r = (pos // GRID_W).astype(F32)[:, None] * freqs[None, :]
    ang_c = (pos % GRID_W).astype(F32)[:, None] * freqs[None, :]
    cos = jnp.concatenate([jnp.cos(ang_r), jnp.cos(ang_r), jnp.cos(ang_c), jnp.cos(ang_c)], axis=-1)
    sin = jnp.concatenate([-jnp.sin(ang_r), jnp.sin(ang_r), -jnp.sin(ang_c), jnp.sin(ang_c)], axis=-1)
    return cos, sin


def _pad_lanes(v, n=128):
    v = v.reshape(1, -1)
    return jnp.pad(v, ((0, 0), (0, n - v.shape[1])))


def _forward(x, c, ctx, c_ctx, w_ada, b_ada, norm_w, ffn1_gu, ffn1_down, ffn2_gu, ffn2_down,
             w_in, w_out, ret_log_decay, ret_norm_w, attn_sink, ssd_conv_w, ssd_conv_b,
             ssd_a_log, ssd_dt_bias, ssd_d, ssd_norm_w):
    b, t, d = x.shape
    lc = ctx.shape[1]
    depth = w_ada.shape[0]
    geom = _Geom(b, t, lc)
    n_lat = b * t
    tm = math.gcd(1024, math.gcd(t, b * lc))
    tm_out = math.gcd(512, tm)

    def mod_row(tile):
        def f(i):
            r0 = i * tile
            return jnp.where(r0 < n_lat, 1 + r0 // t, 0)
        return f

    cond = jnp.concatenate([c_ctx[None, :], c, jnp.zeros((8 - 1 - b, d), F32)], axis=0)
    mod = _adaln(cond, w_ada, b_ada)
    mod = mod.reshape(8, depth, 3, 3, d).transpose(1, 2, 0, 3, 4).reshape(depth * 3, 8, 3, d)

    cos, sin = _rope_tables(t)
    xs = jnp.concatenate([x.reshape(n_lat, d), ctx.reshape(b * lc, d)], axis=0)
    for layer in range(depth):
        last = layer == depth - 1
        nw = norm_w[layer]
        w1gu = ffn1_gu[layer].astype(BF16)
        w1d = ffn1_down[layer].astype(BF16)
        w2gu = ffn2_gu[layer].astype(BF16)
        w2d = ffn2_down[layer].astype(BF16)
        wi = jnp.pad(w_in[layer].astype(BF16), ((0, 0), (0, IN_COLS_PAD - IN_COLS)))
        wo = w_out[layer].astype(BF16)

        xs = _ffn(xs, xs.shape[0], mod, nw, w1gu, w1d, layer=layer, sub=0, tm=tm_out, mod_row=mod_row(tm_out))
        p = _proj_in(xs, mod, nw, wi, layer=layer, tm=tm, mod_row=mod_row(tm))

        ld = jnp.broadcast_to(ret_log_decay[layer][:, :, None], (2, RET_HEADS, 128))
        ra = _retention(p, geom, ld, ret_norm_w[layer].reshape(1, RET_W))
        at_l, at_c = _attention(p, geom, attn_sink[layer], cos, sin, not last)
        ss = _ssd(p, geom, ssd_conv_w[layer], ssd_conv_b[layer].reshape(1, -1),
                  _pad_lanes(ssd_a_log[layer]), _pad_lanes(ssd_dt_bias[layer]),
                  jnp.repeat(ssd_d[layer], SSD_HD).reshape(1, SSD_W),
                  ssd_norm_w[layer].reshape(1, SSD_W))
        at = at_l if last else jnp.concatenate([at_l, at_c], axis=0)
        rows_out = n_lat if last else xs.shape[0]
        xs = _proj_out(xs, rows_out, ra, at, ss, wo, mod, nw, layer=layer, tm=tm_out, mod_row=mod_row(tm_out))
        xs = _ffn(xs, rows_out, mod, nw, w2gu, w2d, layer=layer, sub=2, tm=tm_out, mod_row=mod_row(tm_out))
    return xs.reshape(b, t, d)


def kernel(x, c, ctx, c_ctx, w_ada, b_ada, norm_w, ffn1_gu, ffn1_down, ffn2_gu, ffn2_down,
           w_in, w_out, ret_log_decay, ret_norm_w, attn_sink, ssd_conv_w, ssd_conv_b,
           ssd_a_log, ssd_dt_bias, ssd_d, ssd_norm_w):
    return _forward(x, c, ctx, c_ctx, w_ada, b_ada, norm_w, ffn1_gu, ffn1_down, ffn2_gu, ffn2_down,
                    w_in, w_out, ret_log_decay, ret_norm_w, attn_sink, ssd_conv_w, ssd_conv_b,
                    ssd_a_log, ssd_dt_bias, ssd_d, ssd_norm_w)
```

```python
import functools
import math

import jax
import jax.numpy as jnp
import numpy as np
from jax import lax
from jax.experimental import pallas as pl
from jax.experimental.pallas import tpu as pltpu

F32 = jnp.float32
BF16 = jnp.bfloat16

D_MODEL = 2048
GRID_W = 64
RET_HEADS = 4
RET_DK = 128
RET_W = 512
ATT_HEADS = 4
ATT_KV_HEADS = 2
ATT_HD = 128
ATT_W = 512
ATT_KV_W = 256
WINDOW = 128
SSD_HEADS = 16
SSD_HD = 64
SSD_W = 1024
SSD_GROUPS = 2
SSD_STATE = 128
SSD_BC_W = 256
CHUNK = 128
D_FF = 5632
FFN_RES = 0.5
ROPE_BASE = 10000.0
NORM_EPS = 1e-6
N_MOD = 9
IN_MAIN = 5632
NEG_BIG = -1e30

CPS = 2
RS = CHUNK * CPS
LANES = 128
SUBLANES = 8

VMEM_LIMIT = 56 * 1024 * 1024


def _cparams(sem):
    return pltpu.CompilerParams(dimension_semantics=sem, vmem_limit_bytes=VMEM_LIMIT)


def _sigmoid(v):
    return 1.0 / (1.0 + jnp.exp(-v))


def _rms(v):
    return v * lax.rsqrt(jnp.mean(v * v, axis=-1, keepdims=True) + NORM_EPS)


def _dot(a, b):
    return jnp.dot(a, b, preferred_element_type=F32)


def _dot_nt(a, b):
    return lax.dot_general(a, b, (((1,), (1,)), ((), ())), preferred_element_type=F32)


def _dot_tn(a, b):
    return lax.dot_general(a, b, (((0,), (0,)), ((), ())), preferred_element_type=F32)


def _iota2(shape, dim):
    return lax.broadcasted_iota(jnp.int32, shape, dim)


def _adaln_kernel(c_ref, w_ref, b_ref, o_ref):
    cnd = c_ref[...]
    s = (cnd * _sigmoid(cnd)).astype(BF16)
    o_ref[...] = _dot(s, w_ref[...].astype(BF16)) + b_ref[...]


def _adaln(cond, w_ada, b_ada):
    depth, d, n = w_ada.shape
    tn = 1024
    nt = n // tn
    return pl.pallas_call(
        _adaln_kernel,
        grid=(depth, nt),
        in_specs=[
            pl.BlockSpec((SUBLANES, d), lambda l, j: (0, 0)),
            pl.BlockSpec((None, d, tn), lambda l, j: (l, 0, j)),
            pl.BlockSpec((None, 1, tn), lambda l, j: (l, 0, j)),
        ],
        out_specs=pl.BlockSpec((SUBLANES, tn), lambda l, j: (0, l * nt + j)),
        out_shape=jax.ShapeDtypeStruct((SUBLANES, depth * n), F32),
        compiler_params=_cparams(("arbitrary", "arbitrary")),
        name="adaln",
    )(cond, w_ada, b_ada.reshape(depth, 1, n))


def _ffn_kernel(*refs, sub, split):
    j = pl.program_id(1)
    is_first = j == 0
    is_last = j == pl.num_programs(1) - 1
    if split is None:
        x_ref, mod_ref, nw_ref, wg_ref, wu_ref, wd_ref, o_ref, h_ref, acc_ref = refs
        sources = [(x_ref, None)]
    else:
        xa_ref, xb_ref, mod_ref, nw_ref, wg_ref, wu_ref, wd_ref, o_ref, h_ref, acc_ref = refs
        from_a = pl.program_id(0) < split
        sources = [(xa_ref, from_a), (xb_ref, jnp.logical_not(from_a))]

    def prologue(x_ref):
        h = _rms(x_ref[...]) * nw_ref[2 * sub:2 * sub + 1, :]
        h = h * (1.0 + mod_ref[1:2, :]) + mod_ref[0:1, :]
        h_ref[...] = h.astype(BF16)
        acc_ref[...] = jnp.zeros_like(acc_ref)

    def epilogue(x_ref):
        yn = _rms(acc_ref[...]) * nw_ref[2 * sub + 1:2 * sub + 2, :]
        o_ref[...] = x_ref[...] + FFN_RES * mod_ref[2:3, :] * yn

    for x_ref, active in sources:
        pl.when(is_first if active is None else is_first & active)(functools.partial(prologue, x_ref))

    h = h_ref[...]
    g = _dot(h, wg_ref[...])
    u = _dot(h, wu_ref[...])
    a = (g * _sigmoid(g) * u).astype(BF16)
    acc_ref[...] += _dot(a, wd_ref[...])

    for x_ref, active in sources:
        pl.when(is_last if active is None else is_last & active)(functools.partial(epilogue, x_ref))


def _ffn(srcs, rows_out, mod, nw, w_gu, w_down, *, layer, sub, tm, mod_row):
    d = srcs[0].shape[1]
    tf = 512
    nf = D_FF // tf
    if len(srcs) == 1:
        split = None
        x_specs = [pl.BlockSpec((tm, d), lambda i, j: (i, 0))]
    else:
        split = srcs[0].shape[0] // tm
        x_specs = [pl.BlockSpec((tm, d), lambda i, j: (jnp.minimum(i, split - 1), 0)),
                   pl.BlockSpec((tm, d), lambda i, j: (jnp.maximum(i - split, 0), 0))]
    return pl.pallas_call(
        functools.partial(_ffn_kernel, sub=sub, split=split),
        grid=(rows_out // tm, nf),
        in_specs=x_specs + [
            pl.BlockSpec((None, None, 3, d), lambda i, j: (layer * 3 + sub, mod_row(i), 0, 0)),
            pl.BlockSpec((None, 6, d), lambda i, j: (layer, 0, 0)),
            pl.BlockSpec((None, d, tf), lambda i, j: (layer, 0, j)),
            pl.BlockSpec((None, d, tf), lambda i, j: (layer, 0, j + nf)),
            pl.BlockSpec((None, tf, d), lambda i, j: (layer, j, 0)),
        ],
        out_specs=pl.BlockSpec((tm, d), lambda i, j: (i, 0)),
        out_shape=jax.ShapeDtypeStruct((rows_out, d), F32),
        scratch_shapes=[pltpu.VMEM((tm, d), BF16), pltpu.VMEM((tm, d), F32)],
        compiler_params=_cparams(("arbitrary", "arbitrary")),
        name=f"ffn{sub}",
    )(*srcs, mod, nw, w_gu, w_gu, w_down)


def _proj_in_kernel(x_ref, mod_ref, nw_ref, w_ref, wdt_ref, o_ref, odt_ref, h_ref):
    @pl.when(pl.program_id(1) == 0)
    def _():
        h = _rms(x_ref[...]) * nw_ref[2:3, :]
        h = (h * (1.0 + mod_ref[1:2, :]) + mod_ref[0:1, :]).astype(BF16)
        h_ref[...] = h
        odt_ref[...] = _dot(h, wdt_ref[...])

    o_ref[...] = _dot(h_ref[...], w_ref[...])


def _proj_in(xs, mod, nw, w_main, w_dt, *, layer, tm, mod_row):
    rows, d = xs.shape
    n = w_main.shape[2]
    tn = n // 2
    return pl.pallas_call(
        _proj_in_kernel,
        grid=(rows // tm, n // tn),
        in_specs=[
            pl.BlockSpec((tm, d), lambda i, j: (i, 0)),
            pl.BlockSpec((None, None, 3, d), lambda i, j: (layer * 3 + 1, mod_row(i), 0, 0)),
            pl.BlockSpec((None, 6, d), lambda i, j: (layer, 0, 0)),
            pl.BlockSpec((None, d, tn), lambda i, j: (layer, 0, j)),
            pl.BlockSpec((None, d, LANES), lambda i, j: (layer, 0, 0)),
        ],
        out_specs=[pl.BlockSpec((tm, tn), lambda i, j: (i, j)),
                   pl.BlockSpec((tm, LANES), lambda i, j: (i, 0))],
        out_shape=[jax.ShapeDtypeStruct((rows, n), F32), jax.ShapeDtypeStruct((rows, LANES), F32)],
        scratch_shapes=[pltpu.VMEM((tm, d), BF16)],
        compiler_params=_cparams(("arbitrary", "arbitrary")),
        name="proj_in",
    )(xs, mod, nw, w_main, w_dt)


def _proj_out_kernel(x_ref, ra_ref, at_ref, ss_ref, w_ref, mod_ref, nw_ref, o_ref):
    y = _dot(ra_ref[...], w_ref[0:RET_W, :])
    y = y + _dot(at_ref[...], w_ref[RET_W:RET_W + ATT_W, :])
    y = y + _dot(ss_ref[...], w_ref[RET_W + ATT_W:, :])
    yn = _rms(y) * nw_ref[3:4, :]
    o_ref[...] = x_ref[...] + mod_ref[2:3, :] * yn


def _proj_out(xs, rows_out, ra, at, ss, w_out, mod, nw, *, layer, tm, mod_row):
    d = xs.shape[1]
    return pl.pallas_call(
        _proj_out_kernel,
        grid=(rows_out // tm,),
        in_specs=[
            pl.BlockSpec((tm, d), lambda i: (i, 0)),
            pl.BlockSpec((tm, RET_W), lambda i: (i, 0)),
            pl.BlockSpec((tm, ATT_W), lambda i: (i, 0)),
            pl.BlockSpec((tm, SSD_W), lambda i: (i, 0)),
            pl.BlockSpec((None, d, d), lambda i: (layer, 0, 0)),
            pl.BlockSpec((None, None, 3, d), lambda i: (layer * 3 + 1, mod_row(i), 0, 0)),
            pl.BlockSpec((None, 6, d), lambda i: (layer, 0, 0)),
        ],
        out_specs=pl.BlockSpec((tm, d), lambda i: (i, 0)),
        out_shape=jax.ShapeDtypeStruct((rows_out, d), F32),
        compiler_params=_cparams(("arbitrary",)),
        name="proj_out",
    )(xs, ra, at, ss, w_out, mod, nw)


class _Geom:
    def __init__(self, b, t, lc):
        assert t % RS == 0 and lc % RS == 0
        self.b, self.t, self.lc = b, t, lc
        self.nlat = t // RS
        self.nctx = lc // RS
        self.nst = self.nlat + self.nctx
        self.rows = b * (t + lc)

    def row_block(self, bi, j):
        return jnp.where(j < self.nctx,
                         self.b * self.nlat + bi * self.nctx + j,
                         bi * self.nlat + (j - self.nctx))

    def bwd_step(self, s):
        return jnp.where(s < self.nctx, self.nctx - 1 - s, self.nst + self.nctx - 1 - s)

    def seg_first(self, j):
        return (j == 0) | (j == self.nctx)

    def seg_last(self, j):
        return (j == self.nctx - 1) | (j == self.nst - 1)


def _ret_bstate_kernel(k_ref, v_ref, ld_ref, sb_ref, s_ref):
    @pl.when(pl.program_id(1) == 0)
    def _():
        s_ref[...] = jnp.zeros_like(s_ref)

    jrow = _iota2((CHUNK, CHUNK), 0).astype(F32)
    for h in range(RET_HEADS):
        hs = slice(h * RET_DK, (h + 1) * RET_DK)
        lgb = -jnp.abs(ld_ref[1, h:h + 1, :])
        kdec = jnp.exp(lgb * jrow)
        cdec = jnp.exp(lgb * float(CHUNK))
        for c in reversed(range(CPS)):
            rows = slice(c * CHUNK, (c + 1) * CHUNK)
            sb_ref[c, h] = s_ref[h]
            kd = (k_ref[rows, hs] * kdec).astype(BF16)
            s_ref[h] = s_ref[h] * cdec + _dot_tn(kd, v_ref[rows, hs].astype(BF16))


def _ret_fwd_kernel(q_ref, k_ref, v_ref, g_ref, sb_ref, ld_ref, nw_ref, o_ref, s_ref):
    @pl.when(pl.program_id(1) == 0)
    def _():
        s_ref[...] = jnp.zeros_like(s_ref)

    irow = _iota2((CHUNK, CHUNK), 0).astype(F32)
    rel = irow - _iota2((CHUNK, CHUNK), 1).astype(F32)
    for h in range(RET_HEADS):
        hs = slice(h * RET_DK, (h + 1) * RET_DK)
        lgf = -jnp.abs(ld_ref[0, h:h + 1, :])
        lgb = -jnp.abs(ld_ref[1, h:h + 1, :])
        dmask = (jnp.where(rel >= 0, jnp.exp(lgf * jnp.maximum(rel, 0.0)), 0.0)
                 + jnp.where(rel <= 0, jnp.exp(lgb * jnp.maximum(-rel, 0.0)), 0.0))
        qdec_f = jnp.exp(lgf * (irow + 1.0))
        qdec_b = jnp.exp(lgb * (float(CHUNK) - irow))
        kdec_f = jnp.exp(lgf * (float(CHUNK) - 1.0 - irow))
        cdec_f = jnp.exp(lgf * float(CHUNK))
        for c in range(CPS):
            rows = slice(c * CHUNK, (c + 1) * CHUNK)
            q = q_ref[rows, hs] * (RET_DK ** -0.5)
            k = k_ref[rows, hs]
            vb = v_ref[rows, hs].astype(BF16)
            inner = _dot_nt(q.astype(BF16), k.astype(BF16)) * dmask
            y = _dot(inner.astype(BF16), vb)
            y = y + _dot((q * qdec_f).astype(BF16), s_ref[h].astype(BF16))
            y = y + _dot((q * qdec_b).astype(BF16), sb_ref[c, h].astype(BF16))
            s_ref[h] = s_ref[h] * cdec_f + _dot_tn((k * kdec_f).astype(BF16), vb)
            mu = jnp.mean(y, axis=-1, keepdims=True)
            yc = y - mu
            var = jnp.mean(yc * yc, axis=-1, keepdims=True)
            gate = g_ref[rows, hs]
            yn = yc * lax.rsqrt(var + NORM_EPS) * nw_ref[:, hs]
            o_ref[rows, hs] = (yn * (gate * _sigmoid(gate))).astype(BF16)


def _retention(p, geom, ld, nw):
    b, nst = geom.b, geom.nst
    ld_spec = pl.BlockSpec((2, RET_HEADS, LANES), lambda bi, s: (0, 0, 0))
    state = pltpu.VMEM((RET_HEADS, RET_DK, RET_DK), F32)
    sb_block = (None, None, CPS, RET_HEADS, RET_DK, RET_DK)

    def blk(col, order):
        return pl.BlockSpec((RS, RET_W), lambda bi, s: (geom.row_block(bi, order(s)), col))

    sb = pl.pallas_call(
        _ret_bstate_kernel,
        grid=(b, nst),
        in_specs=[blk(1, geom.bwd_step), blk(2, geom.bwd_step), ld_spec],
        out_specs=pl.BlockSpec(sb_block, lambda bi, s: (bi, geom.bwd_step(s), 0, 0, 0, 0)),
        out_shape=jax.ShapeDtypeStruct((b, nst, CPS, RET_HEADS, RET_DK, RET_DK), F32),
        scratch_shapes=[state],
        compiler_params=_cparams(("arbitrary", "arbitrary")),
        name="ret_bstate",
    )(p, p, ld)

    ident = lambda s: s
    return pl.pallas_call(
        _ret_fwd_kernel,
        grid=(b, nst),
        in_specs=[blk(0, ident), blk(1, ident), blk(2, ident), blk(3, ident),
                  pl.BlockSpec(sb_block, lambda bi, s: (bi, s, 0, 0, 0, 0)),
                  ld_spec,
                  pl.BlockSpec((1, RET_W), lambda bi, s: (0, 0))],
        out_specs=pl.BlockSpec((RS, RET_W), lambda bi, s: (geom.row_block(bi, s), 0)),
        out_shape=jax.ShapeDtypeStruct((geom.rows, RET_W), BF16),
        scratch_shapes=[state],
        compiler_params=_cparams(("arbitrary", "arbitrary")),
        name="ret_fwd",
    )(p, p, p, p, sb, ld, nw)


def _rope(x, cos, sin):
    lane = _iota2(x.shape, 1)
    swapped = jnp.where((lane // 32) % 2 == 0, pltpu.roll(x, 96, axis=1), pltpu.roll(x, 32, axis=1))
    return x * cos + swapped * sin


def _att_kernel(q_ref, kl_ref, vl_ref, kc_ref, vc_ref, cos_ref, sin_ref, sink_ref, o_ref,
                kr_ref, vb_ref, *, t):
    n = pl.program_id(1)
    nb = pl.num_programs(1)
    blk = CHUNK

    @pl.when(n == 0)
    def _():
        zeros = jnp.zeros((blk, ATT_KV_W), BF16)
        kr_ref[0:blk, :] = zeros
        kr_ref[t + blk:t + 2 * blk, :] = zeros
        vb_ref[0:blk, :] = zeros
        vb_ref[t + blk:t + 2 * blk, :] = zeros
        for h in range(ATT_KV_HEADS):
            hs = slice(h * ATT_HD, (h + 1) * ATT_HD)
            kr_ref[blk:t + blk, hs] = _rope(kl_ref[:, hs], cos_ref[...], sin_ref[...]).astype(BF16)
        vb_ref[blk:t + blk, :] = vl_ref[...].astype(BF16)

    row0 = pl.multiple_of(n * blk, blk)
    cos = cos_ref[pl.ds(row0, blk), :]
    sin = sin_ref[pl.ds(row0, blk), :]
    scale = ATT_HD ** -0.5

    qi = _iota2((2 * blk, 3 * blk), 0) % blk
    kj = _iota2((2 * blk, 3 * blk), 1)
    rel = kj - blk - qi
    valid = (jnp.abs(rel) <= WINDOW)
    valid = valid & ((kj >= blk) | (n > 0)) & ((kj < 2 * blk) | (n < nb - 1))
    first_head = _iota2((2 * blk, 1), 0) < blk

    for h in range(ATT_KV_HEADS):
        hs = slice(h * ATT_HD, (h + 1) * ATT_HD)
        q0 = _rope(q_ref[:, 2 * h * ATT_HD:(2 * h + 1) * ATT_HD], cos, sin)
        q1 = _rope(q_ref[:, (2 * h + 1) * ATT_HD:(2 * h + 2) * ATT_HD], cos, sin)
        q2 = jnp.concatenate([q0, q1], axis=0).astype(BF16)
        kw = kr_ref[pl.ds(row0, 3 * blk), hs]
        vw = vb_ref[pl.ds(row0, 3 * blk), hs]
        s_loc = jnp.where(valid, _dot_nt(q2, kw) * scale, NEG_BIG)
        s_cx = _dot_nt(q2, kc_ref[:, hs].astype(BF16)) * scale
        sink = jnp.where(first_head, sink_ref[2 * h], sink_ref[2 * h + 1])
        m = jnp.maximum(jnp.maximum(jnp.max(s_loc, axis=-1, keepdims=True),
                                    jnp.max(s_cx, axis=-1, keepdims=True)), sink)
        e_loc = jnp.exp(s_loc - m)
        e_cx = jnp.exp(s_cx - m)
        den = (jnp.sum(e_loc, axis=-1, keepdims=True) + jnp.sum(e_cx, axis=-1, keepdims=True)
               + jnp.exp(sink - m))
        o = _dot(e_loc.astype(BF16), vw) + _dot(e_cx.astype(BF16), vc_ref[:, hs].astype(BF16))
        o = o / den
        o_ref[:, 2 * h * ATT_HD:(2 * h + 1) * ATT_HD] = o[0:blk].astype(BF16)
        o_ref[:, (2 * h + 1) * ATT_HD:(2 * h + 2) * ATT_HD] = o[blk:2 * blk].astype(BF16)


def _att_ctx_kernel(q_ref, kc_ref, vc_ref, sink_ref, o_ref):
    lc = q_ref.shape[0]
    scale = ATT_HD ** -0.5
    first_head = _iota2((2 * lc, 1), 0) < lc
    for h in range(ATT_KV_HEADS):
        hs = slice(h * ATT_HD, (h + 1) * ATT_HD)
        q2 = jnp.concatenate([q_ref[:, 2 * h * ATT_HD:(2 * h + 1) * ATT_HD],
                              q_ref[:, (2 * h + 1) * ATT_HD:(2 * h + 2) * ATT_HD]], axis=0).astype(BF16)
        s = _dot_nt(q2, kc_ref[:, hs].astype(BF16)) * scale
        sink = jnp.where(first_head, sink_ref[2 * h], sink_ref[2 * h + 1])
        m = jnp.maximum(jnp.max(s, axis=-1, keepdims=True), sink)
        e = jnp.exp(s - m)
        den = jnp.sum(e, axis=-1, keepdims=True) + jnp.exp(sink - m)
        o = _dot(e.astype(BF16), vc_ref[:, hs].astype(BF16)) / den
        o_ref[:, 2 * h * ATT_HD:(2 * h + 1) * ATT_HD] = o[0:lc].astype(BF16)
        o_ref[:, (2 * h + 1) * ATT_HD:(2 * h + 2) * ATT_HD] = o[lc:2 * lc].astype(BF16)


def _attention(p, geom, sink, cos, sin, ctx_out):
    b, t, lc = geom.b, geom.t, geom.lc
    nb = t // CHUNK
    ctx0 = (b * t) // lc
    qcol = 2048 // ATT_W
    kcol = (2048 + ATT_W) // ATT_KV_W
    vcol = kcol + 1
    smem = pl.BlockSpec(memory_space=pltpu.SMEM)
    out_l = pl.pallas_call(
        functools.partial(_att_kernel, t=t),
        grid=(b, nb),
        in_specs=[
            pl.BlockSpec((CHUNK, ATT_W), lambda bi, n: (bi * nb + n, qcol)),
            pl.BlockSpec((t, ATT_KV_W), lambda bi, n: (bi, kcol)),
            pl.BlockSpec((t, ATT_KV_W), lambda bi, n: (bi, vcol)),
            pl.BlockSpec((lc, ATT_KV_W), lambda bi, n: (ctx0 + bi, kcol)),
            pl.BlockSpec((lc, ATT_KV_W), lambda bi, n: (ctx0 + bi, vcol)),
            pl.BlockSpec((t, ATT_HD), lambda bi, n: (0, 0)),
            pl.BlockSpec((t, ATT_HD), lambda bi, n: (0, 0)),
            smem,
        ],
        out_specs=pl.BlockSpec((CHUNK, ATT_W), lambda bi, n: (bi * nb + n, 0)),
        out_shape=jax.ShapeDtypeStruct((b * t, ATT_W), BF16),
        scratch_shapes=[pltpu.VMEM((t + 2 * CHUNK, ATT_KV_W), BF16),
                        pltpu.VMEM((t + 2 * CHUNK, ATT_KV_W), BF16)],
        compiler_params=_cparams(("arbitrary", "arbitrary")),
        name="att_lat",
    )(p, p, p, p, p, cos, sin, sink)
    if not ctx_out:
        return out_l, None
    out_c = pl.pallas_call(
        _att_ctx_kernel,
        grid=(b,),
        in_specs=[
            pl.BlockSpec((lc, ATT_W), lambda bi: (ctx0 + bi, qcol)),
            pl.BlockSpec((lc, ATT_KV_W), lambda bi: (ctx0 + bi, kcol)),
            pl.BlockSpec((lc, ATT_KV_W), lambda bi: (ctx0 + bi, vcol)),
            smem,
        ],
        out_specs=pl.BlockSpec((lc, ATT_W), lambda bi: (bi, 0)),
        out_shape=jax.ShapeDtypeStruct((b * lc, ATT_W), BF16),
        compiler_params=_cparams(("arbitrary",)),
        name="att_ctx",
    )(p, p, p, sink)
    return out_l, out_c


N_DH = 2 * SSD_HEADS


def _pack3(v):
    hi = v.astype(BF16).astype(F32)
    r = v - hi
    mid = r.astype(BF16).astype(F32)
    lo = r - mid
    return (hi + pltpu.roll(mid, N_DH, axis=1) + pltpu.roll(lo, 2 * N_DH, axis=1)).astype(BF16)


def _unpack3(r):
    return r + pltpu.roll(r, LANES - N_DH, axis=1) + pltpu.roll(r, LANES - 2 * N_DH, axis=1)


def _conv_silu(u, prev_row, next_row, w, bias):
    n = u.shape[0]
    rows = _iota2(u.shape, 0)
    up = jnp.where(rows == 0, prev_row, pltpu.roll(u, 1, axis=0))
    un = jnp.where(rows == n - 1, next_row, pltpu.roll(u, n - 1, axis=0))
    y = w[0:1, :] * up + w[1:2, :] * u + w[2:3, :] * un + bias
    return y * _sigmoid(y)


def _softplus(v):
    return jnp.maximum(v, 0.0) + jnp.log1p(jnp.exp(-jnp.abs(v)))


def _ssd_dt_la(dt_raw, alog_ref, dtb_ref):
    lane = _iota2((1, LANES), 1)
    live = lane < N_DH
    a_row = jnp.where(live, -jnp.exp(alog_ref[...]), 0.0)
    dt = jnp.where(live, _softplus(dt_raw + dtb_ref[...]), 0.0)
    return dt, dt * a_row


def _tri(upper):
    r = _iota2((CHUNK, CHUNK), 0)
    c = _iota2((CHUNK, CHUNK), 1)
    return jnp.where((c >= r) if upper else (c <= r), 1.0, 0.0).astype(BF16)


def _ssd_bstate_kernel(x_ref, xp_ref, xn_ref, bc_ref, bp_ref, bn_ref, dt_ref, cw_ref, cb_ref,
                       alog_ref, dtb_ref, exp_ref, sb_ref, xs_ref, bco_ref, s_ref, *, geom):
    s = pl.program_id(1)
    j = geom.bwd_step(s)

    @pl.when(s == 0)
    def _():
        s_ref[...] = jnp.zeros_like(s_ref)

    first = geom.seg_first(j)
    last = geom.seg_last(j)
    hl = SUBLANES - 1
    xs_all = _conv_silu(x_ref[...],
                        jnp.where(first, 0.0, xp_ref[hl:hl + 1, :]), jnp.where(last, 0.0, xn_ref[0:1, :]),
                        cw_ref[:, 0:SSD_W], cb_ref[:, 0:SSD_W])
    bc_all = _conv_silu(bc_ref[...],
                        jnp.where(first, 0.0, bp_ref[hl:hl + 1, :]), jnp.where(last, 0.0, bn_ref[0:1, :]),
                        cw_ref[:, SSD_W:], cb_ref[:, SSD_W:]).astype(BF16)
    xs_ref[...] = xs_all
    bco_ref[...] = bc_all
    dt_all, la_all = _ssd_dt_la(dt_ref[...], alog_ref, dtb_ref)
    lane = _iota2((CHUNK, LANES), 1)
    ex = exp_ref[:, SSD_W:2 * SSD_W]
    gw = SSD_W // SSD_GROUPS
    tri_u = _tri(True)
    for c in reversed(range(CPS)):
        rows = slice(c * CHUNK, (c + 1) * CHUNK)
        sb_ref[c] = s_ref[...]
        rb = jnp.where(lane < N_DH, _unpack3(_dot(tri_u, _pack3(la_all[rows]))), 0.0)
        rbx = _dot(_pack3(rb), ex)
        dtx = _dot(_pack3(dt_all[rows]), ex)
        xw = (xs_all[rows] * dtx * jnp.exp(rbx[0:1, :] - rbx)).astype(BF16)
        cdec = jnp.exp(rbx[0:1, :])
        for g in range(SSD_GROUPS):
            gs = slice(g * gw, (g + 1) * gw)
            bg = bc_all[rows, g * SSD_STATE:(g + 1) * SSD_STATE]
            s_ref[g] = s_ref[g] * cdec[:, gs] + _dot_tn(bg, xw[:, gs])


def _ssd_fwd_kernel(z_ref, xs_ref, bc_ref, dt_ref, sb_ref, alog_ref, dtb_ref, dsk_ref, nw_ref,
                    exp_ref, expl_ref, o_ref, s_ref):
    @pl.when(pl.program_id(1) == 0)
    def _():
        s_ref[...] = jnp.zeros_like(s_ref)

    dt_all, la_all = _ssd_dt_la(dt_ref[...], alog_ref, dtb_ref)
    lane = _iota2((CHUNK, LANES), 1)
    r = _iota2((CHUNK, CHUNK), 0)
    cidx = _iota2((CHUNK, CHUNK), 1)
    lower = r >= cidx
    upper = cidx >= r
    lane64 = lane < SSD_HD
    fwd_lane = (lane % N_DH) < SSD_HEADS
    gw = SSD_W // SSD_GROUPS
    hpg = SSD_HEADS // SSD_GROUPS
    tri_l = _tri(False)
    tri_u = _tri(True)
    for c in range(CPS):
        rows = slice(c * CHUNK, (c + 1) * CHUNK)
        xs = xs_ref[rows, :]
        lap = _pack3(la_all[rows])
        cum = jnp.where(lane < N_DH,
                        _unpack3(jnp.where(fwd_lane, _dot(tri_l, lap), _dot(tri_u, lap))), 0.0)
        cum_t = cum.T
        cump = _pack3(cum)
        cumx = _dot(cump, exp_ref[...])
        dtx = _dot(_pack3(dt_all[rows]), exp_ref[...])
        xdt_f = xs * dtx[:, 0:SSD_W]
        xdt_b = xs * dtx[:, SSD_W:]

        cgs = []
        cbs = []
        for g in range(SSD_GROUPS):
            bg = bc_ref[rows, g * SSD_STATE:(g + 1) * SSD_STATE]
            cg = bc_ref[rows, SSD_BC_W + g * SSD_STATE:SSD_BC_W + (g + 1) * SSD_STATE]
            cgs.append((bg, cg))
            cbs.append(_dot_nt(cg, bg))
        y_parts = []
        for hp in range(SSD_HEADS // 2):
            g = (2 * hp) // hpg
            ps = slice(hp * LANES, (hp + 1) * LANES)
            col_f = _dot(cump, expl_ref[:, 2 * hp * LANES:(2 * hp + 2) * LANES])
            col_b = _dot(cump, expl_ref[:, (SSD_HEADS + 2 * hp) * LANES:(SSD_HEADS + 2 * hp + 2) * LANES])
            lhs = []
            rhs = []
            for e in range(2):
                h = 2 * hp + e
                es = slice(e * LANES, (e + 1) * LANES)
                lf = jnp.exp(jnp.where(lower, col_f[:, es] - cum_t[h:h + 1, :], NEG_BIG))
                lb = jnp.exp(jnp.where(upper, col_b[:, es] - cum_t[SSD_HEADS + h:SSD_HEADS + h + 1, :], NEG_BIG))
                lhs += [(cbs[g] * lf).astype(BF16), (cbs[g] * lb).astype(BF16)]
                keep = lane64 if e == 0 else jnp.logical_not(lane64)
                rhs += [jnp.where(keep, xdt_f[:, ps], 0.0).astype(BF16),
                        jnp.where(keep, xdt_b[:, ps], 0.0).astype(BF16)]
            y_parts.append(_dot(jnp.concatenate(lhs, axis=1), jnp.concatenate(rhs, axis=0)))
        y = jnp.concatenate(y_parts, axis=1)

        e_f = jnp.exp(cumx[:, 0:SSD_W])
        e_b = jnp.exp(cumx[:, SSD_W:])
        yoff_f = jnp.concatenate([_dot(cgs[g][1], s_ref[g].astype(BF16)) for g in range(SSD_GROUPS)], axis=1)
        yoff_b = jnp.concatenate([_dot(cgs[g][1], sb_ref[c, g].astype(BF16)) for g in range(SSD_GROUPS)], axis=1)
        y = y + yoff_f * e_f + yoff_b * e_b

        tot = cumx[CHUNK - 1:CHUNK, 0:SSD_W]
        xw = (xdt_f * jnp.exp(tot - cumx[:, 0:SSD_W])).astype(BF16)
        cdec = jnp.exp(tot)
        for g in range(SSD_GROUPS):
            gs = slice(g * gw, (g + 1) * gw)
            s_ref[g] = s_ref[g] * cdec[:, gs] + _dot_tn(cgs[g][0], xw[:, gs])

        z = z_ref[rows, :]
        y = (y + dsk_ref[...] * xs) * (z * _sigmoid(z))
        o_ref[rows, :] = (_rms(y) * nw_ref[...]).astype(BF16)


def _expand_consts():
    kk = np.arange(N_DH)
    ex = np.zeros((LANES, 2 * SSD_W), np.float32)
    exl = np.zeros((LANES, N_DH * LANES), np.float32)
    for part in range(3):
        for p in range(SSD_HD):
            ex[part * N_DH + kk, kk * SSD_HD + p] = 1.0
        for q in range(LANES):
            exl[part * N_DH + kk, kk * LANES + q] = 1.0
    return jnp.asarray(ex, BF16), jnp.asarray(exl, BF16)


def _ssd(p, pdt, geom, conv_w, conv_b, alog, dtb, dsk, nw):
    b, nst = geom.b, geom.nst
    ex, exl = _expand_consts()
    zcol = 3072 // SSD_W
    xcol = 4096 // SSD_W
    bccol = (4096 + SSD_W) // (2 * SSD_BC_W)
    n8 = geom.rows // SUBLANES
    sub = RS // SUBLANES
    gw = SSD_W // SSD_GROUPS
    sb_block = (None, None, CPS, SSD_GROUPS, SSD_STATE, gw)

    def const(shape):
        return pl.BlockSpec(shape, lambda bi, s: (0,) * len(shape))

    state = pltpu.VMEM((SSD_GROUPS, SSD_STATE, gw), F32)
    rb = lambda bi, s: geom.row_block(bi, geom.bwd_step(s))
    prev8 = lambda bi, s: jnp.maximum(rb(bi, s) * sub - 1, 0)
    next8 = lambda bi, s: jnp.minimum(rb(bi, s) * sub + sub, n8 - 1)
    sb, xs, bcm = pl.pallas_call(
        functools.partial(_ssd_bstate_kernel, geom=geom),
        grid=(b, nst),
        in_specs=[pl.BlockSpec((RS, SSD_W), lambda bi, s: (rb(bi, s), xcol)),
                  pl.BlockSpec((SUBLANES, SSD_W), lambda bi, s: (prev8(bi, s), xcol)),
                  pl.BlockSpec((SUBLANES, SSD_W), lambda bi, s: (next8(bi, s), xcol)),
                  pl.BlockSpec((RS, 2 * SSD_BC_W), lambda bi, s: (rb(bi, s), bccol)),
                  pl.BlockSpec((SUBLANES, 2 * SSD_BC_W), lambda bi, s: (prev8(bi, s), bccol)),
                  pl.BlockSpec((SUBLANES, 2 * SSD_BC_W), lambda bi, s: (next8(bi, s), bccol)),
                  pl.BlockSpec((RS, LANES), lambda bi, s: (rb(bi, s), 0)),
                  const(conv_w.shape), const(conv_b.shape), const(alog.shape), const(dtb.shape),
                  const(ex.shape)],
        out_specs=[pl.BlockSpec(sb_block, lambda bi, s: (bi, geom.bwd_step(s), 0, 0, 0, 0)),
                   pl.BlockSpec((RS, SSD_W), lambda bi, s: (rb(bi, s), 0)),
                   pl.BlockSpec((RS, 2 * SSD_BC_W), lambda bi, s: (rb(bi, s), 0))],
        out_shape=[jax.ShapeDtypeStruct((b, nst, CPS, SSD_GROUPS, SSD_STATE, gw), F32),
                   jax.ShapeDtypeStruct((geom.rows, SSD_W), F32),
                   jax.ShapeDtypeStruct((geom.rows, 2 * SSD_BC_W), BF16)],
        scratch_shapes=[state],
        compiler_params=_cparams(("arbitrary", "arbitrary")),
        name="ssd_bstate",
    )(p, p, p, p, p, p, pdt, conv_w, conv_b, alog, dtb, ex)

    rf = lambda bi, s: geom.row_block(bi, s)
    return pl.pallas_call(
        _ssd_fwd_kernel,
        grid=(b, nst),
        in_specs=[pl.BlockSpec((RS, SSD_W), lambda bi, s: (rf(bi, s), zcol)),
                  pl.BlockSpec((RS, SSD_W), lambda bi, s: (rf(bi, s), 0)),
                  pl.BlockSpec((RS, 2 * SSD_BC_W), lambda bi, s: (rf(bi, s), 0)),
                  pl.BlockSpec((RS, LANES), lambda bi, s: (rf(bi, s), 0)),
                  pl.BlockSpec(sb_block, lambda bi, s: (bi, s, 0, 0, 0, 0)),
                  const(alog.shape), const(dtb.shape), const(dsk.shape), const(nw.shape),
                  const(ex.shape), const(exl.shape)],
        out_specs=pl.BlockSpec((RS, SSD_W), lambda bi, s: (rf(bi, s), 0)),
        out_shape=jax.ShapeDtypeStruct((geom.rows, SSD_W), BF16),
        scratch_shapes=[state],
        compiler_params=_cparams(("arbitrary", "arbitrary")),
        name="ssd_fwd",
    )(p, xs, bcm, pdt, sb, alog, dtb, dsk, nw, ex, exl)


def _rope_tables(t):
    half = ATT_HD // 2
    freqs = ROPE_BASE ** (-jnp.arange(0, half, 2, dtype=F32) / half)
    pos = jnp.arange(t)
    ang_r = (pos // GRID_W).astype(F32)[:, None] * freqs[None, :]
    ang_c = (pos % GRID_W).astype(F32)[:, None] * freqs[None, :]
    cos = jnp.concatenate([jnp.cos(ang_r), jnp.cos(ang_r), jnp.cos(ang_c), jnp.cos(ang_c)], axis=-1)
    sin = jnp.concatenate([-jnp.sin(ang_r), jnp.sin(ang_r), -jnp.sin(ang_c), jnp.sin(ang_c)], axis=-1)
    return cos, sin


def _pad_lanes(v):
    v = v.reshape(1, -1)
    return jnp.pad(v, ((0, 0), (0, LANES - v.shape[1])))


def _mixer(p, pdt, geom, layer, ctx_out, cos, sin, ret_log_decay, ret_norm_w, attn_sink, ssd_conv_w,
           ssd_conv_b, ssd_a_log, ssd_dt_bias, ssd_d, ssd_norm_w):
    ld = jnp.broadcast_to(ret_log_decay[layer][:, :, None], (2, RET_HEADS, LANES))
    ra = _retention(p, geom, ld, ret_norm_w[layer].reshape(1, RET_W))
    at_l, at_c = _attention(p, geom, attn_sink[layer], cos, sin, ctx_out)
    ss = _ssd(p, pdt, geom, ssd_conv_w[layer], ssd_conv_b[layer].reshape(1, -1),
              _pad_lanes(ssd_a_log[layer]), _pad_lanes(ssd_dt_bias[layer]),
              jnp.repeat(ssd_d[layer], SSD_HD).reshape(1, SSD_W),
              ssd_norm_w[layer].reshape(1, SSD_W))
    return ra, at_l, at_c, ss


def _forward(x, c, ctx, c_ctx, w_ada, b_ada, norm_w, ffn1_gu, ffn1_down, ffn2_gu, ffn2_down,
             w_in, w_out, ret_log_decay, ret_norm_w, attn_sink, ssd_conv_w, ssd_conv_b,
             ssd_a_log, ssd_dt_bias, ssd_d, ssd_norm_w):
    b, t, d = x.shape
    lc = ctx.shape[1]
    depth = w_ada.shape[0]
    geom = _Geom(b, t, lc)
    n_lat = b * t
    tm = math.gcd(512, math.gcd(t, b * lc))

    def mod_row(i):
        r0 = i * tm
        return jnp.where(r0 < n_lat, 1 + r0 // t, 0)

    cond = jnp.concatenate([c_ctx[None, :], c, jnp.zeros((SUBLANES - 1 - b, d), F32)], axis=0)
    mod = _adaln(cond, w_ada, b_ada)
    mod = mod.reshape(SUBLANES, depth, 3, 3, d).transpose(1, 2, 0, 3, 4).reshape(depth * 3, SUBLANES, 3, d)

    w1gu = ffn1_gu.astype(BF16)
    w1d = ffn1_down.astype(BF16)
    w2gu = ffn2_gu.astype(BF16)
    w2d = ffn2_down.astype(BF16)
    wi_main = w_in[:, :, :IN_MAIN].astype(BF16)
    wi_dt = jnp.pad(w_in[:, :, IN_MAIN:].astype(BF16), ((0, 0), (0, 0), (0, LANES - N_DH)))
    wo = w_out.astype(BF16)

    cos, sin = _rope_tables(t)
    srcs = [x.reshape(n_lat, d), ctx.reshape(b * lc, d)]
    rows_all = n_lat + b * lc
    for layer in range(depth):
        last = layer == depth - 1
        xs = _ffn(srcs, rows_all, mod, norm_w, w1gu, w1d, layer=layer, sub=0, tm=tm, mod_row=mod_row)
        p, pdt = _proj_in(xs, mod, norm_w, wi_main, wi_dt, layer=layer, tm=tm, mod_row=mod_row)
        ra, at_l, at_c, ss = _mixer(p, pdt, geom, layer, not last, cos, sin, ret_log_decay, ret_norm_w,
                                    attn_sink, ssd_conv_w, ssd_conv_b, ssd_a_log, ssd_dt_bias, ssd_d,
                                    ssd_norm_w)
        at = at_l if last else jnp.concatenate([at_l, at_c], axis=0)
        rows_out = n_lat if last else rows_all
        xs = _proj_out(xs, rows_out, ra, at, ss, wo, mod, norm_w, layer=layer, tm=tm, mod_row=mod_row)
        xs = _ffn([xs], rows_out, mod, norm_w, w2gu, w2d, layer=layer, sub=2, tm=tm, mod_row=mod_row)
        srcs = [xs]
    return xs.reshape(b, t, d)


def kernel(x, c, ctx, c_ctx, w_ada, b_ada, norm_w, ffn1_gu, ffn1_down, ffn2_gu, ffn2_down,
           w_in, w_out, ret_log_decay, ret_norm_w, attn_sink, ssd_conv_w, ssd_conv_b,
           ssd_a_log, ssd_dt_bias, ssd_d, ssd_norm_w):
    return _forward(x, c, ctx, c_ctx, w_ada, b_ada, norm_w, ffn1_gu, ffn1_down, ffn2_gu, ffn2_down,
                    w_in, w_out, ret_log_decay, ret_norm_w, attn_sink, ssd_conv_w, ssd_conv_b,
                    ssd_a_log, ssd_dt_bias, ssd_d, ssd_norm_w)
```

```python
import functools
import math

import jax
import jax.numpy as jnp
import numpy as np
from jax import lax
from jax.experimental import pallas as pl
from jax.experimental.pallas import tpu as pltpu

F32 = jnp.float32
BF16 = jnp.bfloat16

D_MODEL = 2048
GRID_W = 64
RET_HEADS = 4
RET_DK = 128
RET_W = 512
ATT_HEADS = 4
ATT_KV_HEADS = 2
ATT_HD = 128
ATT_W = 512
ATT_KV_W = 256
WINDOW = 128
SSD_HEADS = 16
SSD_HD = 64
SSD_W = 1024
SSD_GROUPS = 2
SSD_STATE = 128
SSD_BC_W = 256
CHUNK = 128
D_FF = 5632
FFN_RES = 0.5
ROPE_BASE = 10000.0
NORM_EPS = 1e-6
N_MOD = 9
IN_MAIN = 5632
NEG_BIG = -1e30
LOG2E = 1.4426950408889634

CPS = 2
RS = CHUNK * CPS
LANES = 128
SUBLANES = 8

VMEM_LIMIT = 56 * 1024 * 1024


def _cparams(sem):
    return pltpu.CompilerParams(dimension_semantics=sem, vmem_limit_bytes=VMEM_LIMIT)


def _sigmoid(v):
    return 1.0 / (1.0 + jnp.exp(-v))


def _rms(v):
    return v * lax.rsqrt(jnp.mean(v * v, axis=-1, keepdims=True) + NORM_EPS)


def _dot(a, b):
    return jnp.dot(a, b, preferred_element_type=F32)


def _dot_nt(a, b):
    return lax.dot_general(a, b, (((1,), (1,)), ((), ())), preferred_element_type=F32)


def _dot_tn(a, b):
    return lax.dot_general(a, b, (((0,), (0,)), ((), ())), preferred_element_type=F32)


def _iota2(shape, dim):
    return lax.broadcasted_iota(jnp.int32, shape, dim)


SLAB = 16
SLAB_UNROLL = 4


def _row_slabs(n_rows, body):
    def step(r, carry):
        body(pl.ds(pl.multiple_of(r * SLAB, SLAB), SLAB))
        return carry

    lax.fori_loop(0, n_rows // SLAB, step, 0, unroll=SLAB_UNROLL)


def _modnorm_rows(x_ref, h_ref, nw_row, mod_ref, zero_ref=None):
    gain = nw_row * (1.0 + mod_ref[1:2, :])
    shift = mod_ref[0:1, :]

    def body(rows):
        h_ref[rows, :] = (_rms(x_ref[rows, :]) * gain + shift).astype(BF16)
        if zero_ref is not None:
            zero_ref[rows, :] = jnp.zeros((SLAB, zero_ref.shape[1]), zero_ref.dtype)

    _row_slabs(x_ref.shape[0], body)


def _residual_rows(x_ref, y_ref, gain, o_ref):
    def body(rows):
        o_ref[rows, :] = x_ref[rows, :] + _rms(y_ref[rows, :]) * gain

    _row_slabs(x_ref.shape[0], body)


def _adaln_kernel(c_ref, w_ref, b_ref, o_ref):
    cnd = c_ref[...]
    s = (cnd * _sigmoid(cnd)).astype(BF16)
    o_ref[...] = _dot(s, w_ref[...].astype(BF16)) + b_ref[...]


def _adaln(cond, w_ada, b_ada):
    depth, d, n = w_ada.shape
    tn = 1024
    nt = n // tn
    return pl.pallas_call(
        _adaln_kernel,
        grid=(depth, nt),
        in_specs=[
            pl.BlockSpec((SUBLANES, d), lambda l, j: (0, 0)),
            pl.BlockSpec((None, d, tn), lambda l, j: (l, 0, j)),
            pl.BlockSpec((None, 1, tn), lambda l, j: (l, 0, j)),
        ],
        out_specs=pl.BlockSpec((SUBLANES, tn), lambda l, j: (0, l * nt + j)),
        out_shape=jax.ShapeDtypeStruct((SUBLANES, depth * n), F32),
        compiler_params=_cparams(("arbitrary", "arbitrary")),
        name="adaln",
    )(cond, w_ada, b_ada.reshape(depth, 1, n))


def _ffn_kernel(*refs, sub, split):
    j = pl.program_id(1)
    is_first = j == 0
    is_last = j == pl.num_programs(1) - 1
    if split is None:
        x_ref, mod_ref, nw_ref, wg_ref, wu_ref, wd_ref, o_ref, h_ref, acc_ref = refs
        sources = [(x_ref, None)]
    else:
        xa_ref, xb_ref, mod_ref, nw_ref, wg_ref, wu_ref, wd_ref, o_ref, h_ref, acc_ref = refs
        from_a = pl.program_id(0) < split
        sources = [(xa_ref, from_a), (xb_ref, jnp.logical_not(from_a))]

    def prologue(x_ref):
        _modnorm_rows(x_ref, h_ref, nw_ref[2 * sub:2 * sub + 1, :], mod_ref, zero_ref=acc_ref)

    def epilogue(x_ref):
        gain = (FFN_RES * mod_ref[2:3, :]) * nw_ref[2 * sub + 1:2 * sub + 2, :]
        _residual_rows(x_ref, acc_ref, gain, o_ref)

    for x_ref, active in sources:
        pl.when(is_first if active is None else is_first & active)(functools.partial(prologue, x_ref))

    h = h_ref[...]
    g = _dot(h, wg_ref[...])
    u = _dot(h, wu_ref[...])
    a = (g * _sigmoid(g) * u).astype(BF16)
    acc_ref[...] += _dot(a, wd_ref[...])

    for x_ref, active in sources:
        pl.when(is_last if active is None else is_last & active)(functools.partial(epilogue, x_ref))


def _ffn(srcs, rows_out, mod, nw, w_gu, w_down, *, layer, sub, tm, mod_row):
    d = srcs[0].shape[1]
    tf = 512
    nf = D_FF // tf
    if len(srcs) == 1:
        split = None
        x_specs = [pl.BlockSpec((tm, d), lambda i, j: (i, 0))]
    else:
        split = srcs[0].shape[0] // tm
        x_specs = [pl.BlockSpec((tm, d), lambda i, j: (jnp.minimum(i, split - 1), 0)),
                   pl.BlockSpec((tm, d), lambda i, j: (jnp.maximum(i - split, 0), 0))]
    return pl.pallas_call(
        functools.partial(_ffn_kernel, sub=sub, split=split),
        grid=(rows_out // tm, nf),
        in_specs=x_specs + [
            pl.BlockSpec((None, None, 3, d), lambda i, j: (layer * 3 + sub, mod_row(i), 0, 0)),
            pl.BlockSpec((None, 6, d), lambda i, j: (layer, 0, 0)),
            pl.BlockSpec((None, d, tf), lambda i, j: (layer, 0, j)),
            pl.BlockSpec((None, d, tf), lambda i, j: (layer, 0, j + nf)),
            pl.BlockSpec((None, tf, d), lambda i, j: (layer, j, 0)),
        ],
        out_specs=pl.BlockSpec((tm, d), lambda i, j: (i, 0)),
        out_shape=jax.ShapeDtypeStruct((rows_out, d), F32),
        scratch_shapes=[pltpu.VMEM((tm, d), BF16), pltpu.VMEM((tm, d), F32)],
        compiler_params=_cparams(("arbitrary", "arbitrary")),
        name=f"ffn{sub}",
    )(*srcs, mod, nw, w_gu, w_gu, w_down)


def _proj_in_kernel(x_ref, mod_ref, nw_ref, w_ref, wdt_ref, o_ref, odt_ref, h_ref, *, tn):
    j = pl.program_id(1)

    @pl.when(j == 0)
    def _():
        _modnorm_rows(x_ref, h_ref, nw_ref[2:3, :], mod_ref)
        odt_ref[...] = _dot(h_ref[...], wdt_ref[...])

    def project(jj):
        o_ref[...] = _dot(h_ref[...], w_ref[:, jj * tn:(jj + 1) * tn])

    for jj in range(IN_MAIN // tn):
        pl.when(j == jj)(functools.partial(project, jj))


def _proj_in(xs, mod, nw, w_in, w_dt, *, layer, tm, mod_row):
    rows, d = xs.shape
    n = IN_MAIN
    tn = n // 2
    nfull = w_in.shape[2]
    return pl.pallas_call(
        functools.partial(_proj_in_kernel, tn=tn),
        grid=(rows // tm, n // tn),
        in_specs=[
            pl.BlockSpec((tm, d), lambda i, j: (i, 0)),
            pl.BlockSpec((None, None, 3, d), lambda i, j: (layer * 3 + 1, mod_row(i), 0, 0)),
            pl.BlockSpec((None, 6, d), lambda i, j: (layer, 0, 0)),
            pl.BlockSpec((None, d, nfull), lambda i, j: (layer, 0, 0), pipeline_mode=pl.Buffered(1)),
            pl.BlockSpec((None, d, LANES), lambda i, j: (layer, 0, 0)),
        ],
        out_specs=[pl.BlockSpec((tm, tn), lambda i, j: (i, j)),
                   pl.BlockSpec((tm, LANES), lambda i, j: (i, 0))],
        out_shape=[jax.ShapeDtypeStruct((rows, n), F32), jax.ShapeDtypeStruct((rows, LANES), F32)],
        scratch_shapes=[pltpu.VMEM((tm, d), BF16)],
        compiler_params=_cparams(("arbitrary", "arbitrary")),
        name="proj_in",
    )(xs, mod, nw, w_in, w_dt)


def _proj_out_kernel(x_ref, ra_ref, at_ref, ss_ref, w_ref, mod_ref, nw_ref, o_ref):
    y = _dot(ra_ref[...], w_ref[0:RET_W, :])
    y = y + _dot(at_ref[...], w_ref[RET_W:RET_W + ATT_W, :])
    y = y + _dot(ss_ref[...], w_ref[RET_W + ATT_W:, :])
    o_ref[...] = y
    _residual_rows(x_ref, o_ref, mod_ref[2:3, :] * nw_ref[3:4, :], o_ref)


def _proj_out(xs, rows_out, ra, at, ss, w_out, mod, nw, *, layer, tm, mod_row):
    d = xs.shape[1]
    return pl.pallas_call(
        _proj_out_kernel,
        grid=(rows_out // tm,),
        in_specs=[
            pl.BlockSpec((tm, d), lambda i: (i, 0)),
            pl.BlockSpec((tm, RET_W), lambda i: (i, 0)),
            pl.BlockSpec((tm, ATT_W), lambda i: (i, 0)),
            pl.BlockSpec((tm, SSD_W), lambda i: (i, 0)),
            pl.BlockSpec((None, d, d), lambda i: (layer, 0, 0)),
            pl.BlockSpec((None, None, 3, d), lambda i: (layer * 3 + 1, mod_row(i), 0, 0)),
            pl.BlockSpec((None, 6, d), lambda i: (layer, 0, 0)),
        ],
        out_specs=pl.BlockSpec((tm, d), lambda i: (i, 0)),
        out_shape=jax.ShapeDtypeStruct((rows_out, d), F32),
        compiler_params=_cparams(("arbitrary",)),
        name="proj_out",
    )(xs, ra, at, ss, w_out, mod, nw)


class _Geom:
    def __init__(self, b, t, lc):
        assert t % RS == 0 and lc % RS == 0
        self.b, self.t, self.lc = b, t, lc
        self.nlat = t // RS
        self.nctx = lc // RS
        self.nst = self.nlat + self.nctx
        self.rows = b * (t + lc)

    def row_block(self, bi, j):
        return jnp.where(j < self.nctx,
                         self.b * self.nlat + bi * self.nctx + j,
                         bi * self.nlat + (j - self.nctx))

    def bwd_step(self, s):
        return jnp.where(s < self.nctx, self.nctx - 1 - s, self.nst + self.nctx - 1 - s)

    def seg_first(self, j):
        return (j == 0) | (j == self.nctx)

    def seg_last(self, j):
        return (j == self.nctx - 1) | (j == self.nst - 1)


def _ret_bstate_kernel(k_ref, v_ref, ld_ref, sb_ref, s_ref):
    @pl.when(pl.program_id(1) == 0)
    def _():
        s_ref[...] = jnp.zeros_like(s_ref)

    jrow = _iota2((CHUNK, CHUNK), 0).astype(F32)
    for h in range(RET_HEADS):
        hs = slice(h * RET_DK, (h + 1) * RET_DK)
        lgb = -jnp.abs(ld_ref[1, h:h + 1, :])
        kdec = jnp.exp(lgb * jrow)
        cdec = jnp.exp(lgb * float(CHUNK))
        for c in reversed(range(CPS)):
            rows = slice(c * CHUNK, (c + 1) * CHUNK)
            sb_ref[c, h] = s_ref[h]
            kd = (k_ref[rows, hs] * kdec).astype(BF16)
            s_ref[h] = s_ref[h] * cdec + _dot_tn(kd, v_ref[rows, hs].astype(BF16))


def _ret_fwd_kernel(q_ref, k_ref, v_ref, g_ref, sb_ref, ld_ref, nw_ref, o_ref, s_ref):
    @pl.when(pl.program_id(1) == 0)
    def _():
        s_ref[...] = jnp.zeros_like(s_ref)

    irow = _iota2((CHUNK, CHUNK), 0).astype(F32)
    rel = irow - _iota2((CHUNK, CHUNK), 1).astype(F32)
    for h in range(RET_HEADS):
        hs = slice(h * RET_DK, (h + 1) * RET_DK)
        lgf = -jnp.abs(ld_ref[0, h:h + 1, :])
        lgb = -jnp.abs(ld_ref[1, h:h + 1, :])
        dmask = (jnp.where(rel >= 0, jnp.exp(lgf * jnp.maximum(rel, 0.0)), 0.0)
                 + jnp.where(rel <= 0, jnp.exp(lgb * jnp.maximum(-rel, 0.0)), 0.0))
        qdec_f = jnp.exp(lgf * (irow + 1.0))
        qdec_b = jnp.exp(lgb * (float(CHUNK) - irow))
        kdec_f = jnp.exp(lgf * (float(CHUNK) - 1.0 - irow))
        cdec_f = jnp.exp(lgf * float(CHUNK))
        for c in range(CPS):
            rows = slice(c * CHUNK, (c + 1) * CHUNK)
            q = q_ref[rows, hs] * (RET_DK ** -0.5)
            k = k_ref[rows, hs]
            vb = v_ref[rows, hs].astype(BF16)
            inner = _dot_nt(q.astype(BF16), k.astype(BF16)) * dmask
            y = _dot(inner.astype(BF16), vb)
            y = y + _dot((q * qdec_f).astype(BF16), s_ref[h].astype(BF16))
            y = y + _dot((q * qdec_b).astype(BF16), sb_ref[c, h].astype(BF16))
            s_ref[h] = s_ref[h] * cdec_f + _dot_tn((k * kdec_f).astype(BF16), vb)
            mu = jnp.mean(y, axis=-1, keepdims=True)
            yc = y - mu
            var = jnp.mean(yc * yc, axis=-1, keepdims=True)
            gate = g_ref[rows, hs]
            yn = yc * lax.rsqrt(var + NORM_EPS) * nw_ref[:, hs]
            o_ref[rows, hs] = (yn * (gate * _sigmoid(gate))).astype(BF16)


def _retention(p, geom, ld, nw):
    b, nst = geom.b, geom.nst
    ld_spec = pl.BlockSpec((2, RET_HEADS, LANES), lambda bi, s: (0, 0, 0))
    state = pltpu.VMEM((RET_HEADS, RET_DK, RET_DK), F32)
    sb_block = (None, None, CPS, RET_HEADS, RET_DK, RET_DK)

    def blk(col, order):
        return pl.BlockSpec((RS, RET_W), lambda bi, s: (geom.row_block(bi, order(s)), col))

    sb = pl.pallas_call(
        _ret_bstate_kernel,
        grid=(b, nst),
        in_specs=[blk(1, geom.bwd_step), blk(2, geom.bwd_step), ld_spec],
        out_specs=pl.BlockSpec(sb_block, lambda bi, s: (bi, geom.bwd_step(s), 0, 0, 0, 0)),
        out_shape=jax.ShapeDtypeStruct((b, nst, CPS, RET_HEADS, RET_DK, RET_DK), F32),
        scratch_shapes=[state],
        compiler_params=_cparams(("arbitrary", "arbitrary")),
        name="ret_bstate",
    )(p, p, ld)

    ident = lambda s: s
    return pl.pallas_call(
        _ret_fwd_kernel,
        grid=(b, nst),
        in_specs=[blk(0, ident), blk(1, ident), blk(2, ident), blk(3, ident),
                  pl.BlockSpec(sb_block, lambda bi, s: (bi, s, 0, 0, 0, 0)),
                  ld_spec,
                  pl.BlockSpec((1, RET_W), lambda bi, s: (0, 0))],
        out_specs=pl.BlockSpec((RS, RET_W), lambda bi, s: (geom.row_block(bi, s), 0)),
        out_shape=jax.ShapeDtypeStruct((geom.rows, RET_W), BF16),
        scratch_shapes=[state],
        compiler_params=_cparams(("arbitrary", "arbitrary")),
        name="ret_fwd",
    )(p, p, p, p, sb, ld, nw)


def _rope(x, cos, sin):
    lane = _iota2(x.shape, 1)
    swapped = jnp.where((lane // 32) % 2 == 0, pltpu.roll(x, 96, axis=1), pltpu.roll(x, 32, axis=1))
    return x * cos + swapped * sin


def _att_kernel(q_ref, kl_ref, vl_ref, kc_ref, vc_ref, cos_ref, sin_ref, sink_ref, o_ref,
                kr_ref, vb_ref, *, t):
    n = pl.program_id(1)
    nb = pl.num_programs(1)
    blk = CHUNK

    @pl.when(n == 0)
    def _():
        zeros = jnp.zeros((blk, ATT_KV_W), BF16)
        kr_ref[0:blk, :] = zeros
        kr_ref[t + blk:t + 2 * blk, :] = zeros
        vb_ref[0:blk, :] = zeros
        vb_ref[t + blk:t + 2 * blk, :] = zeros
        for h in range(ATT_KV_HEADS):
            hs = slice(h * ATT_HD, (h + 1) * ATT_HD)
            kr_ref[blk:t + blk, hs] = _rope(kl_ref[:, hs], cos_ref[...], sin_ref[...]).astype(BF16)
        vb_ref[blk:t + blk, :] = vl_ref[...].astype(BF16)

    row0 = pl.multiple_of(n * blk, blk)
    cos = cos_ref[pl.ds(row0, blk), :]
    sin = sin_ref[pl.ds(row0, blk), :]
    scale = ATT_HD ** -0.5

    qi = _iota2((2 * blk, 3 * blk), 0) % blk
    kj = _iota2((2 * blk, 3 * blk), 1)
    rel = kj - blk - qi
    valid = (jnp.abs(rel) <= WINDOW)
    valid = valid & ((kj >= blk) | (n > 0)) & ((kj < 2 * blk) | (n < nb - 1))
    first_head = _iota2((2 * blk, 1), 0) < blk

    for h in range(ATT_KV_HEADS):
        hs = slice(h * ATT_HD, (h + 1) * ATT_HD)
        q0 = _rope(q_ref[:, 2 * h * ATT_HD:(2 * h + 1) * ATT_HD], cos, sin)
        q1 = _rope(q_ref[:, (2 * h + 1) * ATT_HD:(2 * h + 2) * ATT_HD], cos, sin)
        q2 = jnp.concatenate([q0, q1], axis=0).astype(BF16)
        kw = kr_ref[pl.ds(row0, 3 * blk), hs]
        vw = vb_ref[pl.ds(row0, 3 * blk), hs]
        s_loc = jnp.where(valid, _dot_nt(q2, kw) * scale, NEG_BIG)
        s_cx = _dot_nt(q2, kc_ref[:, hs].astype(BF16)) * scale
        sink = jnp.where(first_head, sink_ref[2 * h], sink_ref[2 * h + 1])
        m = jnp.maximum(jnp.maximum(jnp.max(s_loc, axis=-1, keepdims=True),
                                    jnp.max(s_cx, axis=-1, keepdims=True)), sink)
        e_loc = jnp.exp(s_loc - m)
        e_cx = jnp.exp(s_cx - m)
        den = (jnp.sum(e_loc, axis=-1, keepdims=True) + jnp.sum(e_cx, axis=-1, keepdims=True)
               + jnp.exp(sink - m))
        o = _dot(e_loc.astype(BF16), vw) + _dot(e_cx.astype(BF16), vc_ref[:, hs].astype(BF16))
        o = o / den
        o_ref[:, 2 * h * ATT_HD:(2 * h + 1) * ATT_HD] = o[0:blk].astype(BF16)
        o_ref[:, (2 * h + 1) * ATT_HD:(2 * h + 2) * ATT_HD] = o[blk:2 * blk].astype(BF16)


def _att_ctx_kernel(q_ref, kc_ref, vc_ref, sink_ref, o_ref):
    lc = q_ref.shape[0]
    scale = ATT_HD ** -0.5
    first_head = _iota2((2 * lc, 1), 0) < lc
    for h in range(ATT_KV_HEADS):
        hs = slice(h * ATT_HD, (h + 1) * ATT_HD)
        q2 = jnp.concatenate([q_ref[:, 2 * h * ATT_HD:(2 * h + 1) * ATT_HD],
                              q_ref[:, (2 * h + 1) * ATT_HD:(2 * h + 2) * ATT_HD]], axis=0).astype(BF16)
        s = _dot_nt(q2, kc_ref[:, hs].astype(BF16)) * scale
        sink = jnp.where(first_head, sink_ref[2 * h], sink_ref[2 * h + 1])
        m = jnp.maximum(jnp.max(s, axis=-1, keepdims=True), sink)
        e = jnp.exp(s - m)
        den = jnp.sum(e, axis=-1, keepdims=True) + jnp.exp(sink - m)
        o = _dot(e.astype(BF16), vc_ref[:, hs].astype(BF16)) / den
        o_ref[:, 2 * h * ATT_HD:(2 * h + 1) * ATT_HD] = o[0:lc].astype(BF16)
        o_ref[:, (2 * h + 1) * ATT_HD:(2 * h + 2) * ATT_HD] = o[lc:2 * lc].astype(BF16)


def _attention(p, geom, sink, cos, sin, ctx_out):
    b, t, lc = geom.b, geom.t, geom.lc
    nb = t // CHUNK
    ctx0 = (b * t) // lc
    qcol = 2048 // ATT_W
    kcol = (2048 + ATT_W) // ATT_KV_W
    vcol = kcol + 1
    smem = pl.BlockSpec(memory_space=pltpu.SMEM)
    out_l = pl.pallas_call(
        functools.partial(_att_kernel, t=t),
        grid=(b, nb),
        in_specs=[
            pl.BlockSpec((CHUNK, ATT_W), lambda bi, n: (bi * nb + n, qcol)),
            pl.BlockSpec((t, ATT_KV_W), lambda bi, n: (bi, kcol)),
            pl.BlockSpec((t, ATT_KV_W), lambda bi, n: (bi, vcol)),
            pl.BlockSpec((lc, ATT_KV_W), lambda bi, n: (ctx0 + bi, kcol)),
            pl.BlockSpec((lc, ATT_KV_W), lambda bi, n: (ctx0 + bi, vcol)),
            pl.BlockSpec((t, ATT_HD), lambda bi, n: (0, 0)),
            pl.BlockSpec((t, ATT_HD), lambda bi, n: (0, 0)),
            smem,
        ],
        out_specs=pl.BlockSpec((CHUNK, ATT_W), lambda bi, n: (bi * nb + n, 0)),
        out_shape=jax.ShapeDtypeStruct((b * t, ATT_W), BF16),
        scratch_shapes=[pltpu.VMEM((t + 2 * CHUNK, ATT_KV_W), BF16),
                        pltpu.VMEM((t + 2 * CHUNK, ATT_KV_W), BF16)],
        compiler_params=_cparams(("arbitrary", "arbitrary")),
        name="att_lat",
    )(p, p, p, p, p, cos, sin, sink)
    if not ctx_out:
        return out_l, None
    out_c = pl.pallas_call(
        _att_ctx_kernel,
        grid=(b,),
        in_specs=[
            pl.BlockSpec((lc, ATT_W), lambda bi: (ctx0 + bi, qcol)),
            pl.BlockSpec((lc, ATT_KV_W), lambda bi: (ctx0 + bi, kcol)),
            pl.BlockSpec((lc, ATT_KV_W), lambda bi: (ctx0 + bi, vcol)),
            smem,
        ],
        out_specs=pl.BlockSpec((lc, ATT_W), lambda bi: (bi, 0)),
        out_shape=jax.ShapeDtypeStruct((b * lc, ATT_W), BF16),
        compiler_params=_cparams(("arbitrary",)),
        name="att_ctx",
    )(p, p, p, sink)
    return out_l, out_c


N_DH = 2 * SSD_HEADS


def _pack3(v):
    hi = v.astype(BF16).astype(F32)
    r = v - hi
    mid = r.astype(BF16).astype(F32)
    lo = r - mid
    return (hi + pltpu.roll(mid, N_DH, axis=1) + pltpu.roll(lo, 2 * N_DH, axis=1)).astype(BF16)


def _unpack3(r):
    return r + pltpu.roll(r, LANES - N_DH, axis=1) + pltpu.roll(r, LANES - 2 * N_DH, axis=1)


def _conv_silu(u, prev_row, next_row, w, bias):
    n = u.shape[0]
    rows = _iota2(u.shape, 0)
    up = jnp.where(rows == 0, prev_row, pltpu.roll(u, 1, axis=0))
    un = jnp.where(rows == n - 1, next_row, pltpu.roll(u, n - 1, axis=0))
    y = w[0:1, :] * up + w[1:2, :] * u + w[2:3, :] * un + bias
    return y * _sigmoid(y)


def _softplus(v):
    return jnp.maximum(v, 0.0) + jnp.log1p(jnp.exp(-jnp.abs(v)))


def _ssd_dt_la(dt_raw, alog_ref, dtb_ref):
    lane = _iota2((1, LANES), 1)
    live = lane < N_DH
    a_row = jnp.where(live, -jnp.exp(alog_ref[...]) * LOG2E, 0.0)
    dt = jnp.where(live, _softplus(dt_raw + dtb_ref[...]), 0.0)
    return dt, dt * a_row


def _tri(upper):
    r = _iota2((CHUNK, CHUNK), 0)
    c = _iota2((CHUNK, CHUNK), 1)
    return jnp.where((c >= r) if upper else (c <= r), 1.0, 0.0).astype(BF16)


def _ssd_bstate_kernel(x_ref, xp_ref, xn_ref, bc_ref, bp_ref, bn_ref, dt_ref, cw_ref, cb_ref,
                       alog_ref, dtb_ref, exp_ref, sb_ref, xs_ref, bco_ref, s_ref, *, geom):
    s = pl.program_id(1)
    j = geom.bwd_step(s)

    @pl.when(s == 0)
    def _():
        s_ref[...] = jnp.zeros_like(s_ref)

    first = geom.seg_first(j)
    last = geom.seg_last(j)
    hl = SUBLANES - 1
    xs_all = _conv_silu(x_ref[...],
                        jnp.where(first, 0.0, xp_ref[hl:hl + 1, :]), jnp.where(last, 0.0, xn_ref[0:1, :]),
                        cw_ref[:, 0:SSD_W], cb_ref[:, 0:SSD_W])
    bc_all = _conv_silu(bc_ref[...],
                        jnp.where(first, 0.0, bp_ref[hl:hl + 1, :]), jnp.where(last, 0.0, bn_ref[0:1, :]),
                        cw_ref[:, SSD_W:], cb_ref[:, SSD_W:]).astype(BF16)
    xs_ref[...] = xs_all
    bco_ref[...] = bc_all
    dt_all, la_all = _ssd_dt_la(dt_ref[...], alog_ref, dtb_ref)
    lane = _iota2((CHUNK, LANES), 1)
    ex = exp_ref[:, SSD_W:2 * SSD_W]
    gw = SSD_W // SSD_GROUPS
    tri_u = _tri(True)
    for c in reversed(range(CPS)):
        rows = slice(c * CHUNK, (c + 1) * CHUNK)
        sb_ref[c] = s_ref[...]
        rb = jnp.where(lane < N_DH, _unpack3(_dot(tri_u, _pack3(la_all[rows]))), 0.0)
        rbx = _dot(_pack3(rb), ex)
        dtx = _dot(_pack3(dt_all[rows]), ex)
        xw = (xs_all[rows] * dtx * jnp.exp2(rbx[0:1, :] - rbx)).astype(BF16)
        cdec = jnp.exp2(rbx[0:1, :])
        for g in range(SSD_GROUPS):
            gs = slice(g * gw, (g + 1) * gw)
            bg = bc_all[rows, g * SSD_STATE:(g + 1) * SSD_STATE]
            s_ref[g] = s_ref[g] * cdec[:, gs] + _dot_tn(bg, xw[:, gs])


def _ssd_fwd_kernel(z_ref, xs_ref, bc_ref, dt_ref, sb_ref, alog_ref, dtb_ref, dsk_ref, nw_ref,
                    exp_ref, expl_ref, o_ref, s_ref):
    @pl.when(pl.program_id(1) == 0)
    def _():
        s_ref[...] = jnp.zeros_like(s_ref)

    dt_all, la_all = _ssd_dt_la(dt_ref[...], alog_ref, dtb_ref)
    lane = _iota2((CHUNK, LANES), 1)
    r = _iota2((CHUNK, CHUNK), 0)
    cidx = _iota2((CHUNK, CHUNK), 1)
    lower = r >= cidx
    diag = cidx == r
    lane64 = lane < SSD_HD
    fwd_lane = (lane % N_DH) < SSD_HEADS
    gw = SSD_W // SSD_GROUPS
    hpg = SSD_HEADS // SSD_GROUPS
    tri_l = _tri(False)
    tri_u = _tri(True)
    for c in range(CPS):
        rows = slice(c * CHUNK, (c + 1) * CHUNK)
        xs = xs_ref[rows, :]
        lap = _pack3(la_all[rows])
        cum = jnp.where(lane < N_DH,
                        _unpack3(jnp.where(fwd_lane, _dot(tri_l, lap), _dot(tri_u, lap))), 0.0)
        cum_t = cum.T
        cump = _pack3(cum)
        cumx = _dot(cump, exp_ref[...])
        dt = dt_all[rows]
        dt_t = dt.T
        xdt_f = xs * _dot(_pack3(dt), exp_ref[:, 0:SSD_W])

        cgs = []
        cbs = []
        for g in range(SSD_GROUPS):
            bg = bc_ref[rows, g * SSD_STATE:(g + 1) * SSD_STATE]
            cg = bc_ref[rows, SSD_BC_W + g * SSD_STATE:SSD_BC_W + (g + 1) * SSD_STATE]
            cgs.append((bg, cg))
            cbs.append(_dot_nt(cg, bg))
        col_f = _dot(cump, expl_ref[:, 0:SSD_HEADS * LANES])
        col_b = _dot(cump, expl_ref[:, SSD_HEADS * LANES:])
        y_parts = []
        for hp in range(SSD_HEADS // 2):
            g = (2 * hp) // hpg
            ps = slice(hp * LANES, (hp + 1) * LANES)
            lhs = []
            for e in range(2):
                h = 2 * hp + e
                hb = SSD_HEADS + h
                hs = slice(h * LANES, (h + 1) * LANES)
                dec = jnp.exp2(jnp.where(lower, col_f[:, hs] - cum_t[h:h + 1, :],
                                        col_b[:, hs] - cum_t[hb:hb + 1, :]))
                dtf = dt_t[h:h + 1, :]
                dtb = dt_t[hb:hb + 1, :]
                dts = jnp.where(diag, dtf + dtb, jnp.where(lower, dtf, dtb))
                lhs.append((cbs[g] * dec * dts).astype(BF16))
            xp = xs[:, ps]
            rhs = jnp.concatenate([jnp.where(lane64, xp, 0.0).astype(BF16),
                                   jnp.where(lane64, 0.0, xp).astype(BF16)], axis=0)
            y_parts.append(_dot(jnp.concatenate(lhs, axis=1), rhs))
        y = jnp.concatenate(y_parts, axis=1)

        e_f = jnp.exp2(cumx[:, 0:SSD_W])
        e_b = jnp.exp2(cumx[:, SSD_W:])
        yoff_f = jnp.concatenate([_dot(cgs[g][1], s_ref[g].astype(BF16)) for g in range(SSD_GROUPS)], axis=1)
        yoff_b = jnp.concatenate([_dot(cgs[g][1], sb_ref[c, g].astype(BF16)) for g in range(SSD_GROUPS)], axis=1)
        y = y + yoff_f * e_f + yoff_b * e_b

        tot = cumx[CHUNK - 1:CHUNK, 0:SSD_W]
        xw = (xdt_f * jnp.exp2(tot - cumx[:, 0:SSD_W])).astype(BF16)
        cdec = jnp.exp2(tot)
        for g in range(SSD_GROUPS):
            gs = slice(g * gw, (g + 1) * gw)
            s_ref[g] = s_ref[g] * cdec[:, gs] + _dot_tn(cgs[g][0], xw[:, gs])

        z = z_ref[rows, :]
        y = (y + dsk_ref[...] * xs) * (z * _sigmoid(z))
        o_ref[rows, :] = (_rms(y) * nw_ref[...]).astype(BF16)


def _expand_consts():
    kk = np.arange(N_DH)
    ex = np.zeros((LANES, 2 * SSD_W), np.float32)
    exl = np.zeros((LANES, N_DH * LANES), np.float32)
    for part in range(3):
        for p in range(SSD_HD):
            ex[part * N_DH + kk, kk * SSD_HD + p] = 1.0
        for q in range(LANES):
            exl[part * N_DH + kk, kk * LANES + q] = 1.0
    return jnp.asarray(ex, BF16), jnp.asarray(exl, BF16)


def _ssd(p, pdt, geom, conv_w, conv_b, alog, dtb, dsk, nw):
    b, nst = geom.b, geom.nst
    ex, exl = _expand_consts()
    zcol = 3072 // SSD_W
    xcol = 4096 // SSD_W
    bccol = (4096 + SSD_W) // (2 * SSD_BC_W)
    n8 = geom.rows // SUBLANES
    sub = RS // SUBLANES
    gw = SSD_W // SSD_GROUPS
    sb_block = (None, None, CPS, SSD_GROUPS, SSD_STATE, gw)

    def const(shape):
        return pl.BlockSpec(shape, lambda bi, s: (0,) * len(shape))

    state = pltpu.VMEM((SSD_GROUPS, SSD_STATE, gw), F32)
    rb = lambda bi, s: geom.row_block(bi, geom.bwd_step(s))
    prev8 = lambda bi, s: jnp.maximum(rb(bi, s) * sub - 1, 0)
    next8 = lambda bi, s: jnp.minimum(rb(bi, s) * sub + sub, n8 - 1)
    sb, xs, bcm = pl.pallas_call(
        functools.partial(_ssd_bstate_kernel, geom=geom),
        grid=(b, nst),
        in_specs=[pl.BlockSpec((RS, SSD_W), lambda bi, s: (rb(bi, s), xcol)),
                  pl.BlockSpec((SUBLANES, SSD_W), lambda bi, s: (prev8(bi, s), xcol)),
                  pl.BlockSpec((SUBLANES, SSD_W), lambda bi, s: (next8(bi, s), xcol)),
                  pl.BlockSpec((RS, 2 * SSD_BC_W), lambda bi, s: (rb(bi, s), bccol)),
                  pl.BlockSpec((SUBLANES, 2 * SSD_BC_W), lambda bi, s: (prev8(bi, s), bccol)),
                  pl.BlockSpec((SUBLANES, 2 * SSD_BC_W), lambda bi, s: (next8(bi, s), bccol)),
                  pl.BlockSpec((RS, LANES), lambda bi, s: (rb(bi, s), 0)),
                  const(conv_w.shape), const(conv_b.shape), const(alog.shape), const(dtb.shape),
                  const(ex.shape)],
        out_specs=[pl.BlockSpec(sb_block, lambda bi, s: (bi, geom.bwd_step(s), 0, 0, 0, 0)),
                   pl.BlockSpec((RS, SSD_W), lambda bi, s: (rb(bi, s), 0)),
                   pl.BlockSpec((RS, 2 * SSD_BC_W), lambda bi, s: (rb(bi, s), 0))],
        out_shape=[jax.ShapeDtypeStruct((b, nst, CPS, SSD_GROUPS, SSD_STATE, gw), F32),
                   jax.ShapeDtypeStruct((geom.rows, SSD_W), F32),
                   jax.ShapeDtypeStruct((geom.rows, 2 * SSD_BC_W), BF16)],
        scratch_shapes=[state],
        compiler_params=_cparams(("arbitrary", "arbitrary")),
        name="ssd_bstate",
    )(p, p, p, p, p, p, pdt, conv_w, conv_b, alog, dtb, ex)

    rf = lambda bi, s: geom.row_block(bi, s)
    return pl.pallas_call(
        _ssd_fwd_kernel,
        grid=(b, nst),
        in_specs=[pl.BlockSpec((RS, SSD_W), lambda bi, s: (rf(bi, s), zcol)),
                  pl.BlockSpec((RS, SSD_W), lambda bi, s: (rf(bi, s), 0)),
                  pl.BlockSpec((RS, 2 * SSD_BC_W), lambda bi, s: (rf(bi, s), 0)),
                  pl.BlockSpec((RS, LANES), lambda bi, s: (rf(bi, s), 0)),
                  pl.BlockSpec(sb_block, lambda bi, s: (bi, s, 0, 0, 0, 0)),
                  const(alog.shape), const(dtb.shape), const(dsk.shape), const(nw.shape),
                  const(ex.shape), const(exl.shape)],
        out_specs=pl.BlockSpec((RS, SSD_W), lambda bi, s: (rf(bi, s), 0)),
        out_shape=jax.ShapeDtypeStruct((geom.rows, SSD_W), BF16),
        scratch_shapes=[state],
        compiler_params=_cparams(("arbitrary", "arbitrary")),
        name="ssd_fwd",
    )(p, xs, bcm, pdt, sb, alog, dtb, dsk, nw, ex, exl)


def _rope_tables(t):
    half = ATT_HD // 2
    freqs = ROPE_BASE ** (-jnp.arange(0, half, 2, dtype=F32) / half)
    pos = jnp.arange(t)
    ang_r = (pos // GRID_W).astype(F32)[:, None] * freqs[None, :]
    ang_c = (pos % GRID_W).astype(F32)[:, None] * freqs[None, :]
    cos = jnp.concatenate([jnp.cos(ang_r), jnp.cos(ang_r), jnp.cos(ang_c), jnp.cos(ang_c)], axis=-1)
    sin = jnp.concatenate([-jnp.sin(ang_r), jnp.sin(ang_r), -jnp.sin(ang_c), jnp.sin(ang_c)], axis=-1)
    return cos, sin


def _pad_lanes(v):
    v = v.reshape(1, -1)
    return jnp.pad(v, ((0, 0), (0, LANES - v.shape[1])))


def _mixer(p, pdt, geom, layer, ctx_out, cos, sin, ret_log_decay, ret_norm_w, attn_sink, ssd_conv_w,
           ssd_conv_b, ssd_a_log, ssd_dt_bias, ssd_d, ssd_norm_w):
    ld = jnp.broadcast_to(ret_log_decay[layer][:, :, None], (2, RET_HEADS, LANES))
    ra = _retention(p, geom, ld, ret_norm_w[layer].reshape(1, RET_W))
    at_l, at_c = _attention(p, geom, attn_sink[layer], cos, sin, ctx_out)
    ss = _ssd(p, pdt, geom, ssd_conv_w[layer], ssd_conv_b[layer].reshape(1, -1),
              _pad_lanes(ssd_a_log[layer]), _pad_lanes(ssd_dt_bias[layer]),
              jnp.repeat(ssd_d[layer], SSD_HD).reshape(1, SSD_W),
              ssd_norm_w[layer].reshape(1, SSD_W))
    return ra, at_l, at_c, ss


def _forward(x, c, ctx, c_ctx, w_ada, b_ada, norm_w, ffn1_gu, ffn1_down, ffn2_gu, ffn2_down,
             w_in, w_out, ret_log_decay, ret_norm_w, attn_sink, ssd_conv_w, ssd_conv_b,
             ssd_a_log, ssd_dt_bias, ssd_d, ssd_norm_w):
    b, t, d = x.shape
    lc = ctx.shape[1]
    depth = w_ada.shape[0]
    geom = _Geom(b, t, lc)
    n_lat = b * t
    tm = math.gcd(512, math.gcd(t, b * lc))

    def mod_row(i):
        r0 = i * tm
        return jnp.where(r0 < n_lat, 1 + r0 // t, 0)

    cond = jnp.concatenate([c_ctx[None, :], c, jnp.zeros((SUBLANES - 1 - b, d), F32)], axis=0)
    mod = _adaln(cond, w_ada, b_ada)
    mod = mod.reshape(SUBLANES, depth, 3, 3, d).transpose(1, 2, 0, 3, 4).reshape(depth * 3, SUBLANES, 3, d)

    w1gu = ffn1_gu.astype(BF16)
    w1d = ffn1_down.astype(BF16)
    w2gu = ffn2_gu.astype(BF16)
    w2d = ffn2_down.astype(BF16)
    wi = w_in.astype(BF16)
    wi_dt = jnp.pad(wi[:, :, IN_MAIN:], ((0, 0), (0, 0), (0, LANES - N_DH)))
    wo = w_out.astype(BF16)

    cos, sin = _rope_tables(t)
    srcs = [x.reshape(n_lat, d), ctx.reshape(b * lc, d)]
    rows_all = n_lat + b * lc
    for layer in range(depth):
        last = layer == depth - 1
        xs = _ffn(srcs, rows_all, mod, norm_w, w1gu, w1d, layer=layer, sub=0, tm=tm, mod_row=mod_row)
        p, pdt = _proj_in(xs, mod, norm_w, wi, wi_dt, layer=layer, tm=tm, mod_row=mod_row)
        ra, at_l, at_c, ss = _mixer(p, pdt, geom, layer, not last, cos, sin, ret_log_decay, ret_norm_w,
                                    attn_sink, ssd_conv_w, ssd_conv_b, ssd_a_log, ssd_dt_bias, ssd_d,
                                    ssd_norm_w)
        at = at_l if last else jnp.concatenate([at_l, at_c], axis=0)
        rows_out = n_lat if last else rows_all
        xs = _proj_out(xs, rows_out, ra, at, ss, wo, mod, norm_w, layer=layer, tm=tm, mod_row=mod_row)
        xs = _ffn([xs], rows_out, mod, norm_w, w2gu, w2d, layer=layer, sub=2, tm=tm, mod_row=mod_row)
        srcs = [xs]
    return xs.reshape(b, t, d)


def kernel(x, c, ctx, c_ctx, w_ada, b_ada, norm_w, ffn1_gu, ffn1_down, ffn2_gu, ffn2_down,
           w_in, w_out, ret_log_decay, ret_norm_w, attn_sink, ssd_conv_w, ssd_conv_b,
           ssd_a_log, ssd_dt_bias, ssd_d, ssd_norm_w):
    return _forward(x, c, ctx, c_ctx, w_ada, b_ada, norm_w, ffn1_gu, ffn1_down, ffn2_gu, ffn2_down,
                    w_in, w_out, ret_log_decay, ret_norm_w, attn_sink, ssd_conv_w, ssd_conv_b,
                    ssd_a_log, ssd_dt_bias, ssd_d, ssd_norm_w)
```

```python
import functools
import math

import jax
import jax.numpy as jnp
import numpy as np
from jax import lax
from jax.experimental import pallas as pl
from jax.experimental.pallas import tpu as pltpu

F32 = jnp.float32
BF16 = jnp.bfloat16

D_MODEL = 2048
GRID_W = 64
RET_HEADS = 4
RET_DK = 128
RET_W = 512
ATT_HEADS = 4
ATT_KV_HEADS = 2
ATT_HD = 128
ATT_W = 512
ATT_KV_W = 256
WINDOW = 128
SSD_HEADS = 16
SSD_HD = 64
SSD_W = 1024
SSD_GROUPS = 2
SSD_STATE = 128
SSD_BC_W = 256
CHUNK = 128
D_FF = 5632
FFN_RES = 0.5
ROPE_BASE = 10000.0
NORM_EPS = 1e-6
N_MOD = 9
IN_MAIN = 5632
NEG_BIG = -1e30
LOG2E = 1.4426950408889634

CPS = 2
ATT_QB = 2
RS = CHUNK * CPS
LANES = 128
SUBLANES = 8

VMEM_LIMIT = 56 * 1024 * 1024


def _cparams(sem):
    return pltpu.CompilerParams(dimension_semantics=sem, vmem_limit_bytes=VMEM_LIMIT)


def _sigmoid(v):
    return 1.0 / (1.0 + jnp.exp(-v))


def _rms(v):
    return v * lax.rsqrt(jnp.mean(v * v, axis=-1, keepdims=True) + NORM_EPS)


def _dot(a, b):
    return jnp.dot(a, b, preferred_element_type=F32)


def _dot_nt(a, b):
    return lax.dot_general(a, b, (((1,), (1,)), ((), ())), preferred_element_type=F32)


def _dot_tn(a, b):
    return lax.dot_general(a, b, (((0,), (0,)), ((), ())), preferred_element_type=F32)


def _iota2(shape, dim):
    return lax.broadcasted_iota(jnp.int32, shape, dim)


SLAB = 16
SLAB_UNROLL = 4


def _row_slabs(n_rows, body):
    def step(r, carry):
        body(pl.ds(pl.multiple_of(r * SLAB, SLAB), SLAB))
        return carry

    lax.fori_loop(0, n_rows // SLAB, step, 0, unroll=SLAB_UNROLL)


def _modnorm_rows(x_ref, h_ref, nw_row, mod_ref, zero_ref=None):
    gain = nw_row * (1.0 + mod_ref[1:2, :])
    shift = mod_ref[0:1, :]

    def body(rows):
        h_ref[rows, :] = (_rms(x_ref[rows, :]) * gain + shift).astype(BF16)
        if zero_ref is not None:
            zero_ref[rows, :] = jnp.zeros((SLAB, zero_ref.shape[1]), zero_ref.dtype)

    _row_slabs(x_ref.shape[0], body)


def _residual_rows(x_ref, y_ref, gain, o_ref):
    def body(rows):
        o_ref[rows, :] = x_ref[rows, :] + _rms(y_ref[rows, :]) * gain

    _row_slabs(x_ref.shape[0], body)


def _adaln_kernel(c_ref, w_ref, b_ref, o_ref):
    cnd = c_ref[...]
    s = (cnd * _sigmoid(cnd)).astype(BF16)
    o_ref[...] = _dot(s, w_ref[...].astype(BF16)) + b_ref[...]


def _adaln(cond, w_ada, b_ada):
    depth, d, n = w_ada.shape
    tn = 1024
    nt = n // tn
    return pl.pallas_call(
        _adaln_kernel,
        grid=(depth, nt),
        in_specs=[
            pl.BlockSpec((SUBLANES, d), lambda l, j: (0, 0)),
            pl.BlockSpec((None, d, tn), lambda l, j: (l, 0, j)),
            pl.BlockSpec((None, 1, tn), lambda l, j: (l, 0, j)),
        ],
        out_specs=pl.BlockSpec((SUBLANES, tn), lambda l, j: (0, l * nt + j)),
        out_shape=jax.ShapeDtypeStruct((SUBLANES, depth * n), F32),
        compiler_params=_cparams(("arbitrary", "arbitrary")),
        name="adaln",
    )(cond, w_ada, b_ada.reshape(depth, 1, n))


def _ffn_kernel(*refs, sub, split):
    j = pl.program_id(1)
    is_first = j == 0
    is_last = j == pl.num_programs(1) - 1
    if split is None:
        x_ref, mod_ref, nw_ref, wg_ref, wu_ref, wd_ref, o_ref, h_ref, acc_ref = refs
        sources = [(x_ref, None)]
    else:
        xa_ref, xb_ref, mod_ref, nw_ref, wg_ref, wu_ref, wd_ref, o_ref, h_ref, acc_ref = refs
        from_a = pl.program_id(0) < split
        sources = [(xa_ref, from_a), (xb_ref, jnp.logical_not(from_a))]

    def prologue(x_ref):
        _modnorm_rows(x_ref, h_ref, nw_ref[2 * sub:2 * sub + 1, :], mod_ref, zero_ref=acc_ref)

    def epilogue(x_ref):
        gain = (FFN_RES * mod_ref[2:3, :]) * nw_ref[2 * sub + 1:2 * sub + 2, :]
        _residual_rows(x_ref, acc_ref, gain, o_ref)

    for x_ref, active in sources:
        pl.when(is_first if active is None else is_first & active)(functools.partial(prologue, x_ref))

    h = h_ref[...]
    g = _dot(h, wg_ref[...])
    u = _dot(h, wu_ref[...])
    a = (g * _sigmoid(g) * u).astype(BF16)
    acc_ref[...] += _dot(a, wd_ref[...])

    for x_ref, active in sources:
        pl.when(is_last if active is None else is_last & active)(functools.partial(epilogue, x_ref))


def _ffn(srcs, rows_out, mod, nw, w_gu, w_down, *, layer, sub, tm, mod_row):
    d = srcs[0].shape[1]
    tf = 512
    nf = D_FF // tf
    if len(srcs) == 1:
        split = None
        x_specs = [pl.BlockSpec((tm, d), lambda i, j: (i, 0))]
    else:
        split = srcs[0].shape[0] // tm
        x_specs = [pl.BlockSpec((tm, d), lambda i, j: (jnp.minimum(i, split - 1), 0)),
                   pl.BlockSpec((tm, d), lambda i, j: (jnp.maximum(i - split, 0), 0))]
    return pl.pallas_call(
        functools.partial(_ffn_kernel, sub=sub, split=split),
        grid=(rows_out // tm, nf),
        in_specs=x_specs + [
            pl.BlockSpec((None, None, 3, d), lambda i, j: (layer * 3 + sub, mod_row(i), 0, 0)),
            pl.BlockSpec((None, 6, d), lambda i, j: (layer, 0, 0)),
            pl.BlockSpec((None, d, tf), lambda i, j: (layer, 0, j)),
            pl.BlockSpec((None, d, tf), lambda i, j: (layer, 0, j + nf)),
            pl.BlockSpec((None, tf, d), lambda i, j: (layer, j, 0)),
        ],
        out_specs=pl.BlockSpec((tm, d), lambda i, j: (i, 0)),
        out_shape=jax.ShapeDtypeStruct((rows_out, d), F32),
        scratch_shapes=[pltpu.VMEM((tm, d), BF16), pltpu.VMEM((tm, d), F32)],
        compiler_params=_cparams(("arbitrary", "arbitrary")),
        name=f"ffn{sub}",
    )(*srcs, mod, nw, w_gu, w_gu, w_down)


def _proj_in_kernel(x_ref, mod_ref, nw_ref, w_ref, wdt_ref, o_ref, odt_ref, h_ref, *, tn):
    j = pl.program_id(1)

    @pl.when(j == 0)
    def _():
        _modnorm_rows(x_ref, h_ref, nw_ref[2:3, :], mod_ref)
        odt_ref[...] = _dot(h_ref[...], wdt_ref[...])

    def project(jj):
        o_ref[...] = _dot(h_ref[...], w_ref[:, jj * tn:(jj + 1) * tn])

    for jj in range(IN_MAIN // tn):
        pl.when(j == jj)(functools.partial(project, jj))


def _proj_in(xs, mod, nw, w_in, w_dt, *, layer, tm, mod_row):
    rows, d = xs.shape
    n = IN_MAIN
    tn = n // 2
    nfull = w_in.shape[2]
    return pl.pallas_call(
        functools.partial(_proj_in_kernel, tn=tn),
        grid=(rows // tm, n // tn),
        in_specs=[
            pl.BlockSpec((tm, d), lambda i, j: (i, 0)),
            pl.BlockSpec((None, None, 3, d), lambda i, j: (layer * 3 + 1, mod_row(i), 0, 0)),
            pl.BlockSpec((None, 6, d), lambda i, j: (layer, 0, 0)),
            pl.BlockSpec((None, d, nfull), lambda i, j: (layer, 0, 0), pipeline_mode=pl.Buffered(1)),
            pl.BlockSpec((None, d, LANES), lambda i, j: (layer, 0, 0)),
        ],
        out_specs=[pl.BlockSpec((tm, tn), lambda i, j: (i, j)),
                   pl.BlockSpec((tm, LANES), lambda i, j: (i, 0))],
        out_shape=[jax.ShapeDtypeStruct((rows, n), F32), jax.ShapeDtypeStruct((rows, LANES), F32)],
        scratch_shapes=[pltpu.VMEM((tm, d), BF16)],
        compiler_params=_cparams(("arbitrary", "arbitrary")),
        name="proj_in",
    )(xs, mod, nw, w_in, w_dt)


def _proj_out_kernel(x_ref, ra_ref, at_ref, ss_ref, w_ref, mod_ref, nw_ref, o_ref):
    y = _dot(ra_ref[...], w_ref[0:RET_W, :])
    y = y + _dot(at_ref[...], w_ref[RET_W:RET_W + ATT_W, :])
    y = y + _dot(ss_ref[...], w_ref[RET_W + ATT_W:, :])
    o_ref[...] = x_ref[...] + _rms(y) * (mod_ref[2:3, :] * nw_ref[3:4, :])


def _proj_out(xs, rows_out, ra, at, ss, w_out, mod, nw, *, layer, tm, mod_row):
    d = xs.shape[1]
    return pl.pallas_call(
        _proj_out_kernel,
        grid=(rows_out // tm,),
        in_specs=[
            pl.BlockSpec((tm, d), lambda i: (i, 0)),
            pl.BlockSpec((tm, RET_W), lambda i: (i, 0)),
            pl.BlockSpec((tm, ATT_W), lambda i: (i, 0)),
            pl.BlockSpec((tm, SSD_W), lambda i: (i, 0)),
            pl.BlockSpec((None, d, d), lambda i: (layer, 0, 0)),
            pl.BlockSpec((None, None, 3, d), lambda i: (layer * 3 + 1, mod_row(i), 0, 0)),
            pl.BlockSpec((None, 6, d), lambda i: (layer, 0, 0)),
        ],
        out_specs=pl.BlockSpec((tm, d), lambda i: (i, 0)),
        out_shape=jax.ShapeDtypeStruct((rows_out, d), F32),
        compiler_params=_cparams(("arbitrary",)),
        name="proj_out",
    )(xs, ra, at, ss, w_out, mod, nw)


class _Geom:
    def __init__(self, b, t, lc):
        assert t % RS == 0 and lc % RS == 0
        self.b, self.t, self.lc = b, t, lc
        self.nlat = t // RS
        self.nctx = lc // RS
        self.nst = self.nlat + self.nctx
        self.rows = b * (t + lc)

    def row_block(self, bi, j):
        return jnp.where(j < self.nctx,
                         self.b * self.nlat + bi * self.nctx + j,
                         bi * self.nlat + (j - self.nctx))

    def bwd_step(self, s):
        return jnp.where(s < self.nctx, self.nctx - 1 - s, self.nst + self.nctx - 1 - s)

    def seg_first(self, j):
        return (j == 0) | (j == self.nctx)

    def seg_last(self, j):
        return (j == self.nctx - 1) | (j == self.nst - 1)


def _ret_bstate_kernel(k_ref, v_ref, ld_ref, sb_ref, s_ref):
    @pl.when(pl.program_id(1) == 0)
    def _():
        s_ref[...] = jnp.zeros_like(s_ref)

    jrow = _iota2((CHUNK, CHUNK), 0).astype(F32)
    for h in range(RET_HEADS):
        hs = slice(h * RET_DK, (h + 1) * RET_DK)
        lgb = -jnp.abs(ld_ref[1, h:h + 1, :])
        kdec = jnp.exp(lgb * jrow)
        cdec = jnp.exp(lgb * float(CHUNK))
        for c in reversed(range(CPS)):
            rows = slice(c * CHUNK, (c + 1) * CHUNK)
            sb_ref[c, h] = s_ref[h]
            kd = (k_ref[rows, hs] * kdec).astype(BF16)
            s_ref[h] = s_ref[h] * cdec + _dot_tn(kd, v_ref[rows, hs].astype(BF16))


def _ret_fwd_kernel(q_ref, k_ref, v_ref, g_ref, sb_ref, ld_ref, nw_ref, o_ref, s_ref):
    @pl.when(pl.program_id(1) == 0)
    def _():
        s_ref[...] = jnp.zeros_like(s_ref)

    irow = _iota2((CHUNK, CHUNK), 0).astype(F32)
    rel = irow - _iota2((CHUNK, CHUNK), 1).astype(F32)
    for h in range(RET_HEADS):
        hs = slice(h * RET_DK, (h + 1) * RET_DK)
        lgf = -jnp.abs(ld_ref[0, h:h + 1, :])
        lgb = -jnp.abs(ld_ref[1, h:h + 1, :])
        dmask = (jnp.where(rel >= 0, jnp.exp(lgf * jnp.maximum(rel, 0.0)), 0.0)
                 + jnp.where(rel <= 0, jnp.exp(lgb * jnp.maximum(-rel, 0.0)), 0.0))
        qdec_f = jnp.exp(lgf * (irow + 1.0))
        qdec_b = jnp.exp(lgb * (float(CHUNK) - irow))
        kdec_f = jnp.exp(lgf * (float(CHUNK) - 1.0 - irow))
        cdec_f = jnp.exp(lgf * float(CHUNK))
        for c in range(CPS):
            rows = slice(c * CHUNK, (c + 1) * CHUNK)
            q = q_ref[rows, hs] * (RET_DK ** -0.5)
            k = k_ref[rows, hs]
            vb = v_ref[rows, hs].astype(BF16)
            inner = _dot_nt(q.astype(BF16), k.astype(BF16)) * dmask
            y = _dot(inner.astype(BF16), vb)
            y = y + _dot((q * qdec_f).astype(BF16), s_ref[h].astype(BF16))
            y = y + _dot((q * qdec_b).astype(BF16), sb_ref[c, h].astype(BF16))
            s_ref[h] = s_ref[h] * cdec_f + _dot_tn((k * kdec_f).astype(BF16), vb)
            mu = jnp.mean(y, axis=-1, keepdims=True)
            yc = y - mu
            var = jnp.mean(yc * yc, axis=-1, keepdims=True)
            gate = g_ref[rows, hs]
            yn = yc * lax.rsqrt(var + NORM_EPS) * nw_ref[:, hs]
            o_ref[rows, hs] = (yn * (gate * _sigmoid(gate))).astype(BF16)


def _retention(p, geom, ld, nw):
    b, nst = geom.b, geom.nst
    ld_spec = pl.BlockSpec((2, RET_HEADS, LANES), lambda bi, s: (0, 0, 0))
    state = pltpu.VMEM((RET_HEADS, RET_DK, RET_DK), F32)
    sb_block = (None, None, CPS, RET_HEADS, RET_DK, RET_DK)

    def blk(col, order):
        return pl.BlockSpec((RS, RET_W), lambda bi, s: (geom.row_block(bi, order(s)), col))

    sb = pl.pallas_call(
        _ret_bstate_kernel,
        grid=(b, nst),
        in_specs=[blk(1, geom.bwd_step), blk(2, geom.bwd_step), ld_spec],
        out_specs=pl.BlockSpec(sb_block, lambda bi, s: (bi, geom.bwd_step(s), 0, 0, 0, 0)),
        out_shape=jax.ShapeDtypeStruct((b, nst, CPS, RET_HEADS, RET_DK, RET_DK), F32),
        scratch_shapes=[state],
        compiler_params=_cparams(("arbitrary", "arbitrary")),
        name="ret_bstate",
    )(p, p, ld)

    ident = lambda s: s
    return pl.pallas_call(
        _ret_fwd_kernel,
        grid=(b, nst),
        in_specs=[blk(0, ident), blk(1, ident), blk(2, ident), blk(3, ident),
                  pl.BlockSpec(sb_block, lambda bi, s: (bi, s, 0, 0, 0, 0)),
                  ld_spec,
                  pl.BlockSpec((1, RET_W), lambda bi, s: (0, 0))],
        out_specs=pl.BlockSpec((RS, RET_W), lambda bi, s: (geom.row_block(bi, s), 0)),
        out_shape=jax.ShapeDtypeStruct((geom.rows, RET_W), BF16),
        scratch_shapes=[state],
        compiler_params=_cparams(("arbitrary", "arbitrary")),
        name="ret_fwd",
    )(p, p, p, p, sb, ld, nw)


def _rope(x, cos, sin):
    lane = _iota2(x.shape, 1)
    swapped = jnp.where((lane // 32) % 2 == 0, pltpu.roll(x, 96, axis=1), pltpu.roll(x, 32, axis=1))
    return x * cos + swapped * sin


def _att_kernel(q_ref, kl_ref, vl_ref, kc_ref, vc_ref, cos_ref, sin_ref, sink_ref, o_ref,
                kr_ref, vb_ref, *, t):
    step = pl.program_id(1)
    nb = pl.num_programs(1) * ATT_QB
    blk = CHUNK

    @pl.when(step == 0)
    def _():
        zeros = jnp.zeros((blk, ATT_KV_W), BF16)
        kr_ref[0:blk, :] = zeros
        kr_ref[t + blk:t + 2 * blk, :] = zeros
        vb_ref[0:blk, :] = zeros
        vb_ref[t + blk:t + 2 * blk, :] = zeros
        for h in range(ATT_KV_HEADS):
            hs = slice(h * ATT_HD, (h + 1) * ATT_HD)
            kr_ref[blk:t + blk, hs] = _rope(kl_ref[:, hs], cos_ref[...], sin_ref[...]).astype(BF16)
        vb_ref[blk:t + blk, :] = vl_ref[...].astype(BF16)

    scale = ATT_HD ** -0.5
    qi = _iota2((2 * blk, 3 * blk), 0) % blk
    kj = _iota2((2 * blk, 3 * blk), 1)
    in_window = jnp.abs(kj - blk - qi) <= WINDOW
    first_head = _iota2((2 * blk, 1), 0) < blk

    for qb in range(ATT_QB):
        n = step * ATT_QB + qb
        qs = slice(qb * blk, (qb + 1) * blk)
        row0 = pl.multiple_of(n * blk, blk)
        cos = cos_ref[pl.ds(row0, blk), :]
        sin = sin_ref[pl.ds(row0, blk), :]
        valid = in_window & ((kj >= blk) | (n > 0)) & ((kj < 2 * blk) | (n < nb - 1))
        for h in range(ATT_KV_HEADS):
            hs = slice(h * ATT_HD, (h + 1) * ATT_HD)
            q0 = _rope(q_ref[qs, 2 * h * ATT_HD:(2 * h + 1) * ATT_HD], cos, sin)
            q1 = _rope(q_ref[qs, (2 * h + 1) * ATT_HD:(2 * h + 2) * ATT_HD], cos, sin)
            q2 = jnp.concatenate([q0, q1], axis=0).astype(BF16)
            kw = kr_ref[pl.ds(row0, 3 * blk), hs]
            vw = vb_ref[pl.ds(row0, 3 * blk), hs]
            s_loc = jnp.where(valid, _dot_nt(q2, kw) * scale, NEG_BIG)
            s_cx = _dot_nt(q2, kc_ref[:, hs].astype(BF16)) * scale
            sink = jnp.where(first_head, sink_ref[2 * h], sink_ref[2 * h + 1])
            m = jnp.maximum(jnp.maximum(jnp.max(s_loc, axis=-1, keepdims=True),
                                        jnp.max(s_cx, axis=-1, keepdims=True)), sink)
            e_loc = jnp.exp(s_loc - m)
            e_cx = jnp.exp(s_cx - m)
            den = (jnp.sum(e_loc, axis=-1, keepdims=True) + jnp.sum(e_cx, axis=-1, keepdims=True)
                   + jnp.exp(sink - m))
            o = _dot(e_loc.astype(BF16), vw) + _dot(e_cx.astype(BF16), vc_ref[:, hs].astype(BF16))
            o = o / den
            o_ref[qs, 2 * h * ATT_HD:(2 * h + 1) * ATT_HD] = o[0:blk].astype(BF16)
            o_ref[qs, (2 * h + 1) * ATT_HD:(2 * h + 2) * ATT_HD] = o[blk:2 * blk].astype(BF16)


def _att_ctx_kernel(q_ref, kc_ref, vc_ref, sink_ref, o_ref):
    lc = q_ref.shape[0]
    scale = ATT_HD ** -0.5
    first_head = _iota2((2 * lc, 1), 0) < lc
    for h in range(ATT_KV_HEADS):
        hs = slice(h * ATT_HD, (h + 1) * ATT_HD)
        q2 = jnp.concatenate([q_ref[:, 2 * h * ATT_HD:(2 * h + 1) * ATT_HD],
                              q_ref[:, (2 * h + 1) * ATT_HD:(2 * h + 2) * ATT_HD]], axis=0).astype(BF16)
        s = _dot_nt(q2, kc_ref[:, hs].astype(BF16)) * scale
        sink = jnp.where(first_head, sink_ref[2 * h], sink_ref[2 * h + 1])
        m = jnp.maximum(jnp.max(s, axis=-1, keepdims=True), sink)
        e = jnp.exp(s - m)
        den = jnp.sum(e, axis=-1, keepdims=True) + jnp.exp(sink - m)
        o = _dot(e.astype(BF16), vc_ref[:, hs].astype(BF16)) / den
        o_ref[:, 2 * h * ATT_HD:(2 * h + 1) * ATT_HD] = o[0:lc].astype(BF16)
        o_ref[:, (2 * h + 1) * ATT_HD:(2 * h + 2) * ATT_HD] = o[lc:2 * lc].astype(BF16)


def _attention(p, geom, sink, cos, sin, ctx_out):
    b, t, lc = geom.b, geom.t, geom.lc
    rq = CHUNK * ATT_QB
    nb = t // rq
    ctx0 = (b * t) // lc
    qcol = 2048 // ATT_W
    kcol = (2048 + ATT_W) // ATT_KV_W
    vcol = kcol + 1
    smem = pl.BlockSpec(memory_space=pltpu.SMEM)
    out_l = pl.pallas_call(
        functools.partial(_att_kernel, t=t),
        grid=(b, nb),
        in_specs=[
            pl.BlockSpec((rq, ATT_W), lambda bi, n: (bi * nb + n, qcol)),
            pl.BlockSpec((t, ATT_KV_W), lambda bi, n: (bi, kcol)),
            pl.BlockSpec((t, ATT_KV_W), lambda bi, n: (bi, vcol)),
            pl.BlockSpec((lc, ATT_KV_W), lambda bi, n: (ctx0 + bi, kcol)),
            pl.BlockSpec((lc, ATT_KV_W), lambda bi, n: (ctx0 + bi, vcol)),
            pl.BlockSpec((t, ATT_HD), lambda bi, n: (0, 0)),
            pl.BlockSpec((t, ATT_HD), lambda bi, n: (0, 0)),
            smem,
        ],
        out_specs=pl.BlockSpec((rq, ATT_W), lambda bi, n: (bi * nb + n, 0)),
        out_shape=jax.ShapeDtypeStruct((b * t, ATT_W), BF16),
        scratch_shapes=[pltpu.VMEM((t + 2 * CHUNK, ATT_KV_W), BF16),
                        pltpu.VMEM((t + 2 * CHUNK, ATT_KV_W), BF16)],
        compiler_params=_cparams(("arbitrary", "arbitrary")),
        name="att_lat",
    )(p, p, p, p, p, cos, sin, sink)
    if not ctx_out:
        return out_l, None
    out_c = pl.pallas_call(
        _att_ctx_kernel,
        grid=(b,),
        in_specs=[
            pl.BlockSpec((lc, ATT_W), lambda bi: (ctx0 + bi, qcol)),
            pl.BlockSpec((lc, ATT_KV_W), lambda bi: (ctx0 + bi, kcol)),
            pl.BlockSpec((lc, ATT_KV_W), lambda bi: (ctx0 + bi, vcol)),
            smem,
        ],
        out_specs=pl.BlockSpec((lc, ATT_W), lambda bi: (bi, 0)),
        out_shape=jax.ShapeDtypeStruct((b * lc, ATT_W), BF16),
        compiler_params=_cparams(("arbitrary",)),
        name="att_ctx",
    )(p, p, p, sink)
    return out_l, out_c


N_DH = 2 * SSD_HEADS


def _pack3(v):
    hi = v.astype(BF16).astype(F32)
    r = v - hi
    mid = r.astype(BF16).astype(F32)
    lo = r - mid
    return (hi + pltpu.roll(mid, N_DH, axis=1) + pltpu.roll(lo, 2 * N_DH, axis=1)).astype(BF16)


def _unpack3(r):
    return r + pltpu.roll(r, LANES - N_DH, axis=1) + pltpu.roll(r, LANES - 2 * N_DH, axis=1)


def _conv_silu(u, prev_row, next_row, w, bias):
    n = u.shape[0]
    edge = _iota2((SUBLANES, u.shape[1]), 0)
    up = pltpu.roll(u, 1, axis=0)
    un = pltpu.roll(u, n - 1, axis=0)
    up = jnp.concatenate([jnp.where(edge == 0, prev_row, up[0:SUBLANES]), up[SUBLANES:]], axis=0)
    un = jnp.concatenate([un[0:n - SUBLANES], jnp.where(edge == SUBLANES - 1, next_row, un[n - SUBLANES:])], axis=0)
    y = w[0:1, :] * up + w[1:2, :] * u + w[2:3, :] * un + bias
    return y * _sigmoid(y)


def _softplus(v):
    return jnp.maximum(v, 0.0) + jnp.log1p(jnp.exp(-jnp.abs(v)))


def _ssd_dt_la(dt_raw, alog_ref, dtb_ref):
    lane = _iota2((1, LANES), 1)
    live = lane < N_DH
    a_row = jnp.where(live, -jnp.exp(alog_ref[...]) * LOG2E, 0.0)
    dt = jnp.where(live, _softplus(dt_raw + dtb_ref[...]), 0.0)
    return dt, dt * a_row


def _tri(upper):
    r = _iota2((CHUNK, CHUNK), 0)
    c = _iota2((CHUNK, CHUNK), 1)
    return jnp.where((c >= r) if upper else (c <= r), 1.0, 0.0).astype(BF16)


def _ssd_bstate_kernel(x_ref, xp_ref, xn_ref, bc_ref, bp_ref, bn_ref, dt_ref, cw_ref, cb_ref,
                       alog_ref, dtb_ref, exp_ref, sb_ref, xs_ref, bco_ref, s_ref, *, geom):
    s = pl.program_id(1)
    j = geom.bwd_step(s)

    @pl.when(s == 0)
    def _():
        s_ref[...] = jnp.zeros_like(s_ref)

    first = geom.seg_first(j)
    last = geom.seg_last(j)
    hl = SUBLANES - 1
    xs_all = _conv_silu(x_ref[...],
                        jnp.where(first, 0.0, xp_ref[hl:hl + 1, :]), jnp.where(last, 0.0, xn_ref[0:1, :]),
                        cw_ref[:, 0:SSD_W], cb_ref[:, 0:SSD_W])
    bc_all = _conv_silu(bc_ref[...],
                        jnp.where(first, 0.0, bp_ref[hl:hl + 1, :]), jnp.where(last, 0.0, bn_ref[0:1, :]),
                        cw_ref[:, SSD_W:], cb_ref[:, SSD_W:]).astype(BF16)
    xs_ref[...] = xs_all
    bco_ref[...] = bc_all
    dt_all, la_all = _ssd_dt_la(dt_ref[...], alog_ref, dtb_ref)
    lane = _iota2((CHUNK, LANES), 1)
    ex = exp_ref[:, SSD_W:2 * SSD_W]
    gw = SSD_W // SSD_GROUPS
    tri_u = _tri(True)
    for c in reversed(range(CPS)):
        rows = slice(c * CHUNK, (c + 1) * CHUNK)
        sb_ref[c] = s_ref[...]
        rb = jnp.where(lane < N_DH, _unpack3(_dot(tri_u, _pack3(la_all[rows]))), 0.0)
        rbx = _dot(_pack3(rb), ex)
        dtx = _dot(_pack3(dt_all[rows]), ex)
        xw = (xs_all[rows] * dtx * jnp.exp2(rbx[0:1, :] - rbx)).astype(BF16)
        cdec = jnp.exp2(rbx[0:1, :])
        for g in range(SSD_GROUPS):
            gs = slice(g * gw, (g + 1) * gw)
            bg = bc_all[rows, g * SSD_STATE:(g + 1) * SSD_STATE]
            s_ref[g] = s_ref[g] * cdec[:, gs] + _dot_tn(bg, xw[:, gs])


def _ssd_fwd_kernel(z_ref, xs_ref, bc_ref, dt_ref, sb_ref, alog_ref, dtb_ref, dsk_ref, nw_ref,
                    exp_ref, expl_ref, o_ref, s_ref):
    @pl.when(pl.program_id(1) == 0)
    def _():
        s_ref[...] = jnp.zeros_like(s_ref)

    dt_all, la_all = _ssd_dt_la(dt_ref[...], alog_ref, dtb_ref)
    lane = _iota2((CHUNK, LANES), 1)
    r = _iota2((CHUNK, CHUNK), 0)
    cidx = _iota2((CHUNK, CHUNK), 1)
    lower = r >= cidx
    diag = cidx == r
    lane64 = lane < SSD_HD
    fwd_lane = (lane % N_DH) < SSD_HEADS
    gw = SSD_W // SSD_GROUPS
    hpg = SSD_HEADS // SSD_GROUPS
    tri_l = _tri(False)
    tri_u = _tri(True)
    for c in range(CPS):
        rows = slice(c * CHUNK, (c + 1) * CHUNK)
        xs = xs_ref[rows, :]
        lap = _pack3(la_all[rows])
        cum = jnp.where(lane < N_DH,
                        _unpack3(jnp.where(fwd_lane, _dot(tri_l, lap), _dot(tri_u, lap))), 0.0)
        cum_t = cum.T
        cump = _pack3(cum)
        cumx = _dot(cump, exp_ref[...])
        dt = dt_all[rows]
        dt_t = dt.T
        xdt_f = xs * _dot(_pack3(dt), exp_ref[:, 0:SSD_W])

        cgs = []
        cbs = []
        for g in range(SSD_GROUPS):
            bg = bc_ref[rows, g * SSD_STATE:(g + 1) * SSD_STATE]
            cg = bc_ref[rows, SSD_BC_W + g * SSD_STATE:SSD_BC_W + (g + 1) * SSD_STATE]
            cgs.append((bg, cg))
            cbs.append(_dot_nt(cg, bg))
        col_f = _dot(cump, expl_ref[:, 0:SSD_HEADS * LANES])
        col_b = _dot(cump, expl_ref[:, SSD_HEADS * LANES:])
        y_parts = []
        for hp in range(SSD_HEADS // 2):
            g = (2 * hp) // hpg
            ps = slice(hp * LANES, (hp + 1) * LANES)
            lhs = []
            for e in range(2):
                h = 2 * hp + e
                hb = SSD_HEADS + h
                hs = slice(h * LANES, (h + 1) * LANES)
                dec = jnp.exp2(jnp.where(lower, col_f[:, hs] - cum_t[h:h + 1, :],
                                        col_b[:, hs] - cum_t[hb:hb + 1, :]))
                dtf = dt_t[h:h + 1, :]
                dtb = dt_t[hb:hb + 1, :]
                dts = jnp.where(diag, dtf + dtb, jnp.where(lower, dtf, dtb))
                lhs.append((cbs[g] * dec * dts).astype(BF16))
            xp = xs[:, ps]
            rhs = jnp.concatenate([jnp.where(lane64, xp, 0.0).astype(BF16),
                                   jnp.where(lane64, 0.0, xp).astype(BF16)], axis=0)
            y_parts.append(_dot(jnp.concatenate(lhs, axis=1), rhs))
        y = jnp.concatenate(y_parts, axis=1)

        e_f = jnp.exp2(cumx[:, 0:SSD_W])
        e_b = jnp.exp2(cumx[:, SSD_W:])
        yoff_f = jnp.concatenate([_dot(cgs[g][1], s_ref[g].astype(BF16)) for g in range(SSD_GROUPS)], axis=1)
        yoff_b = jnp.concatenate([_dot(cgs[g][1], sb_ref[c, g].astype(BF16)) for g in range(SSD_GROUPS)], axis=1)
        y = y + yoff_f * e_f + yoff_b * e_b

        tot = cumx[CHUNK - 1:CHUNK, 0:SSD_W]
        xw = (xdt_f * jnp.exp2(tot - cumx[:, 0:SSD_W])).astype(BF16)
        cdec = jnp.exp2(tot)
        for g in range(SSD_GROUPS):
            gs = slice(g * gw, (g + 1) * gw)
            s_ref[g] = s_ref[g] * cdec[:, gs] + _dot_tn(cgs[g][0], xw[:, gs])

        z = z_ref[rows, :]
        y = (y + dsk_ref[...] * xs) * (z * _sigmoid(z))
        o_ref[rows, :] = (_rms(y) * nw_ref[...]).astype(BF16)


def _expand_consts():
    kk = np.arange(N_DH)
    ex = np.zeros((LANES, 2 * SSD_W), np.float32)
    exl = np.zeros((LANES, N_DH * LANES), np.float32)
    for part in range(3):
        for p in range(SSD_HD):
            ex[part * N_DH + kk, kk * SSD_HD + p] = 1.0
        for q in range(LANES):
            exl[part * N_DH + kk, kk * LANES + q] = 1.0
    return jnp.asarray(ex, BF16), jnp.asarray(exl, BF16)


def _ssd(p, pdt, geom, conv_w, conv_b, alog, dtb, dsk, nw):
    b, nst = geom.b, geom.nst
    ex, exl = _expand_consts()
    zcol = 3072 // SSD_W
    xcol = 4096 // SSD_W
    bccol = (4096 + SSD_W) // (2 * SSD_BC_W)
    n8 = geom.rows // SUBLANES
    sub = RS // SUBLANES
    gw = SSD_W // SSD_GROUPS
    sb_block = (None, None, CPS, SSD_GROUPS, SSD_STATE, gw)

    def const(shape):
        return pl.BlockSpec(shape, lambda bi, s: (0,) * len(shape))

    state = pltpu.VMEM((SSD_GROUPS, SSD_STATE, gw), F32)
    rb = lambda bi, s: geom.row_block(bi, geom.bwd_step(s))
    prev8 = lambda bi, s: jnp.maximum(rb(bi, s) * sub - 1, 0)
    next8 = lambda bi, s: jnp.minimum(rb(bi, s) * sub + sub, n8 - 1)
    sb, xs, bcm = pl.pallas_call(
        functools.partial(_ssd_bstate_kernel, geom=geom),
        grid=(b, nst),
        in_specs=[pl.BlockSpec((RS, SSD_W), lambda bi, s: (rb(bi, s), xcol)),
                  pl.BlockSpec((SUBLANES, SSD_W), lambda bi, s: (prev8(bi, s), xcol)),
                  pl.BlockSpec((SUBLANES, SSD_W), lambda bi, s: (next8(bi, s), xcol)),
                  pl.BlockSpec((RS, 2 * SSD_BC_W), lambda bi, s: (rb(bi, s), bccol)),
                  pl.BlockSpec((SUBLANES, 2 * SSD_BC_W), lambda bi, s: (prev8(bi, s), bccol)),
                  pl.BlockSpec((SUBLANES, 2 * SSD_BC_W), lambda bi, s: (next8(bi, s), bccol)),
                  pl.BlockSpec((RS, LANES), lambda bi, s: (rb(bi, s), 0)),
                  const(conv_w.shape), const(conv_b.shape), const(alog.shape), const(dtb.shape),
                  const(ex.shape)],
        out_specs=[pl.BlockSpec(sb_block, lambda bi, s: (bi, geom.bwd_step(s), 0, 0, 0, 0)),
                   pl.BlockSpec((RS, SSD_W), lambda bi, s: (rb(bi, s), 0)),
                   pl.BlockSpec((RS, 2 * SSD_BC_W), lambda bi, s: (rb(bi, s), 0))],
        out_shape=[jax.ShapeDtypeStruct((b, nst, CPS, SSD_GROUPS, SSD_STATE, gw), F32),
                   jax.ShapeDtypeStruct((geom.rows, SSD_W), F32),
                   jax.ShapeDtypeStruct((geom.rows, 2 * SSD_BC_W), BF16)],
        scratch_shapes=[state],
        compiler_params=_cparams(("arbitrary", "arbitrary")),
        name="ssd_bstate",
    )(p, p, p, p, p, p, pdt, conv_w, conv_b, alog, dtb, ex)

    rf = lambda bi, s: geom.row_block(bi, s)
    return pl.pallas_call(
        _ssd_fwd_kernel,
        grid=(b, nst),
        in_specs=[pl.BlockSpec((RS, SSD_W), lambda bi, s: (rf(bi, s), zcol)),
                  pl.BlockSpec((RS, SSD_W), lambda bi, s: (rf(bi, s), 0)),
                  pl.BlockSpec((RS, 2 * SSD_BC_W), lambda bi, s: (rf(bi, s), 0)),
                  pl.BlockSpec((RS, LANES), lambda bi, s: (rf(bi, s), 0)),
                  pl.BlockSpec(sb_block, lambda bi, s: (bi, s, 0, 0, 0, 0)),
                  const(alog.shape), const(dtb.shape), const(dsk.shape), const(nw.shape),
                  const(ex.shape), const(exl.shape)],
        out_specs=pl.BlockSpec((RS, SSD_W), lambda bi, s: (rf(bi, s), 0)),
        out_shape=jax.ShapeDtypeStruct((geom.rows, SSD_W), BF16),
        scratch_shapes=[state],
        compiler_params=_cparams(("arbitrary", "arbitrary")),
        name="ssd_fwd",
    )(p, xs, bcm, pdt, sb, alog, dtb, dsk, nw, ex, exl)


def _rope_tables(t):
    half = ATT_HD // 2
    freqs = ROPE_BASE ** (-jnp.arange(0, half, 2, dtype=F32) / half)
    pos = jnp.arange(t)
    ang_r = (pos // GRID_W).astype(F32)[:, None] * freqs[None, :]
    ang_c = (pos % GRID_W).astype(F32)[:, None] * freqs[None, :]
    cos = jnp.concatenate([jnp.cos(ang_r), jnp.cos(ang_r), jnp.cos(ang_c), jnp.cos(ang_c)], axis=-1)
    sin = jnp.concatenate([-jnp.sin(ang_r), jnp.sin(ang_r), -jnp.sin(ang_c), jnp.sin(ang_c)], axis=-1)
    return cos, sin


def _pad_lanes(v):
    v = v.reshape(1, -1)
    return jnp.pad(v, ((0, 0), (0, LANES - v.shape[1])))


def _mixer(p, pdt, geom, layer, ctx_out, cos, sin, ret_log_decay, ret_norm_w, attn_sink, ssd_conv_w,
           ssd_conv_b, ssd_a_log, ssd_dt_bias, ssd_d, ssd_norm_w):
    ld = jnp.broadcast_to(ret_log_decay[layer][:, :, None], (2, RET_HEADS, LANES))
    ra = _retention(p, geom, ld, ret_norm_w[layer].reshape(1, RET_W))
    at_l, at_c = _attention(p, geom, attn_sink[layer], cos, sin, ctx_out)
    ss = _ssd(p, pdt, geom, ssd_conv_w[layer], ssd_conv_b[layer].reshape(1, -1),
              _pad_lanes(ssd_a_log[layer]), _pad_lanes(ssd_dt_bias[layer]),
              jnp.repeat(ssd_d[layer], SSD_HD).reshape(1, SSD_W),
              ssd_norm_w[layer].reshape(1, SSD_W))
    return ra, at_l, at_c, ss


def _forward(x, c, ctx, c_ctx, w_ada, b_ada, norm_w, ffn1_gu, ffn1_down, ffn2_gu, ffn2_down,
             w_in, w_out, ret_log_decay, ret_norm_w, attn_sink, ssd_conv_w, ssd_conv_b,
             ssd_a_log, ssd_dt_bias, ssd_d, ssd_norm_w):
    b, t, d = x.shape
    lc = ctx.shape[1]
    depth = w_ada.shape[0]
    geom = _Geom(b, t, lc)
    n_lat = b * t
    tm = math.gcd(512, math.gcd(t, b * lc))

    def mod_row(i):
        r0 = i * tm
        return jnp.where(r0 < n_lat, 1 + r0 // t, 0)

    cond = jnp.concatenate([c_ctx[None, :], c, jnp.zeros((SUBLANES - 1 - b, d), F32)], axis=0)
    mod = _adaln(cond, w_ada, b_ada)
    mod = mod.reshape(SUBLANES, depth, 3, 3, d).transpose(1, 2, 0, 3, 4).reshape(depth * 3, SUBLANES, 3, d)

    w1gu = ffn1_gu.astype(BF16)
    w1d = ffn1_down.astype(BF16)
    w2gu = ffn2_gu.astype(BF16)
    w2d = ffn2_down.astype(BF16)
    wi = w_in.astype(BF16)
    wi_dt = jnp.pad(wi[:, :, IN_MAIN:], ((0, 0), (0, 0), (0, LANES - N_DH)))
    wo = w_out.astype(BF16)

    cos, sin = _rope_tables(t)
    srcs = [x.reshape(n_lat, d), ctx.reshape(b * lc, d)]
    rows_all = n_lat + b * lc
    for layer in range(depth):
        last = layer == depth - 1
        xs = _ffn(srcs, rows_all, mod, norm_w, w1gu, w1d, layer=layer, sub=0, tm=tm, mod_row=mod_row)
        p, pdt = _proj_in(xs, mod, norm_w, wi, wi_dt, layer=layer, tm=tm, mod_row=mod_row)
        ra, at_l, at_c, ss = _mixer(p, pdt, geom, layer, not last, cos, sin, ret_log_decay, ret_norm_w,
                                    attn_sink, ssd_conv_w, ssd_conv_b, ssd_a_log, ssd_dt_bias, ssd_d,
                                    ssd_norm_w)
        at = at_l if last else jnp.concatenate([at_l, at_c], axis=0)
        rows_out = n_lat if last else rows_all
        xs = _proj_out(xs, rows_out, ra, at, ss, wo, mod, norm_w, layer=layer, tm=tm, mod_row=mod_row)
        xs = _ffn([xs], rows_out, mod, norm_w, w2gu, w2d, layer=layer, sub=2, tm=tm, mod_row=mod_row)
        srcs = [xs]
    return xs.reshape(b, t, d)


def kernel(x, c, ctx, c_ctx, w_ada, b_ada, norm_w, ffn1_gu, ffn1_down, ffn2_gu, ffn2_down,
           w_in, w_out, ret_log_decay, ret_norm_w, attn_sink, ssd_conv_w, ssd_conv_b,
           ssd_a_log, ssd_dt_bias, ssd_d, ssd_norm_w):
    return _forward(x, c, ctx, c_ctx, w_ada, b_ada, norm_w, ffn1_gu, ffn1_down, ffn2_gu, ffn2_down,
                    w_in, w_out, ret_log_decay, ret_norm_w, attn_sink, ssd_conv_w, ssd_conv_b,
                    ssd_a_log, ssd_dt_bias, ssd_d, ssd_norm_w)
```

```python
import functools
import math

import jax
import jax.numpy as jnp
import numpy as np
from jax import lax
from jax.experimental import pallas as pl
from jax.experimental.pallas import tpu as pltpu

F32 = jnp.float32
BF16 = jnp.bfloat16

D_MODEL = 2048
GRID_W = 64
RET_HEADS = 4
RET_DK = 128
RET_W = 512
ATT_HEADS = 4
ATT_KV_HEADS = 2
ATT_HD = 128
ATT_W = 512
ATT_KV_W = 256
WINDOW = 128
SSD_HEADS = 16
SSD_HD = 64
SSD_W = 1024
SSD_GROUPS = 2
SSD_STATE = 128
SSD_BC_W = 256
CHUNK = 128
D_FF = 5632
FFN_RES = 0.5
ROPE_BASE = 10000.0
NORM_EPS = 1e-6
N_MOD = 9
IN_MAIN = 5632
NEG_BIG = -1e30
LOG2E = 1.4426950408889634

CPS = 2
ATT_QB = 2
RS = CHUNK * CPS
LANES = 128
SUBLANES = 8

VMEM_LIMIT = 56 * 1024 * 1024


def _cparams(sem):
    return pltpu.CompilerParams(dimension_semantics=sem, vmem_limit_bytes=VMEM_LIMIT)


def _sigmoid(v):
    return 1.0 / (1.0 + jnp.exp(-v))


def _rms(v):
    return v * lax.rsqrt(jnp.mean(v * v, axis=-1, keepdims=True) + NORM_EPS)


def _dot(a, b):
    return jnp.dot(a, b, preferred_element_type=F32)


def _dot_nt(a, b):
    return lax.dot_general(a, b, (((1,), (1,)), ((), ())), preferred_element_type=F32)


def _dot_tn(a, b):
    return lax.dot_general(a, b, (((0,), (0,)), ((), ())), preferred_element_type=F32)


def _iota2(shape, dim):
    return lax.broadcasted_iota(jnp.int32, shape, dim)


SLAB = 16
SLAB_UNROLL = 4


def _row_slabs(n_rows, body, straight_line=False):
    if straight_line:
        for r in range(n_rows // SLAB):
            body(pl.ds(r * SLAB, SLAB))
        return

    def step(r, carry):
        body(pl.ds(pl.multiple_of(r * SLAB, SLAB), SLAB))
        return carry

    lax.fori_loop(0, n_rows // SLAB, step, 0, unroll=SLAB_UNROLL)


def _modnorm_rows(x_ref, h_ref, nw_row, mod_ref, zero_ref=None, straight_line=False):
    gain = nw_row * (1.0 + mod_ref[1:2, :])
    shift = mod_ref[0:1, :]

    def body(rows):
        h_ref[rows, :] = (_rms(x_ref[rows, :]) * gain + shift).astype(BF16)
        if zero_ref is not None:
            zero_ref[rows, :] = jnp.zeros((SLAB, zero_ref.shape[1]), zero_ref.dtype)

    _row_slabs(x_ref.shape[0], body, straight_line)


def _residual_rows(x_ref, y_ref, gain, o_ref, straight_line=False):
    def body(rows):
        o_ref[rows, :] = x_ref[rows, :] + _rms(y_ref[rows, :]) * gain

    _row_slabs(x_ref.shape[0], body, straight_line)


def _adaln_kernel(c_ref, w_ref, b_ref, o_ref):
    cnd = c_ref[...]
    s = (cnd * _sigmoid(cnd)).astype(BF16)
    o_ref[...] = _dot(s, w_ref[...].astype(BF16)) + b_ref[...]


def _adaln(cond, w_ada, b_ada):
    depth, d, n = w_ada.shape
    tn = 1024
    nt = n // tn
    return pl.pallas_call(
        _adaln_kernel,
        grid=(depth, nt),
        in_specs=[
            pl.BlockSpec((SUBLANES, d), lambda l, j: (0, 0)),
            pl.BlockSpec((None, d, tn), lambda l, j: (l, 0, j)),
            pl.BlockSpec((None, 1, tn), lambda l, j: (l, 0, j)),
        ],
        out_specs=pl.BlockSpec((SUBLANES, tn), lambda l, j: (0, l * nt + j)),
        out_shape=jax.ShapeDtypeStruct((SUBLANES, depth * n), F32),
        compiler_params=_cparams(("arbitrary", "arbitrary")),
        name="adaln",
    )(cond, w_ada, b_ada.reshape(depth, 1, n))


def _ffn_kernel(*refs, sub, split):
    j = pl.program_id(1)
    is_first = j == 0
    is_last = j == pl.num_programs(1) - 1
    if split is None:
        x_ref, mod_ref, nw_ref, wg_ref, wu_ref, wd_ref, o_ref, h_ref, acc_ref = refs
        sources = [(x_ref, None)]
    else:
        xa_ref, xb_ref, mod_ref, nw_ref, wg_ref, wu_ref, wd_ref, o_ref, h_ref, acc_ref = refs
        from_a = pl.program_id(0) < split
        sources = [(xa_ref, from_a), (xb_ref, jnp.logical_not(from_a))]

    def prologue(x_ref):
        _modnorm_rows(x_ref, h_ref, nw_ref[2 * sub:2 * sub + 1, :], mod_ref, zero_ref=acc_ref)

    def epilogue(x_ref):
        gain = (FFN_RES * mod_ref[2:3, :]) * nw_ref[2 * sub + 1:2 * sub + 2, :]
        _residual_rows(x_ref, acc_ref, gain, o_ref)

    for x_ref, active in sources:
        pl.when(is_first if active is None else is_first & active)(functools.partial(prologue, x_ref))

    h = h_ref[...]
    g = _dot(h, wg_ref[...])
    u = _dot(h, wu_ref[...])
    a = (g * _sigmoid(g) * u).astype(BF16)
    acc_ref[...] += _dot(a, wd_ref[...])

    for x_ref, active in sources:
        pl.when(is_last if active is None else is_last & active)(functools.partial(epilogue, x_ref))


def _ffn_lag_kernel(xp_ref, xn_ref, modp_ref, modn_ref, nw_ref, wg_ref, wu_ref, wd_ref, o_ref,
                    h_ref, acc_ref, a0_ref, a1_ref, *, sub, nf):
    i = pl.program_id(0)
    jj = pl.program_id(1)
    last_tile = pl.num_programs(0) - 1
    slot = i % 2
    nw_in = nw_ref[2 * sub:2 * sub + 1, :]
    nw_out = nw_ref[2 * sub + 1:2 * sub + 2, :]
    a_of_chunk = lambda k: a0_ref if k % 2 == 0 else a1_ref

    def gate_up(a_ref):
        h = h_ref[slot]
        g = _dot(h, wg_ref[...])
        u = _dot(h, wu_ref[...])
        a_ref[...] = (g * _sigmoid(g) * u).astype(BF16)

    def finish(x_ref, mod_ref, acc_slot, straight_line):
        gain = (FFN_RES * mod_ref[2:3, :]) * nw_out
        _residual_rows(x_ref, acc_ref.at[acc_slot], gain, o_ref, straight_line)

    @pl.when((jj == 0) & (i == 0))
    def _():
        _modnorm_rows(xn_ref, h_ref.at[0], nw_in, modp_ref)
        gate_up(a_of_chunk(0))

    @pl.when((jj == 0) & (i > 0))
    def _():
        gate_up(a_of_chunk(0))
        finish(xp_ref, modp_ref, 1 - slot, True)

    @pl.when(jj == 1)
    def _():
        gate_up(a_of_chunk(1))
        acc_ref[slot] = _dot(a_of_chunk(0)[...], wd_ref[...])

    odd = (jj % 2) == 1
    mid = (jj > 1) & (jj < nf)

    @pl.when(mid & odd)
    def _():
        gate_up(a_of_chunk(1))
        acc_ref[slot] += _dot(a_of_chunk(0)[...], wd_ref[...])

    @pl.when(mid & jnp.logical_not(odd))
    def _():
        gate_up(a_of_chunk(0))
        acc_ref[slot] += _dot(a_of_chunk(1)[...], wd_ref[...])

    @pl.when(jj == nf)
    def _():
        acc_ref[slot] += _dot(a_of_chunk(nf - 1)[...], wd_ref[...])
        _modnorm_rows(xn_ref, h_ref.at[1 - slot], nw_in, modn_ref, straight_line=True)

    @pl.when((jj == nf) & (i == last_tile))
    def _():
        finish(xn_ref, modn_ref, slot, False)


def _ffn_lag(xs, rows_out, mod, nw, w_gu, w_down, *, layer, sub, tm, mod_row):
    d = xs.shape[1]
    tf = 512
    nf = D_FF // tf
    nt = rows_out // tm
    prev_tile = lambda i, j: jnp.maximum(i - (j == 0).astype(jnp.int32), 0)
    next_tile = lambda i, j: jnp.minimum(i + (j >= 1).astype(jnp.int32), nt - 1)
    out_tile = lambda i, j: jnp.where((i == nt - 1) & (j == nf), nt - 1, jnp.maximum(i - 1, 0))
    gu = lambda j: jnp.minimum(j, nf - 1)
    msub = layer * 3 + sub
    return pl.pallas_call(
        functools.partial(_ffn_lag_kernel, sub=sub, nf=nf),
        grid=(nt, nf + 1),
        in_specs=[
            pl.BlockSpec((tm, d), lambda i, j: (prev_tile(i, j), 0)),
            pl.BlockSpec((tm, d), lambda i, j: (next_tile(i, j), 0)),
            pl.BlockSpec((None, None, 3, d), lambda i, j: (msub, mod_row(jnp.maximum(i - 1, 0)), 0, 0)),
            pl.BlockSpec((None, None, 3, d), lambda i, j: (msub, mod_row(jnp.minimum(i + 1, nt - 1)), 0, 0)),
            pl.BlockSpec((None, 6, d), lambda i, j: (layer, 0, 0)),
            pl.BlockSpec((None, d, tf), lambda i, j: (layer, 0, gu(j))),
            pl.BlockSpec((None, d, tf), lambda i, j: (layer, 0, gu(j) + nf)),
            pl.BlockSpec((None, tf, d), lambda i, j: (layer, jnp.maximum(j - 1, 0), 0)),
        ],
        out_specs=pl.BlockSpec((tm, d), lambda i, j: (out_tile(i, j), 0)),
        out_shape=jax.ShapeDtypeStruct((rows_out, d), F32),
        scratch_shapes=[pltpu.VMEM((2, tm, d), BF16), pltpu.VMEM((2, tm, d), F32),
                        pltpu.VMEM((tm, tf), BF16), pltpu.VMEM((tm, tf), BF16)],
        compiler_params=_cparams(("arbitrary", "arbitrary")),
        name=f"ffn{sub}",
    )(xs, xs, mod, mod, nw, w_gu, w_gu, w_down)


def _ffn(srcs, rows_out, mod, nw, w_gu, w_down, *, layer, sub, tm, mod_row):
    if len(srcs) == 1:
        return _ffn_lag(srcs[0], rows_out, mod, nw, w_gu, w_down, layer=layer, sub=sub, tm=tm, mod_row=mod_row)
    d = srcs[0].shape[1]
    tf = 512
    nf = D_FF // tf
    if len(srcs) == 1:
        split = None
        x_specs = [pl.BlockSpec((tm, d), lambda i, j: (i, 0))]
    else:
        split = srcs[0].shape[0] // tm
        x_specs = [pl.BlockSpec((tm, d), lambda i, j: (jnp.minimum(i, split - 1), 0)),
                   pl.BlockSpec((tm, d), lambda i, j: (jnp.maximum(i - split, 0), 0))]
    return pl.pallas_call(
        functools.partial(_ffn_kernel, sub=sub, split=split),
        grid=(rows_out // tm, nf),
        in_specs=x_specs + [
            pl.BlockSpec((None, None, 3, d), lambda i, j: (layer * 3 + sub, mod_row(i), 0, 0)),
            pl.BlockSpec((None, 6, d), lambda i, j: (layer, 0, 0)),
            pl.BlockSpec((None, d, tf), lambda i, j: (layer, 0, j)),
            pl.BlockSpec((None, d, tf), lambda i, j: (layer, 0, j + nf)),
            pl.BlockSpec((None, tf, d), lambda i, j: (layer, j, 0)),
        ],
        out_specs=pl.BlockSpec((tm, d), lambda i, j: (i, 0)),
        out_shape=jax.ShapeDtypeStruct((rows_out, d), F32),
        scratch_shapes=[pltpu.VMEM((tm, d), BF16), pltpu.VMEM((tm, d), F32)],
        compiler_params=_cparams(("arbitrary", "arbitrary")),
        name=f"ffn{sub}",
    )(*srcs, mod, nw, w_gu, w_gu, w_down)


def _proj_in_kernel(x_ref, mod_ref, nw_ref, w_ref, wdt_ref, o_ref, odt_ref, h_ref, *, tn):
    j = pl.program_id(1)

    @pl.when(j == 0)
    def _():
        _modnorm_rows(x_ref, h_ref, nw_ref[2:3, :], mod_ref)
        odt_ref[...] = _dot(h_ref[...], wdt_ref[...])

    def project(jj):
        o_ref[...] = _dot(h_ref[...], w_ref[:, jj * tn:(jj + 1) * tn])

    for jj in range(IN_MAIN // tn):
        pl.when(j == jj)(functools.partial(project, jj))


def _proj_in(xs, mod, nw, w_in, w_dt, *, layer, tm, mod_row):
    rows, d = xs.shape
    n = IN_MAIN
    tn = n // 2
    nfull = w_in.shape[2]
    return pl.pallas_call(
        functools.partial(_proj_in_kernel, tn=tn),
        grid=(rows // tm, n // tn),
        in_specs=[
            pl.BlockSpec((tm, d), lambda i, j: (i, 0)),
            pl.BlockSpec((None, None, 3, d), lambda i, j: (layer * 3 + 1, mod_row(i), 0, 0)),
            pl.BlockSpec((None, 6, d), lambda i, j: (layer, 0, 0)),
            pl.BlockSpec((None, d, nfull), lambda i, j: (layer, 0, 0), pipeline_mode=pl.Buffered(1)),
            pl.BlockSpec((None, d, LANES), lambda i, j: (layer, 0, 0)),
        ],
        out_specs=[pl.BlockSpec((tm, tn), lambda i, j: (i, j)),
                   pl.BlockSpec((tm, LANES), lambda i, j: (i, 0))],
        out_shape=[jax.ShapeDtypeStruct((rows, n), F32), jax.ShapeDtypeStruct((rows, LANES), F32)],
        scratch_shapes=[pltpu.VMEM((tm, d), BF16)],
        compiler_params=_cparams(("arbitrary", "arbitrary")),
        name="proj_in",
    )(xs, mod, nw, w_in, w_dt)


def _proj_out_kernel(x_ref, ra_ref, at_ref, ss_ref, w_ref, mod_ref, nw_ref, o_ref):
    y = _dot(ra_ref[...], w_ref[0:RET_W, :])
    y = y + _dot(at_ref[...], w_ref[RET_W:RET_W + ATT_W, :])
    y = y + _dot(ss_ref[...], w_ref[RET_W + ATT_W:, :])
    o_ref[...] = x_ref[...] + _rms(y) * (mod_ref[2:3, :] * nw_ref[3:4, :])


def _proj_out(xs, rows_out, ra, at, ss, w_out, mod, nw, *, layer, tm, mod_row):
    d = xs.shape[1]
    return pl.pallas_call(
        _proj_out_kernel,
        grid=(rows_out // tm,),
        in_specs=[
            pl.BlockSpec((tm, d), lambda i: (i, 0)),
            pl.BlockSpec((tm, RET_W), lambda i: (i, 0)),
            pl.BlockSpec((tm, ATT_W), lambda i: (i, 0)),
            pl.BlockSpec((tm, SSD_W), lambda i: (i, 0)),
            pl.BlockSpec((None, d, d), lambda i: (layer, 0, 0)),
            pl.BlockSpec((None, None, 3, d), lambda i: (layer * 3 + 1, mod_row(i), 0, 0)),
            pl.BlockSpec((None, 6, d), lambda i: (layer, 0, 0)),
        ],
        out_specs=pl.BlockSpec((tm, d), lambda i: (i, 0)),
        out_shape=jax.ShapeDtypeStruct((rows_out, d), F32),
        compiler_params=_cparams(("arbitrary",)),
        name="proj_out",
    )(xs, ra, at, ss, w_out, mod, nw)


class _Geom:
    def __init__(self, b, t, lc):
        assert t % RS == 0 and lc % RS == 0
        self.b, self.t, self.lc = b, t, lc
        self.nlat = t // RS
        self.nctx = lc // RS
        self.nst = self.nlat + self.nctx
        self.rows = b * (t + lc)

    def row_block(self, bi, j):
        return jnp.where(j < self.nctx,
                         self.b * self.nlat + bi * self.nctx + j,
                         bi * self.nlat + (j - self.nctx))

    def bwd_step(self, s):
        return jnp.where(s < self.nctx, self.nctx - 1 - s, self.nst + self.nctx - 1 - s)

    def seg_first(self, j):
        return (j == 0) | (j == self.nctx)

    def seg_last(self, j):
        return (j == self.nctx - 1) | (j == self.nst - 1)


def _ret_bstate_kernel(k_ref, v_ref, ld_ref, sb_ref, s_ref):
    @pl.when(pl.program_id(1) == 0)
    def _():
        s_ref[...] = jnp.zeros_like(s_ref)

    jrow = _iota2((CHUNK, CHUNK), 0).astype(F32)
    for h in range(RET_HEADS):
        hs = slice(h * RET_DK, (h + 1) * RET_DK)
        lgb = -jnp.abs(ld_ref[1, h:h + 1, :])
        kdec = jnp.exp(lgb * jrow)
        cdec = jnp.exp(lgb * float(CHUNK))
        for c in reversed(range(CPS)):
            rows = slice(c * CHUNK, (c + 1) * CHUNK)
            sb_ref[c, h] = s_ref[h]
            kd = (k_ref[rows, hs] * kdec).astype(BF16)
            s_ref[h] = s_ref[h] * cdec + _dot_tn(kd, v_ref[rows, hs].astype(BF16))


def _ret_fwd_kernel(q_ref, k_ref, v_ref, g_ref, sb_ref, ld_ref, nw_ref, o_ref, s_ref):
    @pl.when(pl.program_id(1) == 0)
    def _():
        s_ref[...] = jnp.zeros_like(s_ref)

    irow = _iota2((CHUNK, CHUNK), 0).astype(F32)
    rel = irow - _iota2((CHUNK, CHUNK), 1).astype(F32)
    for h in range(RET_HEADS):
        hs = slice(h * RET_DK, (h + 1) * RET_DK)
        lgf = -jnp.abs(ld_ref[0, h:h + 1, :])
        lgb = -jnp.abs(ld_ref[1, h:h + 1, :])
        dmask = (jnp.where(rel >= 0, jnp.exp(lgf * jnp.maximum(rel, 0.0)), 0.0)
                 + jnp.where(rel <= 0, jnp.exp(lgb * jnp.maximum(-rel, 0.0)), 0.0))
        qdec_f = jnp.exp(lgf * (irow + 1.0))
        qdec_b = jnp.exp(lgb * (float(CHUNK) - irow))
        kdec_f = jnp.exp(lgf * (float(CHUNK) - 1.0 - irow))
        cdec_f = jnp.exp(lgf * float(CHUNK))
        for c in range(CPS):
            rows = slice(c * CHUNK, (c + 1) * CHUNK)
            q = q_ref[rows, hs] * (RET_DK ** -0.5)
            k = k_ref[rows, hs]
            vb = v_ref[rows, hs].astype(BF16)
            inner = _dot_nt(q.astype(BF16), k.astype(BF16)) * dmask
            y = _dot(inner.astype(BF16), vb)
            y = y + _dot((q * qdec_f).astype(BF16), s_ref[h].astype(BF16))
            y = y + _dot((q * qdec_b).astype(BF16), sb_ref[c, h].astype(BF16))
            s_ref[h] = s_ref[h] * cdec_f + _dot_tn((k * kdec_f).astype(BF16), vb)
            mu = jnp.mean(y, axis=-1, keepdims=True)
            yc = y - mu
            var = jnp.mean(yc * yc, axis=-1, keepdims=True)
            gate = g_ref[rows, hs]
            yn = yc * lax.rsqrt(var + NORM_EPS) * nw_ref[:, hs]
            o_ref[rows, hs] = (yn * (gate * _sigmoid(gate))).astype(BF16)


def _retention(p, geom, ld, nw):
    b, nst = geom.b, geom.nst
    ld_spec = pl.BlockSpec((2, RET_HEADS, LANES), lambda bi, s: (0, 0, 0))
    state = pltpu.VMEM((RET_HEADS, RET_DK, RET_DK), F32)
    sb_block = (None, None, CPS, RET_HEADS, RET_DK, RET_DK)

    def blk(col, order):
        return pl.BlockSpec((RS, RET_W), lambda bi, s: (geom.row_block(bi, order(s)), col))

    sb = pl.pallas_call(
        _ret_bstate_kernel,
        grid=(b, nst),
        in_specs=[blk(1, geom.bwd_step), blk(2, geom.bwd_step), ld_spec],
        out_specs=pl.BlockSpec(sb_block, lambda bi, s: (bi, geom.bwd_step(s), 0, 0, 0, 0)),
        out_shape=jax.ShapeDtypeStruct((b, nst, CPS, RET_HEADS, RET_DK, RET_DK), F32),
        scratch_shapes=[state],
        compiler_params=_cparams(("arbitrary", "arbitrary")),
        name="ret_bstate",
    )(p, p, ld)

    ident = lambda s: s
    return pl.pallas_call(
        _ret_fwd_kernel,
        grid=(b, nst),
        in_specs=[blk(0, ident), blk(1, ident), blk(2, ident), blk(3, ident),
                  pl.BlockSpec(sb_block, lambda bi, s: (bi, s, 0, 0, 0, 0)),
                  ld_spec,
                  pl.BlockSpec((1, RET_W), lambda bi, s: (0, 0))],
        out_specs=pl.BlockSpec((RS, RET_W), lambda bi, s: (geom.row_block(bi, s), 0)),
        out_shape=jax.ShapeDtypeStruct((geom.rows, RET_W), BF16),
        scratch_shapes=[state],
        compiler_params=_cparams(("arbitrary", "arbitrary")),
        name="ret_fwd",
    )(p, p, p, p, sb, ld, nw)


def _rope(x, cos, sin):
    lane = _iota2(x.shape, 1)
    swapped = jnp.where((lane // 32) % 2 == 0, pltpu.roll(x, 96, axis=1), pltpu.roll(x, 32, axis=1))
    return x * cos + swapped * sin


def _att_kernel(q_ref, kl_ref, vl_ref, kc_ref, vc_ref, cos_ref, sin_ref, sink_ref, o_ref,
                kr_ref, vb_ref, *, t):
    step = pl.program_id(1)
    nb = pl.num_programs(1) * ATT_QB
    blk = CHUNK

    @pl.when(step == 0)
    def _():
        zeros = jnp.zeros((blk, ATT_KV_W), BF16)
        kr_ref[0:blk, :] = zeros
        kr_ref[t + blk:t + 2 * blk, :] = zeros
        vb_ref[0:blk, :] = zeros
        vb_ref[t + blk:t + 2 * blk, :] = zeros
        for h in range(ATT_KV_HEADS):
            hs = slice(h * ATT_HD, (h + 1) * ATT_HD)
            kr_ref[blk:t + blk, hs] = _rope(kl_ref[:, hs], cos_ref[...], sin_ref[...]).astype(BF16)
        vb_ref[blk:t + blk, :] = vl_ref[...].astype(BF16)

    scale = ATT_HD ** -0.5
    qi = _iota2((2 * blk, 3 * blk), 0) % blk
    kj = _iota2((2 * blk, 3 * blk), 1)
    in_window = jnp.abs(kj - blk - qi) <= WINDOW
    first_head = _iota2((2 * blk, 1), 0) < blk

    for qb in range(ATT_QB):
        n = step * ATT_QB + qb
        qs = slice(qb * blk, (qb + 1) * blk)
        row0 = pl.multiple_of(n * blk, blk)
        cos = cos_ref[pl.ds(row0, blk), :]
        sin = sin_ref[pl.ds(row0, blk), :]
        valid = in_window & ((kj >= blk) | (n > 0)) & ((kj < 2 * blk) | (n < nb - 1))
        for h in range(ATT_KV_HEADS):
            hs = slice(h * ATT_HD, (h + 1) * ATT_HD)
            q0 = _rope(q_ref[qs, 2 * h * ATT_HD:(2 * h + 1) * ATT_HD], cos, sin)
            q1 = _rope(q_ref[qs, (2 * h + 1) * ATT_HD:(2 * h + 2) * ATT_HD], cos, sin)
            q2 = jnp.concatenate([q0, q1], axis=0).astype(BF16)
            kw = kr_ref[pl.ds(row0, 3 * blk), hs]
            vw = vb_ref[pl.ds(row0, 3 * blk), hs]
            s_loc = jnp.where(valid, _dot_nt(q2, kw) * scale, NEG_BIG)
            s_cx = _dot_nt(q2, kc_ref[:, hs].astype(BF16)) * scale
            sink = jnp.where(first_head, sink_ref[2 * h], sink_ref[2 * h + 1])
            m = jnp.maximum(jnp.maximum(jnp.max(s_loc, axis=-1, keepdims=True),
                                        jnp.max(s_cx, axis=-1, keepdims=True)), sink)
            e_loc = jnp.exp(s_loc - m)
            e_cx = jnp.exp(s_cx - m)
            den = (jnp.sum(e_loc, axis=-1, keepdims=True) + jnp.sum(e_cx, axis=-1, keepdims=True)
                   + jnp.exp(sink - m))
            o = _dot(e_loc.astype(BF16), vw) + _dot(e_cx.astype(BF16), vc_ref[:, hs].astype(BF16))
            o = o / den
            o_ref[qs, 2 * h * ATT_HD:(2 * h + 1) * ATT_HD] = o[0:blk].astype(BF16)
            o_ref[qs, (2 * h + 1) * ATT_HD:(2 * h + 2) * ATT_HD] = o[blk:2 * blk].astype(BF16)


def _att_ctx_kernel(q_ref, kc_ref, vc_ref, sink_ref, o_ref):
    lc = q_ref.shape[0]
    scale = ATT_HD ** -0.5
    first_head = _iota2((2 * lc, 1), 0) < lc
    for h in range(ATT_KV_HEADS):
        hs = slice(h * ATT_HD, (h + 1) * ATT_HD)
        q2 = jnp.concatenate([q_ref[:, 2 * h * ATT_HD:(2 * h + 1) * ATT_HD],
                              q_ref[:, (2 * h + 1) * ATT_HD:(2 * h + 2) * ATT_HD]], axis=0).astype(BF16)
        s = _dot_nt(q2, kc_ref[:, hs].astype(BF16)) * scale
        sink = jnp.where(first_head, sink_ref[2 * h], sink_ref[2 * h + 1])
        m = jnp.maximum(jnp.max(s, axis=-1, keepdims=True), sink)
        e = jnp.exp(s - m)
        den = jnp.sum(e, axis=-1, keepdims=True) + jnp.exp(sink - m)
        o = _dot(e.astype(BF16), vc_ref[:, hs].astype(BF16)) / den
        o_ref[:, 2 * h * ATT_HD:(2 * h + 1) * ATT_HD] = o[0:lc].astype(BF16)
        o_ref[:, (2 * h + 1) * ATT_HD:(2 * h + 2) * ATT_HD] = o[lc:2 * lc].astype(BF16)


def _attention(p, geom, sink, cos, sin, ctx_out):
    b, t, lc = geom.b, geom.t, geom.lc
    rq = CHUNK * ATT_QB
    nb = t // rq
    ctx0 = (b * t) // lc
    qcol = 2048 // ATT_W
    kcol = (2048 + ATT_W) // ATT_KV_W
    vcol = kcol + 1
    smem = pl.BlockSpec(memory_space=pltpu.SMEM)
    out_l = pl.pallas_call(
        functools.partial(_att_kernel, t=t),
        grid=(b, nb),
        in_specs=[
            pl.BlockSpec((rq, ATT_W), lambda bi, n: (bi * nb + n, qcol)),
            pl.BlockSpec((t, ATT_KV_W), lambda bi, n: (bi, kcol)),
            pl.BlockSpec((t, ATT_KV_W), lambda bi, n: (bi, vcol)),
            pl.BlockSpec((lc, ATT_KV_W), lambda bi, n: (ctx0 + bi, kcol)),
            pl.BlockSpec((lc, ATT_KV_W), lambda bi, n: (ctx0 + bi, vcol)),
            pl.BlockSpec((t, ATT_HD), lambda bi, n: (0, 0)),
            pl.BlockSpec((t, ATT_HD), lambda bi, n: (0, 0)),
            smem,
        ],
        out_specs=pl.BlockSpec((rq, ATT_W), lambda bi, n: (bi * nb + n, 0)),
        out_shape=jax.ShapeDtypeStruct((b * t, ATT_W), BF16),
        scratch_shapes=[pltpu.VMEM((t + 2 * CHUNK, ATT_KV_W), BF16),
                        pltpu.VMEM((t + 2 * CHUNK, ATT_KV_W), BF16)],
        compiler_params=_cparams(("arbitrary", "arbitrary")),
        name="att_lat",
    )(p, p, p, p, p, cos, sin, sink)
    if not ctx_out:
        return out_l, None
    out_c = pl.pallas_call(
        _att_ctx_kernel,
        grid=(b,),
        in_specs=[
            pl.BlockSpec((lc, ATT_W), lambda bi: (ctx0 + bi, qcol)),
            pl.BlockSpec((lc, ATT_KV_W), lambda bi: (ctx0 + bi, kcol)),
            pl.BlockSpec((lc, ATT_KV_W), lambda bi: (ctx0 + bi, vcol)),
            smem,
        ],
        out_specs=pl.BlockSpec((lc, ATT_W), lambda bi: (bi, 0)),
        out_shape=jax.ShapeDtypeStruct((b * lc, ATT_W), BF16),
        compiler_params=_cparams(("arbitrary",)),
        name="att_ctx",
    )(p, p, p, sink)
    return out_l, out_c


N_DH = 2 * SSD_HEADS


def _pack3(v):
    hi = v.astype(BF16).astype(F32)
    r = v - hi
    mid = r.astype(BF16).astype(F32)
    lo = r - mid
    return (hi + pltpu.roll(mid, N_DH, axis=1) + pltpu.roll(lo, 2 * N_DH, axis=1)).astype(BF16)


def _unpack3(r):
    return r + pltpu.roll(r, LANES - N_DH, axis=1) + pltpu.roll(r, LANES - 2 * N_DH, axis=1)


def _conv_silu(u, prev_row, next_row, w, bias):
    n = u.shape[0]
    edge = _iota2((SUBLANES, u.shape[1]), 0)
    up = pltpu.roll(u, 1, axis=0)
    un = pltpu.roll(u, n - 1, axis=0)
    up = jnp.concatenate([jnp.where(edge == 0, prev_row, up[0:SUBLANES]), up[SUBLANES:]], axis=0)
    un = jnp.concatenate([un[0:n - SUBLANES], jnp.where(edge == SUBLANES - 1, next_row, un[n - SUBLANES:])], axis=0)
    y = w[0:1, :] * up + w[1:2, :] * u + w[2:3, :] * un + bias
    return y * _sigmoid(y)


def _softplus(v):
    return jnp.maximum(v, 0.0) + jnp.log1p(jnp.exp(-jnp.abs(v)))


def _ssd_dt_la(dt_raw, alog_ref, dtb_ref):
    lane = _iota2((1, LANES), 1)
    live = lane < N_DH
    a_row = jnp.where(live, -jnp.exp(alog_ref[...]) * LOG2E, 0.0)
    dt = jnp.where(live, _softplus(dt_raw + dtb_ref[...]), 0.0)
    return dt, dt * a_row


def _tri(upper):
    r = _iota2((CHUNK, CHUNK), 0)
    c = _iota2((CHUNK, CHUNK), 1)
    return jnp.where((c >= r) if upper else (c <= r), 1.0, 0.0).astype(BF16)


def _ssd_bstate_kernel(x_ref, xp_ref, xn_ref, bc_ref, bp_ref, bn_ref, dt_ref, cw_ref, cb_ref,
                       alog_ref, dtb_ref, exp_ref, sb_ref, xs_ref, bco_ref, s_ref, *, geom):
    s = pl.program_id(1)
    j = geom.bwd_step(s)

    @pl.when(s == 0)
    def _():
        s_ref[...] = jnp.zeros_like(s_ref)

    first = geom.seg_first(j)
    last = geom.seg_last(j)
    hl = SUBLANES - 1
    xs_all = _conv_silu(x_ref[...],
                        jnp.where(first, 0.0, xp_ref[hl:hl + 1, :]), jnp.where(last, 0.0, xn_ref[0:1, :]),
                        cw_ref[:, 0:SSD_W], cb_ref[:, 0:SSD_W])
    bc_all = _conv_silu(bc_ref[...],
                        jnp.where(first, 0.0, bp_ref[hl:hl + 1, :]), jnp.where(last, 0.0, bn_ref[0:1, :]),
                        cw_ref[:, SSD_W:], cb_ref[:, SSD_W:]).astype(BF16)
    xs_ref[...] = xs_all
    bco_ref[...] = bc_all
    dt_all, la_all = _ssd_dt_la(dt_ref[...], alog_ref, dtb_ref)
    lane = _iota2((CHUNK, LANES), 1)
    ex = exp_ref[:, SSD_W:2 * SSD_W]
    gw = SSD_W // SSD_GROUPS
    tri_u = _tri(True)
    for c in reversed(range(CPS)):
        rows = slice(c * CHUNK, (c + 1) * CHUNK)
        sb_ref[c] = s_ref[...]
        rb = jnp.where(lane < N_DH, _unpack3(_dot(tri_u, _pack3(la_all[rows]))), 0.0)
        rbx = _dot(_pack3(rb), ex)
        dtx = _dot(_pack3(dt_all[rows]), ex)
        xw = (xs_all[rows] * dtx * jnp.exp2(rbx[0:1, :] - rbx)).astype(BF16)
        cdec = jnp.exp2(rbx[0:1, :])
        for g in range(SSD_GROUPS):
            gs = slice(g * gw, (g + 1) * gw)
            bg = bc_all[rows, g * SSD_STATE:(g + 1) * SSD_STATE]
            s_ref[g] = s_ref[g] * cdec[:, gs] + _dot_tn(bg, xw[:, gs])


def _ssd_fwd_kernel(z_ref, xs_ref, bc_ref, dt_ref, sb_ref, alog_ref, dtb_ref, dsk_ref, nw_ref,
                    exp_ref, expl_ref, o_ref, s_ref):
    @pl.when(pl.program_id(1) == 0)
    def _():
        s_ref[...] = jnp.zeros_like(s_ref)

    dt_all, la_all = _ssd_dt_la(dt_ref[...], alog_ref, dtb_ref)
    lane = _iota2((CHUNK, LANES), 1)
    r = _iota2((CHUNK, CHUNK), 0)
    cidx = _iota2((CHUNK, CHUNK), 1)
    lower = r >= cidx
    diag = cidx == r
    lane64 = lane < SSD_HD
    fwd_lane = (lane % N_DH) < SSD_HEADS
    gw = SSD_W // SSD_GROUPS
    hpg = SSD_HEADS // SSD_GROUPS
    tri_l = _tri(False)
    tri_u = _tri(True)
    for c in range(CPS):
        rows = slice(c * CHUNK, (c + 1) * CHUNK)
        xs = xs_ref[rows, :]
        lap = _pack3(la_all[rows])
        cum = jnp.where(lane < N_DH,
                        _unpack3(jnp.where(fwd_lane, _dot(tri_l, lap), _dot(tri_u, lap))), 0.0)
        cum_t = cum.T
        cump = _pack3(cum)
        cumx = _dot(cump, exp_ref[...])
        dt = dt_all[rows]
        dt_t = dt.T
        xdt_f = xs * _dot(_pack3(dt), exp_ref[:, 0:SSD_W])

        cgs = []
        cbs = []
        for g in range(SSD_GROUPS):
            bg = bc_ref[rows, g * SSD_STATE:(g + 1) * SSD_STATE]
            cg = bc_ref[rows, SSD_BC_W + g * SSD_STATE:SSD_BC_W + (g + 1) * SSD_STATE]
            cgs.append((bg, cg))
            cbs.append(_dot_nt(cg, bg))
        col_f = _dot(cump, expl_ref[:, 0:SSD_HEADS * LANES])
        col_b = _dot(cump, expl_ref[:, SSD_HEADS * LANES:])
        y_parts = []
        for hp in range(SSD_HEADS // 2):
            g = (2 * hp) // hpg
            ps = slice(hp * LANES, (hp + 1) * LANES)
            lhs = []
            for e in range(2):
                h = 2 * hp + e
                hb = SSD_HEADS + h
                hs = slice(h * LANES, (h + 1) * LANES)
                dec = jnp.exp2(jnp.where(lower, col_f[:, hs] - cum_t[h:h + 1, :],
                                        col_b[:, hs] - cum_t[hb:hb + 1, :]))
                dtf = dt_t[h:h + 1, :]
                dtb = dt_t[hb:hb + 1, :]
                dts = jnp.where(diag, dtf + dtb, jnp.where(lower, dtf, dtb))
                lhs.append((cbs[g] * dec * dts).astype(BF16))
            xp = xs[:, ps]
            rhs = jnp.concatenate([jnp.where(lane64, xp, 0.0).astype(BF16),
                                   jnp.where(lane64, 0.0, xp).astype(BF16)], axis=0)
            y_parts.append(_dot(jnp.concatenate(lhs, axis=1), rhs))
        y = jnp.concatenate(y_parts, axis=1)

        e_f = jnp.exp2(cumx[:, 0:SSD_W])
        e_b = jnp.exp2(cumx[:, SSD_W:])
        yoff_f = jnp.concatenate([_dot(cgs[g][1], s_ref[g].astype(BF16)) for g in range(SSD_GROUPS)], axis=1)
        yoff_b = jnp.concatenate([_dot(cgs[g][1], sb_ref[c, g].astype(BF16)) for g in range(SSD_GROUPS)], axis=1)
        y = y + yoff_f * e_f + yoff_b * e_b

        tot = cumx[CHUNK - 1:CHUNK, 0:SSD_W]
        xw = (xdt_f * jnp.exp2(tot - cumx[:, 0:SSD_W])).astype(BF16)
        cdec = jnp.exp2(tot)
        for g in range(SSD_GROUPS):
            gs = slice(g * gw, (g + 1) * gw)
            s_ref[g] = s_ref[g] * cdec[:, gs] + _dot_tn(cgs[g][0], xw[:, gs])

        z = z_ref[rows, :]
        y = (y + dsk_ref[...] * xs) * (z * _sigmoid(z))
        o_ref[rows, :] = (_rms(y) * nw_ref[...]).astype(BF16)


def _expand_consts():
    kk = np.arange(N_DH)
    ex = np.zeros((LANES, 2 * SSD_W), np.float32)
    exl = np.zeros((LANES, N_DH * LANES), np.float32)
    for part in range(3):
        for p in range(SSD_HD):
            ex[part * N_DH + kk, kk * SSD_HD + p] = 1.0
        for q in range(LANES):
            exl[part * N_DH + kk, kk * LANES + q] = 1.0
    return jnp.asarray(ex, BF16), jnp.asarray(exl, BF16)


def _ssd(p, pdt, geom, conv_w, conv_b, alog, dtb, dsk, nw):
    b, nst = geom.b, geom.nst
    ex, exl = _expand_consts()
    zcol = 3072 // SSD_W
    xcol = 4096 // SSD_W
    bccol = (4096 + SSD_W) // (2 * SSD_BC_W)
    n8 = geom.rows // SUBLANES
    sub = RS // SUBLANES
    gw = SSD_W // SSD_GROUPS
    sb_block = (None, None, CPS, SSD_GROUPS, SSD_STATE, gw)

    def const(shape):
        return pl.BlockSpec(shape, lambda bi, s: (0,) * len(shape))

    state = pltpu.VMEM((SSD_GROUPS, SSD_STATE, gw), F32)
    rb = lambda bi, s: geom.row_block(bi, geom.bwd_step(s))
    prev8 = lambda bi, s: jnp.maximum(rb(bi, s) * sub - 1, 0)
    next8 = lambda bi, s: jnp.minimum(rb(bi, s) * sub + sub, n8 - 1)
    sb, xs, bcm = pl.pallas_call(
        functools.partial(_ssd_bstate_kernel, geom=geom),
        grid=(b, nst),
        in_specs=[pl.BlockSpec((RS, SSD_W), lambda bi, s: (rb(bi, s), xcol)),
                  pl.BlockSpec((SUBLANES, SSD_W), lambda bi, s: (prev8(bi, s), xcol)),
                  pl.BlockSpec((SUBLANES, SSD_W), lambda bi, s: (next8(bi, s), xcol)),
                  pl.BlockSpec((RS, 2 * SSD_BC_W), lambda bi, s: (rb(bi, s), bccol)),
                  pl.BlockSpec((SUBLANES, 2 * SSD_BC_W), lambda bi, s: (prev8(bi, s), bccol)),
                  pl.BlockSpec((SUBLANES, 2 * SSD_BC_W), lambda bi, s: (next8(bi, s), bccol)),
                  pl.BlockSpec((RS, LANES), lambda bi, s: (rb(bi, s), 0)),
                  const(conv_w.shape), const(conv_b.shape), const(alog.shape), const(dtb.shape),
                  const(ex.shape)],
        out_specs=[pl.BlockSpec(sb_block, lambda bi, s: (bi, geom.bwd_step(s), 0, 0, 0, 0)),
                   pl.BlockSpec((RS, SSD_W), lambda bi, s: (rb(bi, s), 0)),
                   pl.BlockSpec((RS, 2 * SSD_BC_W), lambda bi, s: (rb(bi, s), 0))],
        out_shape=[jax.ShapeDtypeStruct((b, nst, CPS, SSD_GROUPS, SSD_STATE, gw), F32),
                   jax.ShapeDtypeStruct((geom.rows, SSD_W), F32),
                   jax.ShapeDtypeStruct((geom.rows, 2 * SSD_BC_W), BF16)],
        scratch_shapes=[state],
        compiler_params=_cparams(("arbitrary", "arbitrary")),
        name="ssd_bstate",
    )(p, p, p, p, p, p, pdt, conv_w, conv_b, alog, dtb, ex)

    rf = lambda bi, s: geom.row_block(bi, s)
    return pl.pallas_call(
        _ssd_fwd_kernel,
        grid=(b, nst),
        in_specs=[pl.BlockSpec((RS, SSD_W), lambda bi, s: (rf(bi, s), zcol)),
                  pl.BlockSpec((RS, SSD_W), lambda bi, s: (rf(bi, s), 0)),
                  pl.BlockSpec((RS, 2 * SSD_BC_W), lambda bi, s: (rf(bi, s), 0)),
                  pl.BlockSpec((RS, LANES), lambda bi, s: (rf(bi, s), 0)),
                  pl.BlockSpec(sb_block, lambda bi, s: (bi, s, 0, 0, 0, 0)),
                  const(alog.shape), const(dtb.shape), const(dsk.shape), const(nw.shape),
                  const(ex.shape), const(exl.shape)],
        out_specs=pl.BlockSpec((RS, SSD_W), lambda bi, s: (rf(bi, s), 0)),
        out_shape=jax.ShapeDtypeStruct((geom.rows, SSD_W), BF16),
        scratch_shapes=[state],
        compiler_params=_cparams(("arbitrary", "arbitrary")),
        name="ssd_fwd",
    )(p, xs, bcm, pdt, sb, alog, dtb, dsk, nw, ex, exl)


def _rope_tables(t):
    half = ATT_HD // 2
    freqs = ROPE_BASE ** (-jnp.arange(0, half, 2, dtype=F32) / half)
    pos = jnp.arange(t)
    ang_r = (pos // GRID_W).astype(F32)[:, None] * freqs[None, :]
    ang_c = (pos % GRID_W).astype(F32)[:, None] * freqs[None, :]
    cos = jnp.concatenate([jnp.cos(ang_r), jnp.cos(ang_r), jnp.cos(ang_c), jnp.cos(ang_c)], axis=-1)
    sin = jnp.concatenate([-jnp.sin(ang_r), jnp.sin(ang_r), -jnp.sin(ang_c), jnp.sin(ang_c)], axis=-1)
    return cos, sin


def _pad_lanes(v):
    v = v.reshape(1, -1)
    return jnp.pad(v, ((0, 0), (0, LANES - v.shape[1])))


def _mixer(p, pdt, geom, layer, ctx_out, cos, sin, ret_log_decay, ret_norm_w, attn_sink, ssd_conv_w,
           ssd_conv_b, ssd_a_log, ssd_dt_bias, ssd_d, ssd_norm_w):
    ld = jnp.broadcast_to(ret_log_decay[layer][:, :, None], (2, RET_HEADS, LANES))
    ra = _retention(p, geom, ld, ret_norm_w[layer].reshape(1, RET_W))
    at_l, at_c = _attention(p, geom, attn_sink[layer], cos, sin, ctx_out)
    ss = _ssd(p, pdt, geom, ssd_conv_w[layer], ssd_conv_b[layer].reshape(1, -1),
              _pad_lanes(ssd_a_log[layer]), _pad_lanes(ssd_dt_bias[layer]),
              jnp.repeat(ssd_d[layer], SSD_HD).reshape(1, SSD_W),
              ssd_norm_w[layer].reshape(1, SSD_W))
    return ra, at_l, at_c, ss


def _forward(x, c, ctx, c_ctx, w_ada, b_ada, norm_w, ffn1_gu, ffn1_down, ffn2_gu, ffn2_down,
             w_in, w_out, ret_log_decay, ret_norm_w, attn_sink, ssd_conv_w, ssd_conv_b,
             ssd_a_log, ssd_dt_bias, ssd_d, ssd_norm_w):
    b, t, d = x.shape
    lc = ctx.shape[1]
    depth = w_ada.shape[0]
    geom = _Geom(b, t, lc)
    n_lat = b * t
    tm = math.gcd(512, math.gcd(t, b * lc))

    def mod_row(i):
        r0 = i * tm
        return jnp.where(r0 < n_lat, 1 + r0 // t, 0)

    cond = jnp.concatenate([c_ctx[None, :], c, jnp.zeros((SUBLANES - 1 - b, d), F32)], axis=0)
    mod = _adaln(cond, w_ada, b_ada)
    mod = mod.reshape(SUBLANES, depth, 3, 3, d).transpose(1, 2, 0, 3, 4).reshape(depth * 3, SUBLANES, 3, d)

    w1gu = ffn1_gu.astype(BF16)
    w1d = ffn1_down.astype(BF16)
    w2gu = ffn2_gu.astype(BF16)
    w2d = ffn2_down.astype(BF16)
    wi = w_in.astype(BF16)
    wi_dt = jnp.pad(wi[:, :, IN_MAIN:], ((0, 0), (0, 0), (0, LANES - N_DH)))
    wo = w_out.astype(BF16)

    cos, sin = _rope_tables(t)
    srcs = [x.reshape(n_lat, d), ctx.reshape(b * lc, d)]
    rows_all = n_lat + b * lc
    for layer in range(depth):
        last = layer == depth - 1
        xs = _ffn(srcs, rows_all, mod, norm_w, w1gu, w1d, layer=layer, sub=0, tm=tm, mod_row=mod_row)
        p, pdt = _proj_in(xs, mod, norm_w, wi, wi_dt, layer=layer, tm=tm, mod_row=mod_row)
        ra, at_l, at_c, ss = _mixer(p, pdt, geom, layer, not last, cos, sin, ret_log_decay, ret_norm_w,
                                    attn_sink, ssd_conv_w, ssd_conv_b, ssd_a_log, ssd_dt_bias, ssd_d,
                                    ssd_norm_w)
        at = at_l if last else jnp.concatenate([at_l, at_c], axis=0)
        rows_out = n_lat if last else rows_all
        xs = _proj_out(xs, rows_out, ra, at, ss, wo, mod, norm_w, layer=layer, tm=tm, mod_row=mod_row)
        xs = _ffn([xs], rows_out, mod, norm_w, w2gu, w2d, layer=layer, sub=2, tm=tm, mod_row=mod_row)
        srcs = [xs]
    return xs.reshape(b, t, d)


def kernel(x, c, ctx, c_ctx, w_ada, b_ada, norm_w, ffn1_gu, ffn1_down, ffn2_gu, ffn2_down,
           w_in, w_out, ret_log_decay, ret_norm_w, attn_sink, ssd_conv_w, ssd_conv_b,
           ssd_a_log, ssd_dt_bias, ssd_d, ssd_norm_w):
    return _forward(x, c, ctx, c_ctx, w_ada, b_ada, norm_w, ffn1_gu, ffn1_down, ffn2_gu, ffn2_down,
                    w_in, w_out, ret_log_decay, ret_norm_w, attn_sink, ssd_conv_w, ssd_conv_b,
                    ssd_a_log, ssd_dt_bias, ssd_d, ssd_norm_w)
```

```python
import functools
import math

import jax
import jax.numpy as jnp
import numpy as np
from jax import lax
from jax.experimental import pallas as pl
from jax.experimental.pallas import tpu as pltpu

F32 = jnp.float32
BF16 = jnp.bfloat16

D_MODEL = 2048
GRID_W = 64
RET_HEADS = 4
RET_DK = 128
RET_W = 512
ATT_HEADS = 4
ATT_KV_HEADS = 2
ATT_HD = 128
ATT_W = 512
ATT_KV_W = 256
WINDOW = 128
SSD_HEADS = 16
SSD_HD = 64
SSD_W = 1024
SSD_GROUPS = 2
SSD_STATE = 128
SSD_BC_W = 256
CHUNK = 128
D_FF = 5632
FFN_RES = 0.5
FFN_TF = 512
ROPE_BASE = 10000.0
NORM_EPS = 1e-6
N_MOD = 9
IN_MAIN = 5632
NEG_BIG = -1e30
LOG2E = 1.4426950408889634

CPS = 2
ATT_QB = 2
RS = CHUNK * CPS
LANES = 128
SUBLANES = 8

VMEM_LIMIT = 56 * 1024 * 1024


def _cparams(sem):
    return pltpu.CompilerParams(dimension_semantics=sem, vmem_limit_bytes=VMEM_LIMIT)


def _sigmoid(v):
    return 1.0 / (1.0 + jnp.exp(-v))


def _rms(v):
    return v * lax.rsqrt(jnp.mean(v * v, axis=-1, keepdims=True) + NORM_EPS)


def _dot(a, b):
    return jnp.dot(a, b, preferred_element_type=F32)


def _dot_nt(a, b):
    return lax.dot_general(a, b, (((1,), (1,)), ((), ())), preferred_element_type=F32)


def _dot_tn(a, b):
    return lax.dot_general(a, b, (((0,), (0,)), ((), ())), preferred_element_type=F32)


def _iota2(shape, dim):
    return lax.broadcasted_iota(jnp.int32, shape, dim)


SLAB = 16
SLAB_UNROLL = 4


def _row_slabs(n_rows, body, straight_line=False):
    if straight_line:
        for r in range(n_rows // SLAB):
            body(pl.ds(r * SLAB, SLAB))
        return

    def step(r, carry):
        body(pl.ds(pl.multiple_of(r * SLAB, SLAB), SLAB))
        return carry

    lax.fori_loop(0, n_rows // SLAB, step, 0, unroll=SLAB_UNROLL)


def _modnorm_rows(x_ref, h_ref, nw_row, mod_ref, zero_ref=None, straight_line=False):
    gain = nw_row * (1.0 + mod_ref[1:2, :])
    shift = mod_ref[0:1, :]

    def body(rows):
        h_ref[rows, :] = (_rms(x_ref[rows, :]) * gain + shift).astype(BF16)
        if zero_ref is not None:
            zero_ref[rows, :] = jnp.zeros((SLAB, zero_ref.shape[1]), zero_ref.dtype)

    _row_slabs(x_ref.shape[0], body, straight_line)


def _residual_rows(x_ref, y_ref, gain, o_ref, straight_line=False):
    def body(rows):
        o_ref[rows, :] = x_ref[rows, :] + _rms(y_ref[rows, :]) * gain

    _row_slabs(x_ref.shape[0], body, straight_line)


def _adaln_kernel(c_ref, w_ref, b_ref, o_ref):
    cnd = c_ref[...]
    s = (cnd * _sigmoid(cnd)).astype(BF16)
    o_ref[...] = _dot(s, w_ref[...].astype(BF16)) + b_ref[...]


def _adaln(cond, w_ada, b_ada):
    depth, d, n = w_ada.shape
    tn = 1024
    nt = n // tn
    return pl.pallas_call(
        _adaln_kernel,
        grid=(depth, nt),
        in_specs=[
            pl.BlockSpec((SUBLANES, d), lambda l, j: (0, 0)),
            pl.BlockSpec((None, d, tn), lambda l, j: (l, 0, j)),
            pl.BlockSpec((None, 1, tn), lambda l, j: (l, 0, j)),
        ],
        out_specs=pl.BlockSpec((SUBLANES, tn), lambda l, j: (0, l * nt + j)),
        out_shape=jax.ShapeDtypeStruct((SUBLANES, depth * n), F32),
        compiler_params=_cparams(("arbitrary", "arbitrary")),
        name="adaln",
    )(cond, w_ada, b_ada.reshape(depth, 1, n))


def _ffn_kernel(*refs, sub, split):
    j = pl.program_id(1)
    is_first = j == 0
    is_last = j == pl.num_programs(1) - 1
    if split is None:
        x_ref, mod_ref, nw_ref, wgu_ref, wd_ref, o_ref, h_ref, acc_ref = refs
        sources = [(x_ref, None)]
    else:
        xa_ref, xb_ref, mod_ref, nw_ref, wgu_ref, wd_ref, o_ref, h_ref, acc_ref = refs
        from_a = pl.program_id(0) < split
        sources = [(xa_ref, from_a), (xb_ref, jnp.logical_not(from_a))]

    def prologue(x_ref):
        _modnorm_rows(x_ref, h_ref, nw_ref[2 * sub:2 * sub + 1, :], mod_ref, zero_ref=acc_ref)

    def epilogue(x_ref):
        gain = (FFN_RES * mod_ref[2:3, :]) * nw_ref[2 * sub + 1:2 * sub + 2, :]
        _residual_rows(x_ref, acc_ref, gain, o_ref)

    for x_ref, active in sources:
        pl.when(is_first if active is None else is_first & active)(functools.partial(prologue, x_ref))

    h = h_ref[...]
    g = _dot(h, wgu_ref[:, 0:FFN_TF])
    u = _dot(h, wgu_ref[:, FFN_TF:])
    a = (g * _sigmoid(g) * u).astype(BF16)
    acc_ref[...] += _dot(a, wd_ref[...])

    for x_ref, active in sources:
        pl.when(is_last if active is None else is_last & active)(functools.partial(epilogue, x_ref))


def _ffn_lag_kernel(xp_ref, xn_ref, modp_ref, modn_ref, nw_ref, wgu_ref, wd_ref, o_ref,
                    h_ref, acc_ref, a0_ref, a1_ref, *, sub, nf):
    i = pl.program_id(0)
    jj = pl.program_id(1)
    last_tile = pl.num_programs(0) - 1
    slot = i % 2
    nw_in = nw_ref[2 * sub:2 * sub + 1, :]
    nw_out = nw_ref[2 * sub + 1:2 * sub + 2, :]
    a_of_chunk = lambda k: a0_ref if k % 2 == 0 else a1_ref

    def gate_up(a_ref):
        h = h_ref[slot]
        g = _dot(h, wgu_ref[:, 0:FFN_TF])
        u = _dot(h, wgu_ref[:, FFN_TF:])
        a_ref[...] = (g * _sigmoid(g) * u).astype(BF16)

    def finish(x_ref, mod_ref, acc_slot, straight_line):
        gain = (FFN_RES * mod_ref[2:3, :]) * nw_out
        _residual_rows(x_ref, acc_ref.at[acc_slot], gain, o_ref, straight_line)

    @pl.when((jj == 0) & (i == 0))
    def _():
        _modnorm_rows(xn_ref, h_ref.at[0], nw_in, modp_ref)
        gate_up(a_of_chunk(0))

    @pl.when((jj == 0) & (i > 0))
    def _():
        gate_up(a_of_chunk(0))
        finish(xp_ref, modp_ref, 1 - slot, True)

    @pl.when(jj == 1)
    def _():
        gate_up(a_of_chunk(1))
        acc_ref[slot] = _dot(a_of_chunk(0)[...], wd_ref[...])

    odd = (jj % 2) == 1
    mid = (jj > 1) & (jj < nf)

    @pl.when(mid & odd)
    def _():
        gate_up(a_of_chunk(1))
        acc_ref[slot] += _dot(a_of_chunk(0)[...], wd_ref[...])

    @pl.when(mid & jnp.logical_not(odd))
    def _():
        gate_up(a_of_chunk(0))
        acc_ref[slot] += _dot(a_of_chunk(1)[...], wd_ref[...])

    @pl.when(jj == nf)
    def _():
        acc_ref[slot] += _dot(a_of_chunk(nf - 1)[...], wd_ref[...])
        _modnorm_rows(xn_ref, h_ref.at[1 - slot], nw_in, modn_ref, straight_line=True)

    @pl.when((jj == nf) & (i == last_tile))
    def _():
        finish(xn_ref, modn_ref, slot, False)


def _ffn_lag(xs, rows_out, mod, nw, w_gu, w_down, *, layer, sub, tm, mod_row):
    d = xs.shape[1]
    tf = FFN_TF
    nf = D_FF // tf
    nt = rows_out // tm
    prev_tile = lambda i, j: jnp.maximum(i - (j == 0).astype(jnp.int32), 0)
    next_tile = lambda i, j: jnp.minimum(i + (j >= 1).astype(jnp.int32), nt - 1)
    out_tile = lambda i, j: jnp.where((i == nt - 1) & (j == nf), nt - 1, jnp.maximum(i - 1, 0))
    gu = lambda j: jnp.minimum(j, nf - 1)
    msub = layer * 3 + sub
    return pl.pallas_call(
        functools.partial(_ffn_lag_kernel, sub=sub, nf=nf),
        grid=(nt, nf + 1),
        in_specs=[
            pl.BlockSpec((tm, d), lambda i, j: (prev_tile(i, j), 0)),
            pl.BlockSpec((tm, d), lambda i, j: (next_tile(i, j), 0)),
            pl.BlockSpec((None, None, 3, d), lambda i, j: (msub, mod_row(jnp.maximum(i - 1, 0)), 0, 0)),
            pl.BlockSpec((None, None, 3, d), lambda i, j: (msub, mod_row(jnp.minimum(i + 1, nt - 1)), 0, 0)),
            pl.BlockSpec((None, 6, d), lambda i, j: (layer, 0, 0)),
            pl.BlockSpec((None, None, d, 2 * tf), lambda i, j: (layer, gu(j), 0, 0)),
            pl.BlockSpec((None, tf, d), lambda i, j: (layer, jnp.maximum(j - 1, 0), 0)),
        ],
        out_specs=pl.BlockSpec((tm, d), lambda i, j: (out_tile(i, j), 0)),
        out_shape=jax.ShapeDtypeStruct((rows_out, d), F32),
        scratch_shapes=[pltpu.VMEM((2, tm, d), BF16), pltpu.VMEM((2, tm, d), F32),
                        pltpu.VMEM((tm, tf), BF16), pltpu.VMEM((tm, tf), BF16)],
        compiler_params=_cparams(("arbitrary", "arbitrary")),
        name=f"ffn{sub}",
    )(xs, xs, mod, mod, nw, w_gu, w_down)


def _ffn(srcs, rows_out, mod, nw, w_gu, w_down, *, layer, sub, tm, mod_row):
    if len(srcs) == 1:
        return _ffn_lag(srcs[0], rows_out, mod, nw, w_gu, w_down, layer=layer, sub=sub, tm=tm, mod_row=mod_row)
    d = srcs[0].shape[1]
    tf = FFN_TF
    nf = D_FF // tf
    if len(srcs) == 1:
        split = None
        x_specs = [pl.BlockSpec((tm, d), lambda i, j: (i, 0))]
    else:
        split = srcs[0].shape[0] // tm
        x_specs = [pl.BlockSpec((tm, d), lambda i, j: (jnp.minimum(i, split - 1), 0)),
                   pl.BlockSpec((tm, d), lambda i, j: (jnp.maximum(i - split, 0), 0))]
    return pl.pallas_call(
        functools.partial(_ffn_kernel, sub=sub, split=split),
        grid=(rows_out // tm, nf),
        in_specs=x_specs + [
            pl.BlockSpec((None, None, 3, d), lambda i, j: (layer * 3 + sub, mod_row(i), 0, 0)),
            pl.BlockSpec((None, 6, d), lambda i, j: (layer, 0, 0)),
            pl.BlockSpec((None, None, d, 2 * tf), lambda i, j: (layer, j, 0, 0)),
            pl.BlockSpec((None, tf, d), lambda i, j: (layer, j, 0)),
        ],
        out_specs=pl.BlockSpec((tm, d), lambda i, j: (i, 0)),
        out_shape=jax.ShapeDtypeStruct((rows_out, d), F32),
        scratch_shapes=[pltpu.VMEM((tm, d), BF16), pltpu.VMEM((tm, d), F32)],
        compiler_params=_cparams(("arbitrary", "arbitrary")),
        name=f"ffn{sub}",
    )(*srcs, mod, nw, w_gu, w_down)


def _proj_in_kernel(x_ref, mod_ref, nw_ref, w_ref, wdt_ref, o_ref, odt_ref, h_ref, *, tn):
    j = pl.program_id(1)

    @pl.when(j == 0)
    def _():
        _modnorm_rows(x_ref, h_ref, nw_ref[2:3, :], mod_ref)
        odt_ref[...] = _dot(h_ref[...], wdt_ref[...])

    def project(jj):
        o_ref[...] = _dot(h_ref[...], w_ref[:, jj * tn:(jj + 1) * tn])

    for jj in range(IN_MAIN // tn):
        pl.when(j == jj)(functools.partial(project, jj))


def _proj_in(xs, mod, nw, w_in, w_dt, *, layer, tm, mod_row):
    rows, d = xs.shape
    n = IN_MAIN
    tn = n // 2
    nfull = w_in.shape[2]
    return pl.pallas_call(
        functools.partial(_proj_in_kernel, tn=tn),
        grid=(rows // tm, n // tn),
        in_specs=[
            pl.BlockSpec((tm, d), lambda i, j: (i, 0)),
            pl.BlockSpec((None, None, 3, d), lambda i, j: (layer * 3 + 1, mod_row(i), 0, 0)),
            pl.BlockSpec((None, 6, d), lambda i, j: (layer, 0, 0)),
            pl.BlockSpec((None, d, nfull), lambda i, j: (layer, 0, 0), pipeline_mode=pl.Buffered(1)),
            pl.BlockSpec((None, d, LANES), lambda i, j: (layer, 0, 0)),
        ],
        out_specs=[pl.BlockSpec((tm, tn), lambda i, j: (i, j)),
                   pl.BlockSpec((tm, LANES), lambda i, j: (i, 0))],
        out_shape=[jax.ShapeDtypeStruct((rows, n), F32), jax.ShapeDtypeStruct((rows, LANES), F32)],
        scratch_shapes=[pltpu.VMEM((tm, d), BF16)],
        compiler_params=_cparams(("arbitrary", "arbitrary")),
        name="proj_in",
    )(xs, mod, nw, w_in, w_dt)


def _proj_out_kernel(x_ref, ra_ref, at_ref, ss_ref, w_ref, mod_ref, nw_ref, o_ref):
    y = _dot(ra_ref[...], w_ref[0:RET_W, :])
    y = y + _dot(at_ref[...], w_ref[RET_W:RET_W + ATT_W, :])
    y = y + _dot(ss_ref[...], w_ref[RET_W + ATT_W:, :])
    o_ref[...] = x_ref[...] + _rms(y) * (mod_ref[2:3, :] * nw_ref[3:4, :])


def _proj_out(xs, rows_out, ra, at, ss, w_out, mod, nw, *, layer, tm, mod_row):
    d = xs.shape[1]
    return pl.pallas_call(
        _proj_out_kernel,
        grid=(rows_out // tm,),
        in_specs=[
            pl.BlockSpec((tm, d), lambda i: (i, 0)),
            pl.BlockSpec((tm, RET_W), lambda i: (i, 0)),
            pl.BlockSpec((tm, ATT_W), lambda i: (i, 0)),
            pl.BlockSpec((tm, SSD_W), lambda i: (i, 0)),
            pl.BlockSpec((None, d, d), lambda i: (layer, 0, 0)),
            pl.BlockSpec((None, None, 3, d), lambda i: (layer * 3 + 1, mod_row(i), 0, 0)),
            pl.BlockSpec((None, 6, d), lambda i: (layer, 0, 0)),
        ],
        out_specs=pl.BlockSpec((tm, d), lambda i: (i, 0)),
        out_shape=jax.ShapeDtypeStruct((rows_out, d), F32),
        compiler_params=_cparams(("arbitrary",)),
        name="proj_out",
    )(xs, ra, at, ss, w_out, mod, nw)


class _Geom:
    def __init__(self, b, t, lc):
        assert t % RS == 0 and lc % RS == 0
        self.b, self.t, self.lc = b, t, lc
        self.nlat = t // RS
        self.nctx = lc // RS
        self.nst = self.nlat + self.nctx
        self.rows = b * (t + lc)

    def row_block(self, bi, j):
        return jnp.where(j < self.nctx,
                         self.b * self.nlat + bi * self.nctx + j,
                         bi * self.nlat + (j - self.nctx))

    def bwd_step(self, s):
        return jnp.where(s < self.nctx, self.nctx - 1 - s, self.nst + self.nctx - 1 - s)

    def seg_first(self, j):
        return (j == 0) | (j == self.nctx)

    def seg_last(self, j):
        return (j == self.nctx - 1) | (j == self.nst - 1)


def _ret_bstate_kernel(k_ref, v_ref, ld_ref, sb_ref, s_ref):
    @pl.when(pl.program_id(1) == 0)
    def _():
        s_ref[...] = jnp.zeros_like(s_ref)

    jrow = _iota2((CHUNK, CHUNK), 0).astype(F32)
    for h in range(RET_HEADS):
        hs = slice(h * RET_DK, (h + 1) * RET_DK)
        lgb = -jnp.abs(ld_ref[1, h:h + 1, :])
        kdec = jnp.exp(lgb * jrow)
        cdec = jnp.exp(lgb * float(CHUNK))
        for c in reversed(range(CPS)):
            rows = slice(c * CHUNK, (c + 1) * CHUNK)
            sb_ref[c, h] = s_ref[h]
            kd = (k_ref[rows, hs] * kdec).astype(BF16)
            s_ref[h] = s_ref[h] * cdec + _dot_tn(kd, v_ref[rows, hs].astype(BF16))


def _ret_fwd_kernel(q_ref, k_ref, v_ref, g_ref, sb_ref, ld_ref, nw_ref, o_ref, s_ref):
    @pl.when(pl.program_id(1) == 0)
    def _():
        s_ref[...] = jnp.zeros_like(s_ref)

    irow = _iota2((CHUNK, CHUNK), 0).astype(F32)
    rel = irow - _iota2((CHUNK, CHUNK), 1).astype(F32)
    for h in range(RET_HEADS):
        hs = slice(h * RET_DK, (h + 1) * RET_DK)
        lgf = -jnp.abs(ld_ref[0, h:h + 1, :])
        lgb = -jnp.abs(ld_ref[1, h:h + 1, :])
        dmask = (jnp.where(rel >= 0, jnp.exp(lgf * jnp.maximum(rel, 0.0)), 0.0)
                 + jnp.where(rel <= 0, jnp.exp(lgb * jnp.maximum(-rel, 0.0)), 0.0))
        qdec_f = jnp.exp(lgf * (irow + 1.0))
        qdec_b = jnp.exp(lgb * (float(CHUNK) - irow))
        kdec_f = jnp.exp(lgf * (float(CHUNK) - 1.0 - irow))
        cdec_f = jnp.exp(lgf * float(CHUNK))
        for c in range(CPS):
            rows = slice(c * CHUNK, (c + 1) * CHUNK)
            q = q_ref[rows, hs] * (RET_DK ** -0.5)
            k = k_ref[rows, hs]
            vb = v_ref[rows, hs].astype(BF16)
            inner = _dot_nt(q.astype(BF16), k.astype(BF16)) * dmask
            y = _dot(inner.astype(BF16), vb)
            y = y + _dot((q * qdec_f).astype(BF16), s_ref[h].astype(BF16))
            y = y + _dot((q * qdec_b).astype(BF16), sb_ref[c, h].astype(BF16))
            s_ref[h] = s_ref[h] * cdec_f + _dot_tn((k * kdec_f).astype(BF16), vb)
            mu = jnp.mean(y, axis=-1, keepdims=True)
            yc = y - mu
            var = jnp.mean(yc * yc, axis=-1, keepdims=True)
            gate = g_ref[rows, hs]
            yn = yc * lax.rsqrt(var + NORM_EPS) * nw_ref[:, hs]
            o_ref[rows, hs] = (yn * (gate * _sigmoid(gate))).astype(BF16)


def _retention(p, geom, ld, nw):
    b, nst = geom.b, geom.nst
    ld_spec = pl.BlockSpec((2, RET_HEADS, LANES), lambda bi, s: (0, 0, 0))
    state = pltpu.VMEM((RET_HEADS, RET_DK, RET_DK), F32)
    sb_block = (None, None, CPS, RET_HEADS, RET_DK, RET_DK)

    def blk(col, order):
        return pl.BlockSpec((RS, RET_W), lambda bi, s: (geom.row_block(bi, order(s)), col))

    sb = pl.pallas_call(
        _ret_bstate_kernel,
        grid=(b, nst),
        in_specs=[blk(1, geom.bwd_step), blk(2, geom.bwd_step), ld_spec],
        out_specs=pl.BlockSpec(sb_block, lambda bi, s: (bi, geom.bwd_step(s), 0, 0, 0, 0)),
        out_shape=jax.ShapeDtypeStruct((b, nst, CPS, RET_HEADS, RET_DK, RET_DK), F32),
        scratch_shapes=[state],
        compiler_params=_cparams(("arbitrary", "arbitrary")),
        name="ret_bstate",
    )(p, p, ld)

    ident = lambda s: s
    return pl.pallas_call(
        _ret_fwd_kernel,
        grid=(b, nst),
        in_specs=[blk(0, ident), blk(1, ident), blk(2, ident), blk(3, ident),
                  pl.BlockSpec(sb_block, lambda bi, s: (bi, s, 0, 0, 0, 0)),
                  ld_spec,
                  pl.BlockSpec((1, RET_W), lambda bi, s: (0, 0))],
        out_specs=pl.BlockSpec((RS, RET_W), lambda bi, s: (geom.row_block(bi, s), 0)),
        out_shape=jax.ShapeDtypeStruct((geom.rows, RET_W), BF16),
        scratch_shapes=[state],
        compiler_params=_cparams(("arbitrary", "arbitrary")),
        name="ret_fwd",
    )(p, p, p, p, sb, ld, nw)


def _rope(x, cos, sin):
    lane = _iota2(x.shape, 1)
    swapped = jnp.where((lane // 32) % 2 == 0, pltpu.roll(x, 96, axis=1), pltpu.roll(x, 32, axis=1))
    return x * cos + swapped * sin


def _att_kernel(q_ref, kl_ref, vl_ref, kc_ref, vc_ref, cos_ref, sin_ref, sink_ref, o_ref,
                kr_ref, vb_ref, *, t):
    step = pl.program_id(1)
    nb = pl.num_programs(1) * ATT_QB
    blk = CHUNK

    @pl.when(step == 0)
    def _():
        zeros = jnp.zeros((blk, ATT_KV_W), BF16)
        kr_ref[0:blk, :] = zeros
        kr_ref[t + blk:t + 2 * blk, :] = zeros
        vb_ref[0:blk, :] = zeros
        vb_ref[t + blk:t + 2 * blk, :] = zeros
        for h in range(ATT_KV_HEADS):
            hs = slice(h * ATT_HD, (h + 1) * ATT_HD)
            kr_ref[blk:t + blk, hs] = _rope(kl_ref[:, hs], cos_ref[...], sin_ref[...]).astype(BF16)
        vb_ref[blk:t + blk, :] = vl_ref[...].astype(BF16)

    scale = ATT_HD ** -0.5
    qi = _iota2((2 * blk, 3 * blk), 0) % blk
    kj = _iota2((2 * blk, 3 * blk), 1)
    in_window = jnp.abs(kj - blk - qi) <= WINDOW
    first_head = _iota2((2 * blk, 1), 0) < blk

    for qb in range(ATT_QB):
        n = step * ATT_QB + qb
        qs = slice(qb * blk, (qb + 1) * blk)
        row0 = pl.multiple_of(n * blk, blk)
        cos = cos_ref[pl.ds(row0, blk), :]
        sin = sin_ref[pl.ds(row0, blk), :]
        valid = in_window & ((kj >= blk) | (n > 0)) & ((kj < 2 * blk) | (n < nb - 1))
        for h in range(ATT_KV_HEADS):
            hs = slice(h * ATT_HD, (h + 1) * ATT_HD)
            q0 = _rope(q_ref[qs, 2 * h * ATT_HD:(2 * h + 1) * ATT_HD], cos, sin)
            q1 = _rope(q_ref[qs, (2 * h + 1) * ATT_HD:(2 * h + 2) * ATT_HD], cos, sin)
            q2 = jnp.concatenate([q0, q1], axis=0).astype(BF16)
            kw = kr_ref[pl.ds(row0, 3 * blk), hs]
            vw = vb_ref[pl.ds(row0, 3 * blk), hs]
            s_loc = jnp.where(valid, _dot_nt(q2, kw) * scale, NEG_BIG)
            s_cx = _dot_nt(q2, kc_ref[:, hs].astype(BF16)) * scale
            sink = jnp.where(first_head, sink_ref[2 * h], sink_ref[2 * h + 1])
            m = jnp.maximum(jnp.maximum(jnp.max(s_loc, axis=-1, keepdims=True),
                                        jnp.max(s_cx, axis=-1, keepdims=True)), sink)
            e_loc = jnp.exp(s_loc - m)
            e_cx = jnp.exp(s_cx - m)
            den = (jnp.sum(e_loc, axis=-1, keepdims=True) + jnp.sum(e_cx, axis=-1, keepdims=True)
                   + jnp.exp(sink - m))
            o = _dot(e_loc.astype(BF16), vw) + _dot(e_cx.astype(BF16), vc_ref[:, hs].astype(BF16))
            o = o / den
            o_ref[qs, 2 * h * ATT_HD:(2 * h + 1) * ATT_HD] = o[0:blk].astype(BF16)
            o_ref[qs, (2 * h + 1) * ATT_HD:(2 * h + 2) * ATT_HD] = o[blk:2 * blk].astype(BF16)


def _att_ctx_kernel(q_ref, kc_ref, vc_ref, sink_ref, o_ref):
    lc = q_ref.shape[0]
    scale = ATT_HD ** -0.5
    first_head = _iota2((2 * lc, 1), 0) < lc
    for h in range(ATT_KV_HEADS):
        hs = slice(h * ATT_HD, (h + 1) * ATT_HD)
        q2 = jnp.concatenate([q_ref[:, 2 * h * ATT_HD:(2 * h + 1) * ATT_HD],
                              q_ref[:, (2 * h + 1) * ATT_HD:(2 * h + 2) * ATT_HD]], axis=0).astype(BF16)
        s = _dot_nt(q2, kc_ref[:, hs].astype(BF16)) * scale
        sink = jnp.where(first_head, sink_ref[2 * h], sink_ref[2 * h + 1])
        m = jnp.maximum(jnp.max(s, axis=-1, keepdims=True), sink)
        e = jnp.exp(s - m)
        den = jnp.sum(e, axis=-1, keepdims=True) + jnp.exp(sink - m)
        o = _dot(e.astype(BF16), vc_ref[:, hs].astype(BF16)) / den
        o_ref[:, 2 * h * ATT_HD:(2 * h + 1) * ATT_HD] = o[0:lc].astype(BF16)
        o_ref[:, (2 * h + 1) * ATT_HD:(2 * h + 2) * ATT_HD] = o[lc:2 * lc].astype(BF16)


def _attention(p, geom, sink, cos, sin, ctx_out):
    b, t, lc = geom.b, geom.t, geom.lc
    rq = CHUNK * ATT_QB
    nb = t // rq
    ctx0 = (b * t) // lc
    qcol = 2048 // ATT_W
    kcol = (2048 + ATT_W) // ATT_KV_W
    vcol = kcol + 1
    smem = pl.BlockSpec(memory_space=pltpu.SMEM)
    out_l = pl.pallas_call(
        functools.partial(_att_kernel, t=t),
        grid=(b, nb),
        in_specs=[
            pl.BlockSpec((rq, ATT_W), lambda bi, n: (bi * nb + n, qcol)),
            pl.BlockSpec((t, ATT_KV_W), lambda bi, n: (bi, kcol)),
            pl.BlockSpec((t, ATT_KV_W), lambda bi, n: (bi, vcol)),
            pl.BlockSpec((lc, ATT_KV_W), lambda bi, n: (ctx0 + bi, kcol)),
            pl.BlockSpec((lc, ATT_KV_W), lambda bi, n: (ctx0 + bi, vcol)),
            pl.BlockSpec((t, ATT_HD), lambda bi, n: (0, 0)),
            pl.BlockSpec((t, ATT_HD), lambda bi, n: (0, 0)),
            smem,
        ],
        out_specs=pl.BlockSpec((rq, ATT_W), lambda bi, n: (bi * nb + n, 0)),
        out_shape=jax.ShapeDtypeStruct((b * t, ATT_W), BF16),
        scratch_shapes=[pltpu.VMEM((t + 2 * CHUNK, ATT_KV_W), BF16),
                        pltpu.VMEM((t + 2 * CHUNK, ATT_KV_W), BF16)],
        compiler_params=_cparams(("arbitrary", "arbitrary")),
        name="att_lat",
    )(p, p, p, p, p, cos, sin, sink)
    if not ctx_out:
        return out_l, None
    out_c = pl.pallas_call(
        _att_ctx_kernel,
        grid=(b,),
        in_specs=[
            pl.BlockSpec((lc, ATT_W), lambda bi: (ctx0 + bi, qcol)),
            pl.BlockSpec((lc, ATT_KV_W), lambda bi: (ctx0 + bi, kcol)),
            pl.BlockSpec((lc, ATT_KV_W), lambda bi: (ctx0 + bi, vcol)),
            smem,
        ],
        out_specs=pl.BlockSpec((lc, ATT_W), lambda bi: (bi, 0)),
        out_shape=jax.ShapeDtypeStruct((b * lc, ATT_W), BF16),
        compiler_params=_cparams(("arbitrary",)),
        name="att_ctx",
    )(p, p, p, sink)
    return out_l, out_c


N_DH = 2 * SSD_HEADS


def _pack3(v):
    hi = v.astype(BF16).astype(F32)
    r = v - hi
    mid = r.astype(BF16).astype(F32)
    lo = r - mid
    return (hi + pltpu.roll(mid, N_DH, axis=1) + pltpu.roll(lo, 2 * N_DH, axis=1)).astype(BF16)


def _unpack3(r):
    return r + pltpu.roll(r, LANES - N_DH, axis=1) + pltpu.roll(r, LANES - 2 * N_DH, axis=1)


def _conv_silu(u, prev_row, next_row, w, bias):
    n = u.shape[0]
    edge = _iota2((SUBLANES, u.shape[1]), 0)
    up = pltpu.roll(u, 1, axis=0)
    un = pltpu.roll(u, n - 1, axis=0)
    up = jnp.concatenate([jnp.where(edge == 0, prev_row, up[0:SUBLANES]), up[SUBLANES:]], axis=0)
    un = jnp.concatenate([un[0:n - SUBLANES], jnp.where(edge == SUBLANES - 1, next_row, un[n - SUBLANES:])], axis=0)
    y = w[0:1, :] * up + w[1:2, :] * u + w[2:3, :] * un + bias
    return y * _sigmoid(y)


def _softplus(v):
    return jnp.maximum(v, 0.0) + jnp.log1p(jnp.exp(-jnp.abs(v)))


def _ssd_dt_la(dt_raw, alog_ref, dtb_ref):
    lane = _iota2((1, LANES), 1)
    live = lane < N_DH
    a_row = jnp.where(live, -jnp.exp(alog_ref[...]) * LOG2E, 0.0)
    dt = jnp.where(live, _softplus(dt_raw + dtb_ref[...]), 0.0)
    return dt, dt * a_row


def _tri(upper):
    r = _iota2((CHUNK, CHUNK), 0)
    c = _iota2((CHUNK, CHUNK), 1)
    return jnp.where((c >= r) if upper else (c <= r), 1.0, 0.0).astype(BF16)


def _ssd_bstate_kernel(x_ref, xp_ref, xn_ref, bc_ref, bp_ref, bn_ref, dt_ref, cw_ref, cb_ref,
                       alog_ref, dtb_ref, exp_ref, sb_ref, xs_ref, bco_ref, s_ref, *, geom):
    s = pl.program_id(1)
    j = geom.bwd_step(s)

    @pl.when(s == 0)
    def _():
        s_ref[...] = jnp.zeros_like(s_ref)

    first = geom.seg_first(j)
    last = geom.seg_last(j)
    hl = SUBLANES - 1
    xs_all = _conv_silu(x_ref[...],
                        jnp.where(first, 0.0, xp_ref[hl:hl + 1, :]), jnp.where(last, 0.0, xn_ref[0:1, :]),
                        cw_ref[:, 0:SSD_W], cb_ref[:, 0:SSD_W])
    bc_all = _conv_silu(bc_ref[...],
                        jnp.where(first, 0.0, bp_ref[hl:hl + 1, :]), jnp.where(last, 0.0, bn_ref[0:1, :]),
                        cw_ref[:, SSD_W:], cb_ref[:, SSD_W:]).astype(BF16)
    xs_ref[...] = xs_all
    bco_ref[...] = bc_all
    dt_all, la_all = _ssd_dt_la(dt_ref[...], alog_ref, dtb_ref)
    lane = _iota2((CHUNK, LANES), 1)
    ex = exp_ref[:, SSD_W:2 * SSD_W]
    gw = SSD_W // SSD_GROUPS
    tri_u = _tri(True)
    for c in reversed(range(CPS)):
        rows = slice(c * CHUNK, (c + 1) * CHUNK)
        sb_ref[c] = s_ref[...]
        rb = jnp.where(lane < N_DH, _unpack3(_dot(tri_u, _pack3(la_all[rows]))), 0.0)
        rbx = _dot(_pack3(rb), ex)
        dtx = _dot(_pack3(dt_all[rows]), ex)
        xw = (xs_all[rows] * dtx * jnp.exp2(rbx[0:1, :] - rbx)).astype(BF16)
        cdec = jnp.exp2(rbx[0:1, :])
        for g in range(SSD_GROUPS):
            gs = slice(g * gw, (g + 1) * gw)
            bg = bc_all[rows, g * SSD_STATE:(g + 1) * SSD_STATE]
            s_ref[g] = s_ref[g] * cdec[:, gs] + _dot_tn(bg, xw[:, gs])


def _ssd_fwd_kernel(z_ref, xs_ref, bc_ref, dt_ref, sb_ref, alog_ref, dtb_ref, dsk_ref, nw_ref,
                    exp_ref, expl_ref, o_ref, s_ref):
    @pl.when(pl.program_id(1) == 0)
    def _():
        s_ref[...] = jnp.zeros_like(s_ref)

    dt_all, la_all = _ssd_dt_la(dt_ref[...], alog_ref, dtb_ref)
    lane = _iota2((CHUNK, LANES), 1)
    r = _iota2((CHUNK, CHUNK), 0)
    cidx = _iota2((CHUNK, CHUNK), 1)
    lower = r >= cidx
    diag = cidx == r
    lane64 = lane < SSD_HD
    fwd_lane = (lane % N_DH) < SSD_HEADS
    gw = SSD_W // SSD_GROUPS
    hpg = SSD_HEADS // SSD_GROUPS
    tri_l = _tri(False)
    tri_u = _tri(True)
    for c in range(CPS):
        rows = slice(c * CHUNK, (c + 1) * CHUNK)
        xs = xs_ref[rows, :]
        lap = _pack3(la_all[rows])
        cum = jnp.where(lane < N_DH,
                        _unpack3(jnp.where(fwd_lane, _dot(tri_l, lap), _dot(tri_u, lap))), 0.0)
        cum_t = cum.T
        cump = _pack3(cum)
        cumx = _dot(cump, exp_ref[...])
        dt = dt_all[rows]
        dt_t = dt.T
        xdt_f = xs * _dot(_pack3(dt), exp_ref[:, 0:SSD_W])

        cgs = []
        cbs = []
        for g in range(SSD_GROUPS):
            bg = bc_ref[rows, g * SSD_STATE:(g + 1) * SSD_STATE]
            cg = bc_ref[rows, SSD_BC_W + g * SSD_STATE:SSD_BC_W + (g + 1) * SSD_STATE]
            cgs.append((bg, cg))
            cbs.append(_dot_nt(cg, bg))
        col_f = _dot(cump, expl_ref[:, 0:SSD_HEADS * LANES])
        col_b = _dot(cump, expl_ref[:, SSD_HEADS * LANES:])
        y_parts = []
        for hp in range(SSD_HEADS // 2):
            g = (2 * hp) // hpg
            ps = slice(hp * LANES, (hp + 1) * LANES)
            lhs = []
            for e in range(2):
                h = 2 * hp + e
                hb = SSD_HEADS + h
                hs = slice(h * LANES, (h + 1) * LANES)
                dec = jnp.exp2(jnp.where(lower, col_f[:, hs] - cum_t[h:h + 1, :],
                                        col_b[:, hs] - cum_t[hb:hb + 1, :]))
                dtf = dt_t[h:h + 1, :]
                dtb = dt_t[hb:hb + 1, :]
                dts = jnp.where(diag, dtf + dtb, jnp.where(lower, dtf, dtb))
                lhs.append((cbs[g] * dec * dts).astype(BF16))
            xp = xs[:, ps]
            rhs = jnp.concatenate([jnp.where(lane64, xp, 0.0).astype(BF16),
                                   jnp.where(lane64, 0.0, xp).astype(BF16)], axis=0)
            y_parts.append(_dot(jnp.concatenate(lhs, axis=1), rhs))
        y = jnp.concatenate(y_parts, axis=1)

        e_f = jnp.exp2(cumx[:, 0:SSD_W])
        e_b = jnp.exp2(cumx[:, SSD_W:])
        yoff_f = jnp.concatenate([_dot(cgs[g][1], s_ref[g].astype(BF16)) for g in range(SSD_GROUPS)], axis=1)
        yoff_b = jnp.concatenate([_dot(cgs[g][1], sb_ref[c, g].astype(BF16)) for g in range(SSD_GROUPS)], axis=1)
        y = y + yoff_f * e_f + yoff_b * e_b

        tot = cumx[CHUNK - 1:CHUNK, 0:SSD_W]
        xw = (xdt_f * jnp.exp2(tot - cumx[:, 0:SSD_W])).astype(BF16)
        cdec = jnp.exp2(tot)
        for g in range(SSD_GROUPS):
            gs = slice(g * gw, (g + 1) * gw)
            s_ref[g] = s_ref[g] * cdec[:, gs] + _dot_tn(cgs[g][0], xw[:, gs])

        z = z_ref[rows, :]
        y = (y + dsk_ref[...] * xs) * (z * _sigmoid(z))
        o_ref[rows, :] = (_rms(y) * nw_ref[...]).astype(BF16)


def _expand_consts():
    kk = np.arange(N_DH)
    ex = np.zeros((LANES, 2 * SSD_W), np.float32)
    exl = np.zeros((LANES, N_DH * LANES), np.float32)
    for part in range(3):
        for p in range(SSD_HD):
            ex[part * N_DH + kk, kk * SSD_HD + p] = 1.0
        for q in range(LANES):
            exl[part * N_DH + kk, kk * LANES + q] = 1.0
    return jnp.asarray(ex, BF16), jnp.asarray(exl, BF16)


def _ssd(p, pdt, geom, conv_w, conv_b, alog, dtb, dsk, nw):
    b, nst = geom.b, geom.nst
    ex, exl = _expand_consts()
    zcol = 3072 // SSD_W
    xcol = 4096 // SSD_W
    bccol = (4096 + SSD_W) // (2 * SSD_BC_W)
    n8 = geom.rows // SUBLANES
    sub = RS // SUBLANES
    gw = SSD_W // SSD_GROUPS
    sb_block = (None, None, CPS, SSD_GROUPS, SSD_STATE, gw)

    def const(shape):
        return pl.BlockSpec(shape, lambda bi, s: (0,) * len(shape))

    state = pltpu.VMEM((SSD_GROUPS, SSD_STATE, gw), F32)
    rb = lambda bi, s: geom.row_block(bi, geom.bwd_step(s))
    prev8 = lambda bi, s: jnp.maximum(rb(bi, s) * sub - 1, 0)
    next8 = lambda bi, s: jnp.minimum(rb(bi, s) * sub + sub, n8 - 1)
    sb, xs, bcm = pl.pallas_call(
        functools.partial(_ssd_bstate_kernel, geom=geom),
        grid=(b, nst),
        in_specs=[pl.BlockSpec((RS, SSD_W), lambda bi, s: (rb(bi, s), xcol)),
                  pl.BlockSpec((SUBLANES, SSD_W), lambda bi, s: (prev8(bi, s), xcol)),
                  pl.BlockSpec((SUBLANES, SSD_W), lambda bi, s: (next8(bi, s), xcol)),
                  pl.BlockSpec((RS, 2 * SSD_BC_W), lambda bi, s: (rb(bi, s), bccol)),
                  pl.BlockSpec((SUBLANES, 2 * SSD_BC_W), lambda bi, s: (prev8(bi, s), bccol)),
                  pl.BlockSpec((SUBLANES, 2 * SSD_BC_W), lambda bi, s: (next8(bi, s), bccol)),
                  pl.BlockSpec((RS, LANES), lambda bi, s: (rb(bi, s), 0)),
                  const(conv_w.shape), const(conv_b.shape), const(alog.shape), const(dtb.shape),
                  const(ex.shape)],
        out_specs=[pl.BlockSpec(sb_block, lambda bi, s: (bi, geom.bwd_step(s), 0, 0, 0, 0)),
                   pl.BlockSpec((RS, SSD_W), lambda bi, s: (rb(bi, s), 0)),
                   pl.BlockSpec((RS, 2 * SSD_BC_W), lambda bi, s: (rb(bi, s), 0))],
        out_shape=[jax.ShapeDtypeStruct((b, nst, CPS, SSD_GROUPS, SSD_STATE, gw), F32),
                   jax.ShapeDtypeStruct((geom.rows, SSD_W), F32),
                   jax.ShapeDtypeStruct((geom.rows, 2 * SSD_BC_W), BF16)],
        scratch_shapes=[state],
        compiler_params=_cparams(("arbitrary", "arbitrary")),
        name="ssd_bstate",
    )(p, p, p, p, p, p, pdt, conv_w, conv_b, alog, dtb, ex)

    rf = lambda bi, s: geom.row_block(bi, s)
    return pl.pallas_call(
        _ssd_fwd_kernel,
        grid=(b, nst),
        in_specs=[pl.BlockSpec((RS, SSD_W), lambda bi, s: (rf(bi, s), zcol)),
                  pl.BlockSpec((RS, SSD_W), lambda bi, s: (rf(bi, s), 0)),
                  pl.BlockSpec((RS, 2 * SSD_BC_W), lambda bi, s: (rf(bi, s), 0)),
                  pl.BlockSpec((RS, LANES), lambda bi, s: (rf(bi, s), 0)),
                  pl.BlockSpec(sb_block, lambda bi, s: (bi, s, 0, 0, 0, 0)),
                  const(alog.shape), const(dtb.shape), const(dsk.shape), const(nw.shape),
                  const(ex.shape), const(exl.shape)],
        out_specs=pl.BlockSpec((RS, SSD_W), lambda bi, s: (rf(bi, s), 0)),
        out_shape=jax.ShapeDtypeStruct((geom.rows, SSD_W), BF16),
        scratch_shapes=[state],
        compiler_params=_cparams(("arbitrary", "arbitrary")),
        name="ssd_fwd",
    )(p, xs, bcm, pdt, sb, alog, dtb, dsk, nw, ex, exl)


def _rope_tables(t):
    half = ATT_HD // 2
    freqs = ROPE_BASE ** (-jnp.arange(0, half, 2, dtype=F32) / half)
    pos = jnp.arange(t)
    ang_r = (pos // GRID_W).astype(F32)[:, None] * freqs[None, :]
    ang_c = (pos % GRID_W).astype(F32)[:, None] * freqs[None, :]
    cos = jnp.concatenate([jnp.cos(ang_r), jnp.cos(ang_r), jnp.cos(ang_c), jnp.cos(ang_c)], axis=-1)
    sin = jnp.concatenate([-jnp.sin(ang_r), jnp.sin(ang_r), -jnp.sin(ang_c), jnp.sin(ang_c)], axis=-1)
    return cos, sin


def _chunk_major_gu(w_gu):
    depth, d, _ = w_gu.shape
    w = w_gu.astype(BF16).reshape(depth, d, 2, D_FF // FFN_TF, FFN_TF).transpose(0, 3, 1, 2, 4)
    return w.reshape(depth, D_FF // FFN_TF, d, 2 * FFN_TF)


def _pad_lanes(v):
    v = v.reshape(1, -1)
    return jnp.pad(v, ((0, 0), (0, LANES - v.shape[1])))


def _mixer(p, pdt, geom, layer, ctx_out, cos, sin, ret_log_decay, ret_norm_w, attn_sink, ssd_conv_w,
           ssd_conv_b, ssd_a_log, ssd_dt_bias, ssd_d, ssd_norm_w):
    ld = jnp.broadcast_to(ret_log_decay[layer][:, :, None], (2, RET_HEADS, LANES))
    ra = _retention(p, geom, ld, ret_norm_w[layer].reshape(1, RET_W))
    at_l, at_c = _attention(p, geom, attn_sink[layer], cos, sin, ctx_out)
    ss = _ssd(p, pdt, geom, ssd_conv_w[layer], ssd_conv_b[layer].reshape(1, -1),
              _pad_lanes(ssd_a_log[layer]), _pad_lanes(ssd_dt_bias[layer]),
              jnp.repeat(ssd_d[layer], SSD_HD).reshape(1, SSD_W),
              ssd_norm_w[layer].reshape(1, SSD_W))
    return ra, at_l, at_c, ss


def _forward(x, c, ctx, c_ctx, w_ada, b_ada, norm_w, ffn1_gu, ffn1_down, ffn2_gu, ffn2_down,
             w_in, w_out, ret_log_decay, ret_norm_w, attn_sink, ssd_conv_w, ssd_conv_b,
             ssd_a_log, ssd_dt_bias, ssd_d, ssd_norm_w):
    b, t, d = x.shape
    lc = ctx.shape[1]
    depth = w_ada.shape[0]
    geom = _Geom(b, t, lc)
    n_lat = b * t
    tm = math.gcd(512, math.gcd(t, b * lc))

    def mod_row(i):
        r0 = i * tm
        return jnp.where(r0 < n_lat, 1 + r0 // t, 0)

    cond = jnp.concatenate([c_ctx[None, :], c, jnp.zeros((SUBLANES - 1 - b, d), F32)], axis=0)
    mod = _adaln(cond, w_ada, b_ada)
    mod = mod.reshape(SUBLANES, depth, 3, 3, d).transpose(1, 2, 0, 3, 4).reshape(depth * 3, SUBLANES, 3, d)

    w1gu = _chunk_major_gu(ffn1_gu)
    w1d = ffn1_down.astype(BF16)
    w2gu = _chunk_major_gu(ffn2_gu)
    w2d = ffn2_down.astype(BF16)
    wi = w_in.astype(BF16)
    wi_dt = jnp.pad(wi[:, :, IN_MAIN:], ((0, 0), (0, 0), (0, LANES - N_DH)))
    wo = w_out.astype(BF16)

    cos, sin = _rope_tables(t)
    srcs = [x.reshape(n_lat, d), ctx.reshape(b * lc, d)]
    rows_all = n_lat + b * lc
    for layer in range(depth):
        last = layer == depth - 1
        xs = _ffn(srcs, rows_all, mod, norm_w, w1gu, w1d, layer=layer, sub=0, tm=tm, mod_row=mod_row)
        p, pdt = _proj_in(xs, mod, norm_w, wi, wi_dt, layer=layer, tm=tm, mod_row=mod_row)
        ra, at_l, at_c, ss = _mixer(p, pdt, geom, layer, not last, cos, sin, ret_log_decay, ret_norm_w,
                                    attn_sink, ssd_conv_w, ssd_conv_b, ssd_a_log, ssd_dt_bias, ssd_d,
                                    ssd_norm_w)
        at = at_l if last else jnp.concatenate([at_l, at_c], axis=0)
        rows_out = n_lat if last else rows_all
        xs = _proj_out(xs, rows_out, ra, at, ss, wo, mod, norm_w, layer=layer, tm=tm, mod_row=mod_row)
        xs = _ffn([xs], rows_out, mod, norm_w, w2gu, w2d, layer=layer, sub=2, tm=tm, mod_row=mod_row)
        srcs = [xs]
    return xs.reshape(b, t, d)


def kernel(x, c, ctx, c_ctx, w_ada, b_ada, norm_w, ffn1_gu, ffn1_down, ffn2_gu, ffn2_down,
           w_in, w_out, ret_log_decay, ret_norm_w, attn_sink, ssd_conv_w, ssd_conv_b,
           ssd_a_log, ssd_dt_bias, ssd_d, ssd_norm_w):
    return _forward(x, c, ctx, c_ctx, w_ada, b_ada, norm_w, ffn1_gu, ffn1_down, ffn2_gu, ffn2_down,
                    w_in, w_out, ret_log_decay, ret_norm_w, attn_sink, ssd_conv_w, ssd_conv_b,
                    ssd_a_log, ssd_dt_bias, ssd_d, ssd_norm_w)
```

```python
import functools
import math

import jax
import jax.numpy as jnp
import numpy as np
from jax import lax
from jax.experimental import pallas as pl
from jax.experimental.pallas import tpu as pltpu

F32 = jnp.float32
BF16 = jnp.bfloat16

D_MODEL = 2048
GRID_W = 64
RET_HEADS = 4
RET_DK = 128
RET_W = 512
ATT_HEADS = 4
ATT_KV_HEADS = 2
ATT_HD = 128
ATT_W = 512
ATT_KV_W = 256
WINDOW = 128
SSD_HEADS = 16
SSD_HD = 64
SSD_W = 1024
SSD_GROUPS = 2
SSD_STATE = 128
SSD_BC_W = 256
CHUNK = 128
D_FF = 5632
FFN_RES = 0.5
FFN_TF = 512
ROPE_BASE = 10000.0
NORM_EPS = 1e-6
N_MOD = 9
IN_MAIN = 5632
NEG_BIG = -1e30
LOG2E = 1.4426950408889634

CPS = 2
ATT_QB = 2
RS = CHUNK * CPS
LANES = 128
SUBLANES = 8

VMEM_LIMIT = 56 * 1024 * 1024


def _cparams(sem):
    return pltpu.CompilerParams(dimension_semantics=sem, vmem_limit_bytes=VMEM_LIMIT)


def _sigmoid(v):
    return 1.0 / (1.0 + jnp.exp(-v))


def _rms(v):
    return v * lax.rsqrt(jnp.mean(v * v, axis=-1, keepdims=True) + NORM_EPS)


def _dot(a, b):
    return jnp.dot(a, b, preferred_element_type=F32)


def _dot_nt(a, b):
    return lax.dot_general(a, b, (((1,), (1,)), ((), ())), preferred_element_type=F32)


def _dot_tn(a, b):
    return lax.dot_general(a, b, (((0,), (0,)), ((), ())), preferred_element_type=F32)


def _iota2(shape, dim):
    return lax.broadcasted_iota(jnp.int32, shape, dim)


SLAB = 16
SLAB_UNROLL = 4


def _row_slabs(n_rows, body):
    def step(r, carry):
        body(pl.ds(pl.multiple_of(r * SLAB, SLAB), SLAB))
        return carry

    lax.fori_loop(0, n_rows // SLAB, step, 0, unroll=SLAB_UNROLL)


def _modnorm_rows(x_ref, h_ref, nw_row, mod_ref, zero_ref=None):
    gain = nw_row * (1.0 + mod_ref[1:2, :])
    shift = mod_ref[0:1, :]

    def body(rows):
        h_ref[rows, :] = (_rms(x_ref[rows, :]) * gain + shift).astype(BF16)
        if zero_ref is not None:
            zero_ref[rows, :] = jnp.zeros((SLAB, zero_ref.shape[1]), zero_ref.dtype)

    _row_slabs(x_ref.shape[0], body)


def _residual_rows(x_ref, y_ref, gain, o_ref):
    def body(rows):
        o_ref[rows, :] = x_ref[rows, :] + _rms(y_ref[rows, :]) * gain

    _row_slabs(x_ref.shape[0], body)


def _adaln_kernel(c_ref, w_ref, b_ref, o_ref):
    cnd = c_ref[...]
    s = (cnd * _sigmoid(cnd)).astype(BF16)
    o_ref[...] = _dot(s, w_ref[...].astype(BF16)) + b_ref[...]


def _adaln(cond, w_ada, b_ada):
    depth, d, n = w_ada.shape
    tn = 1024
    nt = n // tn
    return pl.pallas_call(
        _adaln_kernel,
        grid=(depth, nt),
        in_specs=[
            pl.BlockSpec((SUBLANES, d), lambda l, j: (0, 0)),
            pl.BlockSpec((None, d, tn), lambda l, j: (l, 0, j)),
            pl.BlockSpec((None, 1, tn), lambda l, j: (l, 0, j)),
        ],
        out_specs=pl.BlockSpec((SUBLANES, tn), lambda l, j: (0, l * nt + j)),
        out_shape=jax.ShapeDtypeStruct((SUBLANES, depth * n), F32),
        compiler_params=_cparams(("arbitrary", "arbitrary")),
        name="adaln",
    )(cond, w_ada, b_ada.reshape(depth, 1, n))


def _ffn_kernel(*refs, sub, split):
    j = pl.program_id(1)
    is_first = j == 0
    is_last = j == pl.num_programs(1) - 1
    if split is None:
        x_ref, mod_ref, nw_ref, wg_ref, wu_ref, wd_ref, o_ref, h_ref, acc_ref = refs
        sources = [(x_ref, None)]
    else:
        xa_ref, xb_ref, mod_ref, nw_ref, wg_ref, wu_ref, wd_ref, o_ref, h_ref, acc_ref = refs
        from_a = pl.program_id(0) < split
        sources = [(xa_ref, from_a), (xb_ref, jnp.logical_not(from_a))]

    def prologue(x_ref):
        _modnorm_rows(x_ref, h_ref, nw_ref[2 * sub:2 * sub + 1, :], mod_ref, zero_ref=acc_ref)

    def epilogue(x_ref):
        gain = (FFN_RES * mod_ref[2:3, :]) * nw_ref[2 * sub + 1:2 * sub + 2, :]
        _residual_rows(x_ref, acc_ref, gain, o_ref)

    for x_ref, active in sources:
        pl.when(is_first if active is None else is_first & active)(functools.partial(prologue, x_ref))

    h = h_ref[...]
    g = _dot(h, wg_ref[...])
    u = _dot(h, wu_ref[...])
    a = (g * _sigmoid(g) * u).astype(BF16)
    acc_ref[...] += _dot(a, wd_ref[...])

    for x_ref, active in sources:
        pl.when(is_last if active is None else is_last & active)(functools.partial(epilogue, x_ref))


def _ffn(srcs, rows_out, mod, nw, w_gu, w_down, *, layer, sub, tm, mod_row):
    d = srcs[0].shape[1]
    tf = FFN_TF
    nf = D_FF // tf
    if len(srcs) == 1:
        split = None
        x_specs = [pl.BlockSpec((tm, d), lambda i, j: (i, 0))]
    else:
        split = srcs[0].shape[0] // tm
        x_specs = [pl.BlockSpec((tm, d), lambda i, j: (jnp.minimum(i, split - 1), 0)),
                   pl.BlockSpec((tm, d), lambda i, j: (jnp.maximum(i - split, 0), 0))]
    return pl.pallas_call(
        functools.partial(_ffn_kernel, sub=sub, split=split),
        grid=(rows_out // tm, nf),
        in_specs=x_specs + [
            pl.BlockSpec((None, None, 3, d), lambda i, j: (layer * 3 + sub, mod_row(i), 0, 0)),
            pl.BlockSpec((None, 6, d), lambda i, j: (layer, 0, 0)),
            pl.BlockSpec((None, d, tf), lambda i, j: (layer, 0, j)),
            pl.BlockSpec((None, d, tf), lambda i, j: (layer, 0, j + nf)),
            pl.BlockSpec((None, tf, d), lambda i, j: (layer, j, 0)),
        ],
        out_specs=pl.BlockSpec((tm, d), lambda i, j: (i, 0)),
        out_shape=jax.ShapeDtypeStruct((rows_out, d), F32),
        scratch_shapes=[pltpu.VMEM((tm, d), BF16), pltpu.VMEM((tm, d), F32)],
        compiler_params=_cparams(("arbitrary", "arbitrary")),
        name=f"ffn{sub}",
    )(*srcs, mod, nw, w_gu, w_gu, w_down)


def _proj_in_kernel(x_ref, mod_ref, nw_ref, w_ref, wdt_ref, o_ref, odt_ref, h_ref, *, tn):
    j = pl.program_id(1)

    @pl.when(j == 0)
    def _():
        _modnorm_rows(x_ref, h_ref, nw_ref[2:3, :], mod_ref)
        odt_ref[...] = _dot(h_ref[...], wdt_ref[...])

    def project(jj):
        o_ref[...] = _dot(h_ref[...], w_ref[:, jj * tn:(jj + 1) * tn])

    for jj in range(IN_MAIN // tn):
        pl.when(j == jj)(functools.partial(project, jj))


def _proj_in(xs, mod, nw, w_in, w_dt, *, layer, tm, mod_row):
    rows, d = xs.shape
    n = IN_MAIN
    tn = n // 2
    nfull = w_in.shape[2]
    return pl.pallas_call(
        functools.partial(_proj_in_kernel, tn=tn),
        grid=(rows // tm, n // tn),
        in_specs=[
            pl.BlockSpec((tm, d), lambda i, j: (i, 0)),
            pl.BlockSpec((None, None, 3, d), lambda i, j: (layer * 3 + 1, mod_row(i), 0, 0)),
            pl.BlockSpec((None, 6, d), lambda i, j: (layer, 0, 0)),
            pl.BlockSpec((None, d, nfull), lambda i, j: (layer, 0, 0), pipeline_mode=pl.Buffered(1)),
            pl.BlockSpec((None, d, LANES), lambda i, j: (layer, 0, 0)),
        ],
        out_specs=[pl.BlockSpec((tm, tn), lambda i, j: (i, j)),
                   pl.BlockSpec((tm, LANES), lambda i, j: (i, 0))],
        out_shape=[jax.ShapeDtypeStruct((rows, n), F32), jax.ShapeDtypeStruct((rows, LANES), F32)],
        scratch_shapes=[pltpu.VMEM((tm, d), BF16)],
        compiler_params=_cparams(("arbitrary", "arbitrary")),
        name="proj_in",
    )(xs, mod, nw, w_in, w_dt)


def _proj_out_kernel(x_ref, ra_ref, at_ref, ss_ref, w_ref, mod_ref, nw_ref, o_ref):
    y = _dot(ra_ref[...], w_ref[0:RET_W, :])
    y = y + _dot(at_ref[...], w_ref[RET_W:RET_W + ATT_W, :])
    y = y + _dot(ss_ref[...], w_ref[RET_W + ATT_W:, :])
    o_ref[...] = x_ref[...] + _rms(y) * (mod_ref[2:3, :] * nw_ref[3:4, :])


def _proj_out(xs, rows_out, ra, at, ss, w_out, mod, nw, *, layer, tm, mod_row):
    d = xs.shape[1]
    return pl.pallas_call(
        _proj_out_kernel,
        grid=(rows_out // tm,),
        in_specs=[
            pl.BlockSpec((tm, d), lambda i: (i, 0)),
            pl.BlockSpec((tm, RET_W), lambda i: (i, 0)),
            pl.BlockSpec((tm, ATT_W), lambda i: (i, 0)),
            pl.BlockSpec((tm, SSD_W), lambda i: (i, 0)),
            pl.BlockSpec((None, d, d), lambda i: (layer, 0, 0)),
            pl.BlockSpec((None, None, 3, d), lambda i: (layer * 3 + 1, mod_row(i), 0, 0)),
            pl.BlockSpec((None, 6, d), lambda i: (layer, 0, 0)),
        ],
        out_specs=pl.BlockSpec((tm, d), lambda i: (i, 0)),
        out_shape=jax.ShapeDtypeStruct((rows_out, d), F32),
        compiler_params=_cparams(("arbitrary",)),
        name="proj_out",
    )(xs, ra, at, ss, w_out, mod, nw)


class _Geom:
    def __init__(self, b, t, lc):
        assert t % RS == 0 and lc % RS == 0
        self.b, self.t, self.lc = b, t, lc
        self.nlat = t // RS
        self.nctx = lc // RS
        self.nst = self.nlat + self.nctx
        self.rows = b * (t + lc)

    def row_block(self, bi, j):
        return jnp.where(j < self.nctx,
                         self.b * self.nlat + bi * self.nctx + j,
                         bi * self.nlat + (j - self.nctx))

    def bwd_step(self, s):
        return jnp.where(s < self.nctx, self.nctx - 1 - s, self.nst + self.nctx - 1 - s)

    def seg_first(self, j):
        return (j == 0) | (j == self.nctx)

    def seg_last(self, j):
        return (j == self.nctx - 1) | (j == self.nst - 1)


def _ret_bstate_body(k_ref, v_ref, ld_ref, sb_ref, s_ref):
    jrow = _iota2((CHUNK, CHUNK), 0).astype(F32)
    for h in range(RET_HEADS):
        hs = slice(h * RET_DK, (h + 1) * RET_DK)
        lgb = -jnp.abs(ld_ref[1, h:h + 1, :])
        kdec = jnp.exp(lgb * jrow)
        cdec = jnp.exp(lgb * float(CHUNK))
        for c in reversed(range(CPS)):
            rows = slice(c * CHUNK, (c + 1) * CHUNK)
            sb_ref[c, h] = s_ref[h]
            kd = (k_ref[rows, hs] * kdec).astype(BF16)
            s_ref[h] = s_ref[h] * cdec + _dot_tn(kd, v_ref[rows, hs].astype(BF16))


def _ret_fwd_body(q_ref, k_ref, v_ref, g_ref, sb_ref, ld_ref, nw_ref, o_ref, s_ref):
    irow = _iota2((CHUNK, CHUNK), 0).astype(F32)
    rel = irow - _iota2((CHUNK, CHUNK), 1).astype(F32)
    for h in range(RET_HEADS):
        hs = slice(h * RET_DK, (h + 1) * RET_DK)
        lgf = -jnp.abs(ld_ref[0, h:h + 1, :])
        lgb = -jnp.abs(ld_ref[1, h:h + 1, :])
        dmask = (jnp.where(rel >= 0, jnp.exp(lgf * jnp.maximum(rel, 0.0)), 0.0)
                 + jnp.where(rel <= 0, jnp.exp(lgb * jnp.maximum(-rel, 0.0)), 0.0))
        qdec_f = jnp.exp(lgf * (irow + 1.0))
        qdec_b = jnp.exp(lgb * (float(CHUNK) - irow))
        kdec_f = jnp.exp(lgf * (float(CHUNK) - 1.0 - irow))
        cdec_f = jnp.exp(lgf * float(CHUNK))
        for c in range(CPS):
            rows = slice(c * CHUNK, (c + 1) * CHUNK)
            q = q_ref[rows, hs] * (RET_DK ** -0.5)
            k = k_ref[rows, hs]
            vb = v_ref[rows, hs].astype(BF16)
            inner = _dot_nt(q.astype(BF16), k.astype(BF16)) * dmask
            y = _dot(inner.astype(BF16), vb)
            y = y + _dot((q * qdec_f).astype(BF16), s_ref[h].astype(BF16))
            y = y + _dot((q * qdec_b).astype(BF16), sb_ref[c, h].astype(BF16))
            s_ref[h] = s_ref[h] * cdec_f + _dot_tn((k * kdec_f).astype(BF16), vb)
            mu = jnp.mean(y, axis=-1, keepdims=True)
            yc = y - mu
            var = jnp.mean(yc * yc, axis=-1, keepdims=True)
            gate = g_ref[rows, hs]
            yn = yc * lax.rsqrt(var + NORM_EPS) * nw_ref[:, hs]
            o_ref[rows, hs] = (yn * (gate * _sigmoid(gate))).astype(BF16)


def _rope(x, cos, sin):
    lane = _iota2(x.shape, 1)
    swapped = jnp.where((lane // 32) % 2 == 0, pltpu.roll(x, 96, axis=1), pltpu.roll(x, 32, axis=1))
    return x * cos + swapped * sin


def _att_kernel(q_ref, kl_ref, vl_ref, kc_ref, vc_ref, cos_ref, sin_ref, sink_ref, o_ref,
                kr_ref, vb_ref, *, t):
    step = pl.program_id(1)
    nb = pl.num_programs(1) * ATT_QB
    blk = CHUNK

    @pl.when(step == 0)
    def _():
        zeros = jnp.zeros((blk, ATT_KV_W), BF16)
        kr_ref[0:blk, :] = zeros
        kr_ref[t + blk:t + 2 * blk, :] = zeros
        vb_ref[0:blk, :] = zeros
        vb_ref[t + blk:t + 2 * blk, :] = zeros
        for h in range(ATT_KV_HEADS):
            hs = slice(h * ATT_HD, (h + 1) * ATT_HD)
            kr_ref[blk:t + blk, hs] = _rope(kl_ref[:, hs], cos_ref[...], sin_ref[...]).astype(BF16)
        vb_ref[blk:t + blk, :] = vl_ref[...].astype(BF16)

    scale = ATT_HD ** -0.5
    qi = _iota2((2 * blk, 3 * blk), 0) % blk
    kj = _iota2((2 * blk, 3 * blk), 1)
    in_window = jnp.abs(kj - blk - qi) <= WINDOW
    first_head = _iota2((2 * blk, 1), 0) < blk

    for qb in range(ATT_QB):
        n = step * ATT_QB + qb
        qs = slice(qb * blk, (qb + 1) * blk)
        row0 = pl.multiple_of(n * blk, blk)
        cos = cos_ref[pl.ds(row0, blk), :]
        sin = sin_ref[pl.ds(row0, blk), :]
        valid = in_window & ((kj >= blk) | (n > 0)) & ((kj < 2 * blk) | (n < nb - 1))
        for h in range(ATT_KV_HEADS):
            hs = slice(h * ATT_HD, (h + 1) * ATT_HD)
            q0 = _rope(q_ref[qs, 2 * h * ATT_HD:(2 * h + 1) * ATT_HD], cos, sin)
            q1 = _rope(q_ref[qs, (2 * h + 1) * ATT_HD:(2 * h + 2) * ATT_HD], cos, sin)
            q2 = jnp.concatenate([q0, q1], axis=0).astype(BF16)
            kw = kr_ref[pl.ds(row0, 3 * blk), hs]
            vw = vb_ref[pl.ds(row0, 3 * blk), hs]
            s_loc = jnp.where(valid, _dot_nt(q2, kw) * scale, NEG_BIG)
            s_cx = _dot_nt(q2, kc_ref[:, hs].astype(BF16)) * scale
            sink = jnp.where(first_head, sink_ref[2 * h], sink_ref[2 * h + 1])
            m = jnp.maximum(jnp.maximum(jnp.max(s_loc, axis=-1, keepdims=True),
                                        jnp.max(s_cx, axis=-1, keepdims=True)), sink)
            e_loc = jnp.exp(s_loc - m)
            e_cx = jnp.exp(s_cx - m)
            den = (jnp.sum(e_loc, axis=-1, keepdims=True) + jnp.sum(e_cx, axis=-1, keepdims=True)
                   + jnp.exp(sink - m))
            o = _dot(e_loc.astype(BF16), vw) + _dot(e_cx.astype(BF16), vc_ref[:, hs].astype(BF16))
            o = o / den
            o_ref[qs, 2 * h * ATT_HD:(2 * h + 1) * ATT_HD] = o[0:blk].astype(BF16)
            o_ref[qs, (2 * h + 1) * ATT_HD:(2 * h + 2) * ATT_HD] = o[blk:2 * blk].astype(BF16)


def _att_ctx_kernel(q_ref, kc_ref, vc_ref, sink_ref, o_ref):
    lc = q_ref.shape[0]
    scale = ATT_HD ** -0.5
    first_head = _iota2((2 * lc, 1), 0) < lc
    for h in range(ATT_KV_HEADS):
        hs = slice(h * ATT_HD, (h + 1) * ATT_HD)
        q2 = jnp.concatenate([q_ref[:, 2 * h * ATT_HD:(2 * h + 1) * ATT_HD],
                              q_ref[:, (2 * h + 1) * ATT_HD:(2 * h + 2) * ATT_HD]], axis=0).astype(BF16)
        s = _dot_nt(q2, kc_ref[:, hs].astype(BF16)) * scale
        sink = jnp.where(first_head, sink_ref[2 * h], sink_ref[2 * h + 1])
        m = jnp.maximum(jnp.max(s, axis=-1, keepdims=True), sink)
        e = jnp.exp(s - m)
        den = jnp.sum(e, axis=-1, keepdims=True) + jnp.exp(sink - m)
        o = _dot(e.astype(BF16), vc_ref[:, hs].astype(BF16)) / den
        o_ref[:, 2 * h * ATT_HD:(2 * h + 1) * ATT_HD] = o[0:lc].astype(BF16)
        o_ref[:, (2 * h + 1) * ATT_HD:(2 * h + 2) * ATT_HD] = o[lc:2 * lc].astype(BF16)


def _attention(p, geom, sink, cos, sin, ctx_out):
    b, t, lc = geom.b, geom.t, geom.lc
    rq = CHUNK * ATT_QB
    nb = t // rq
    ctx0 = (b * t) // lc
    qcol = 2048 // ATT_W
    kcol = (2048 + ATT_W) // ATT_KV_W
    vcol = kcol + 1
    smem = pl.BlockSpec(memory_space=pltpu.SMEM)
    out_l = pl.pallas_call(
        functools.partial(_att_kernel, t=t),
        grid=(b, nb),
        in_specs=[
            pl.BlockSpec((rq, ATT_W), lambda bi, n: (bi * nb + n, qcol)),
            pl.BlockSpec((t, ATT_KV_W), lambda bi, n: (bi, kcol)),
            pl.BlockSpec((t, ATT_KV_W), lambda bi, n: (bi, vcol)),
            pl.BlockSpec((lc, ATT_KV_W), lambda bi, n: (ctx0 + bi, kcol)),
            pl.BlockSpec((lc, ATT_KV_W), lambda bi, n: (ctx0 + bi, vcol)),
            pl.BlockSpec((t, ATT_HD), lambda bi, n: (0, 0)),
            pl.BlockSpec((t, ATT_HD), lambda bi, n: (0, 0)),
            smem,
        ],
        out_specs=pl.BlockSpec((rq, ATT_W), lambda bi, n: (bi * nb + n, 0)),
        out_shape=jax.ShapeDtypeStruct((b * t, ATT_W), BF16),
        scratch_shapes=[pltpu.VMEM((t + 2 * CHUNK, ATT_KV_W), BF16),
                        pltpu.VMEM((t + 2 * CHUNK, ATT_KV_W), BF16)],
        compiler_params=_cparams(("arbitrary", "arbitrary")),
        name="att_lat",
    )(p, p, p, p, p, cos, sin, sink)
    if not ctx_out:
        return out_l, None
    out_c = pl.pallas_call(
        _att_ctx_kernel,
        grid=(b,),
        in_specs=[
            pl.BlockSpec((lc, ATT_W), lambda bi: (ctx0 + bi, qcol)),
            pl.BlockSpec((lc, ATT_KV_W), lambda bi: (ctx0 + bi, kcol)),
            pl.BlockSpec((lc, ATT_KV_W), lambda bi: (ctx0 + bi, vcol)),
            smem,
        ],
        out_specs=pl.BlockSpec((lc, ATT_W), lambda bi: (bi, 0)),
        out_shape=jax.ShapeDtypeStruct((b * lc, ATT_W), BF16),
        compiler_params=_cparams(("arbitrary",)),
        name="att_ctx",
    )(p, p, p, sink)
    return out_l, out_c


N_DH = 2 * SSD_HEADS


def _pack3(v):
    hi = v.astype(BF16).astype(F32)
    r = v - hi
    mid = r.astype(BF16).astype(F32)
    lo = r - mid
    return (hi + pltpu.roll(mid, N_DH, axis=1) + pltpu.roll(lo, 2 * N_DH, axis=1)).astype(BF16)


def _unpack3(r):
    return r + pltpu.roll(r, LANES - N_DH, axis=1) + pltpu.roll(r, LANES - 2 * N_DH, axis=1)


def _conv_silu(u, prev_row, next_row, w, bias):
    n = u.shape[0]
    edge = _iota2((SUBLANES, u.shape[1]), 0)
    up = pltpu.roll(u, 1, axis=0)
    un = pltpu.roll(u, n - 1, axis=0)
    up = jnp.concatenate([jnp.where(edge == 0, prev_row, up[0:SUBLANES]), up[SUBLANES:]], axis=0)
    un = jnp.concatenate([un[0:n - SUBLANES], jnp.where(edge == SUBLANES - 1, next_row, un[n - SUBLANES:])], axis=0)
    y = w[0:1, :] * up + w[1:2, :] * u + w[2:3, :] * un + bias
    return y * _sigmoid(y)


def _softplus(v):
    return jnp.maximum(v, 0.0) + jnp.log1p(jnp.exp(-jnp.abs(v)))


def _ssd_dt_la(dt_raw, alog_ref, dtb_ref):
    lane = _iota2((1, LANES), 1)
    live = lane < N_DH
    a_row = jnp.where(live, -jnp.exp(alog_ref[...]) * LOG2E, 0.0)
    dt = jnp.where(live, _softplus(dt_raw + dtb_ref[...]), 0.0)
    return dt, dt * a_row


def _tri(upper):
    r = _iota2((CHUNK, CHUNK), 0)
    c = _iota2((CHUNK, CHUNK), 1)
    return jnp.where((c >= r) if upper else (c <= r), 1.0, 0.0).astype(BF16)


def _ssd_bstate_body(x_ref, xp_ref, xn_ref, bc_ref, bp_ref, bn_ref, dt_ref, cw_ref, cb_ref,
                     alog_ref, dtb_ref, exp_ref, sb_ref, xs_ref, bco_ref, s_ref, *, geom):
    j = geom.bwd_step(pl.program_id(1))
    first = geom.seg_first(j)
    last = geom.seg_last(j)
    hl = SUBLANES - 1
    xs_all = _conv_silu(x_ref[...],
                        jnp.where(first, 0.0, xp_ref[hl:hl + 1, :]), jnp.where(last, 0.0, xn_ref[0:1, :]),
                        cw_ref[:, 0:SSD_W], cb_ref[:, 0:SSD_W])
    bc_all = _conv_silu(bc_ref[...],
                        jnp.where(first, 0.0, bp_ref[hl:hl + 1, :]), jnp.where(last, 0.0, bn_ref[0:1, :]),
                        cw_ref[:, SSD_W:], cb_ref[:, SSD_W:]).astype(BF16)
    xs_ref[...] = xs_all
    bco_ref[...] = bc_all
    dt_all, la_all = _ssd_dt_la(dt_ref[...], alog_ref, dtb_ref)
    lane = _iota2((CHUNK, LANES), 1)
    ex = exp_ref[:, SSD_W:2 * SSD_W]
    gw = SSD_W // SSD_GROUPS
    tri_u = _tri(True)
    for c in reversed(range(CPS)):
        rows = slice(c * CHUNK, (c + 1) * CHUNK)
        sb_ref[c] = s_ref[...]
        rb = jnp.where(lane < N_DH, _unpack3(_dot(tri_u, _pack3(la_all[rows]))), 0.0)
        rbx = _dot(_pack3(rb), ex)
        dtx = _dot(_pack3(dt_all[rows]), ex)
        xw = (xs_all[rows] * dtx * jnp.exp2(rbx[0:1, :] - rbx)).astype(BF16)
        cdec = jnp.exp2(rbx[0:1, :])
        for g in range(SSD_GROUPS):
            gs = slice(g * gw, (g + 1) * gw)
            bg = bc_all[rows, g * SSD_STATE:(g + 1) * SSD_STATE]
            s_ref[g] = s_ref[g] * cdec[:, gs] + _dot_tn(bg, xw[:, gs])


def _ssd_fwd_body(z_ref, xs_ref, bc_ref, dt_ref, sb_ref, alog_ref, dtb_ref, dsk_ref, nw_ref,
                  exp_ref, expl_ref, o_ref, s_ref):
    dt_all, la_all = _ssd_dt_la(dt_ref[...], alog_ref, dtb_ref)
    lane = _iota2((CHUNK, LANES), 1)
    r = _iota2((CHUNK, CHUNK), 0)
    cidx = _iota2((CHUNK, CHUNK), 1)
    lower = r >= cidx
    diag = cidx == r
    lane64 = lane < SSD_HD
    fwd_lane = (lane % N_DH) < SSD_HEADS
    gw = SSD_W // SSD_GROUPS
    hpg = SSD_HEADS // SSD_GROUPS
    tri_l = _tri(False)
    tri_u = _tri(True)
    for c in range(CPS):
        rows = slice(c * CHUNK, (c + 1) * CHUNK)
        xs = xs_ref[rows, :]
        lap = _pack3(la_all[rows])
        cum = jnp.where(lane < N_DH,
                        _unpack3(jnp.where(fwd_lane, _dot(tri_l, lap), _dot(tri_u, lap))), 0.0)
        cum_t = cum.T
        cump = _pack3(cum)
        cumx = _dot(cump, exp_ref[...])
        dt = dt_all[rows]
        dt_t = dt.T
        xdt_f = xs * _dot(_pack3(dt), exp_ref[:, 0:SSD_W])

        cgs = []
        cbs = []
        for g in range(SSD_GROUPS):
            bg = bc_ref[rows, g * SSD_STATE:(g + 1) * SSD_STATE]
            cg = bc_ref[rows, SSD_BC_W + g * SSD_STATE:SSD_BC_W + (g + 1) * SSD_STATE]
            cgs.append((bg, cg))
            cbs.append(_dot_nt(cg, bg))
        col_f = _dot(cump, expl_ref[:, 0:SSD_HEADS * LANES])
        col_b = _dot(cump, expl_ref[:, SSD_HEADS * LANES:])
        y_parts = []
        for hp in range(SSD_HEADS // 2):
            g = (2 * hp) // hpg
            ps = slice(hp * LANES, (hp + 1) * LANES)
            lhs = []
            for e in range(2):
                h = 2 * hp + e
                hb = SSD_HEADS + h
                hs = slice(h * LANES, (h + 1) * LANES)
                dec = jnp.exp2(jnp.where(lower, col_f[:, hs] - cum_t[h:h + 1, :],
                                        col_b[:, hs] - cum_t[hb:hb + 1, :]))
                dtf = dt_t[h:h + 1, :]
                dtb = dt_t[hb:hb + 1, :]
                dts = jnp.where(diag, dtf + dtb, jnp.where(lower, dtf, dtb))
                lhs.append((cbs[g] * dec * dts).astype(BF16))
            xp = xs[:, ps]
            rhs = jnp.concatenate([jnp.where(lane64, xp, 0.0).astype(BF16),
                                   jnp.where(lane64, 0.0, xp).astype(BF16)], axis=0)
            y_parts.append(_dot(jnp.concatenate(lhs, axis=1), rhs))
        y = jnp.concatenate(y_parts, axis=1)

        e_f = jnp.exp2(cumx[:, 0:SSD_W])
        e_b = jnp.exp2(cumx[:, SSD_W:])
        yoff_f = jnp.concatenate([_dot(cgs[g][1], s_ref[g].astype(BF16)) for g in range(SSD_GROUPS)], axis=1)
        yoff_b = jnp.concatenate([_dot(cgs[g][1], sb_ref[c, g].astype(BF16)) for g in range(SSD_GROUPS)], axis=1)
        y = y + yoff_f * e_f + yoff_b * e_b

        tot = cumx[CHUNK - 1:CHUNK, 0:SSD_W]
        xw = (xdt_f * jnp.exp2(tot - cumx[:, 0:SSD_W])).astype(BF16)
        cdec = jnp.exp2(tot)
        for g in range(SSD_GROUPS):
            gs = slice(g * gw, (g + 1) * gw)
            s_ref[g] = s_ref[g] * cdec[:, gs] + _dot_tn(cgs[g][0], xw[:, gs])

        z = z_ref[rows, :]
        y = (y + dsk_ref[...] * xs) * (z * _sigmoid(z))
        o_ref[rows, :] = (_rms(y) * nw_ref[...]).astype(BF16)


def _expand_consts():
    kk = np.arange(N_DH)
    ex = np.zeros((LANES, 2 * SSD_W), np.float32)
    exl = np.zeros((LANES, N_DH * LANES), np.float32)
    for part in range(3):
        for p in range(SSD_HD):
            ex[part * N_DH + kk, kk * SSD_HD + p] = 1.0
        for q in range(LANES):
            exl[part * N_DH + kk, kk * LANES + q] = 1.0
    return jnp.asarray(ex, BF16), jnp.asarray(exl, BF16)


N_RET_BWD_IN = 3
N_RET_FWD_IN = 7


def _scan_bwd_kernel(*refs, geom):
    ret_in, ssd_in = refs[:N_RET_BWD_IN], refs[N_RET_BWD_IN:-6]
    ret_sb, ssd_sb, xs_ref, bco_ref, ret_s, ssd_s = refs[-6:]

    @pl.when(pl.program_id(1) == 0)
    def _():
        ret_s[...] = jnp.zeros_like(ret_s)
        ssd_s[...] = jnp.zeros_like(ssd_s)

    _ret_bstate_body(*ret_in, ret_sb, ret_s)
    _ssd_bstate_body(*ssd_in, ssd_sb, xs_ref, bco_ref, ssd_s, geom=geom)


def _scan_fwd_kernel(*refs):
    ret_in, ssd_in = refs[:N_RET_FWD_IN], refs[N_RET_FWD_IN:-4]
    ret_o, ssd_o, ret_s, ssd_s = refs[-4:]

    @pl.when(pl.program_id(1) == 0)
    def _():
        ret_s[...] = jnp.zeros_like(ret_s)
        ssd_s[...] = jnp.zeros_like(ssd_s)

    _ret_fwd_body(*ret_in, ret_o, ret_s)
    _ssd_fwd_body(*ssd_in, ssd_o, ssd_s)


def _scan_mixers(p, pdt, geom, ld, ret_nw, conv_w, conv_b, alog, dtb, dsk, ssd_nw):
    b, nst = geom.b, geom.nst
    ex, exl = _expand_consts()
    zcol = 3072 // SSD_W
    xcol = 4096 // SSD_W
    bccol = (4096 + SSD_W) // (2 * SSD_BC_W)
    n8 = geom.rows // SUBLANES
    sub = RS // SUBLANES
    gw = SSD_W // SSD_GROUPS
    ret_sb_block = (None, None, CPS, RET_HEADS, RET_DK, RET_DK)
    ssd_sb_block = (None, None, CPS, SSD_GROUPS, SSD_STATE, gw)
    ret_state = pltpu.VMEM((RET_HEADS, RET_DK, RET_DK), F32)
    ssd_state = pltpu.VMEM((SSD_GROUPS, SSD_STATE, gw), F32)

    def const(shape):
        return pl.BlockSpec(shape, lambda bi, s: (0,) * len(shape))

    def rows_of(step_of):
        return lambda bi, s: geom.row_block(bi, step_of(s))

    def ret_blk(col, rb):
        return pl.BlockSpec((RS, RET_W), lambda bi, s: (rb(bi, s), col))

    ld_spec = const((2, RET_HEADS, LANES))

    rb = rows_of(geom.bwd_step)
    prev8 = lambda bi, s: jnp.maximum(rb(bi, s) * sub - 1, 0)
    next8 = lambda bi, s: jnp.minimum(rb(bi, s) * sub + sub, n8 - 1)
    ret_sb, ssd_sb, xs, bcm = pl.pallas_call(
        functools.partial(_scan_bwd_kernel, geom=geom),
        grid=(b, nst),
        in_specs=[ret_blk(1, rb), ret_blk(2, rb), ld_spec,
                  pl.BlockSpec((RS, SSD_W), lambda bi, s: (rb(bi, s), xcol)),
                  pl.BlockSpec((SUBLANES, SSD_W), lambda bi, s: (prev8(bi, s), xcol)),
                  pl.BlockSpec((SUBLANES, SSD_W), lambda bi, s: (next8(bi, s), xcol)),
                  pl.BlockSpec((RS, 2 * SSD_BC_W), lambda bi, s: (rb(bi, s), bccol)),
                  pl.BlockSpec((SUBLANES, 2 * SSD_BC_W), lambda bi, s: (prev8(bi, s), bccol)),
                  pl.BlockSpec((SUBLANES, 2 * SSD_BC_W), lambda bi, s: (next8(bi, s), bccol)),
                  pl.BlockSpec((RS, LANES), lambda bi, s: (rb(bi, s), 0)),
                  const(conv_w.shape), const(conv_b.shape), const(alog.shape), const(dtb.shape),
                  const(ex.shape)],
        out_specs=[pl.BlockSpec(ret_sb_block, lambda bi, s: (bi, geom.bwd_step(s), 0, 0, 0, 0)),
                   pl.BlockSpec(ssd_sb_block, lambda bi, s: (bi, geom.bwd_step(s), 0, 0, 0, 0)),
                   pl.BlockSpec((RS, SSD_W), lambda bi, s: (rb(bi, s), 0)),
                   pl.BlockSpec((RS, 2 * SSD_BC_W), lambda bi, s: (rb(bi, s), 0))],
        out_shape=[jax.ShapeDtypeStruct((b, nst, CPS, RET_HEADS, RET_DK, RET_DK), F32),
                   jax.ShapeDtypeStruct((b, nst, CPS, SSD_GROUPS, SSD_STATE, gw), F32),
                   jax.ShapeDtypeStruct((geom.rows, SSD_W), F32),
                   jax.ShapeDtypeStruct((geom.rows, 2 * SSD_BC_W), BF16)],
        scratch_shapes=[ret_state, ssd_state],
        compiler_params=_cparams(("arbitrary", "arbitrary")),
        name="scan_bwd",
    )(p, p, ld, p, p, p, p, p, p, pdt, conv_w, conv_b, alog, dtb, ex)

    rf = rows_of(lambda s: s)
    return pl.pallas_call(
        _scan_fwd_kernel,
        grid=(b, nst),
        in_specs=[ret_blk(0, rf), ret_blk(1, rf), ret_blk(2, rf), ret_blk(3, rf),
                  pl.BlockSpec(ret_sb_block, lambda bi, s: (bi, s, 0, 0, 0, 0)),
                  ld_spec, const(ret_nw.shape),
                  pl.BlockSpec((RS, SSD_W), lambda bi, s: (rf(bi, s), zcol)),
                  pl.BlockSpec((RS, SSD_W), lambda bi, s: (rf(bi, s), 0)),
                  pl.BlockSpec((RS, 2 * SSD_BC_W), lambda bi, s: (rf(bi, s), 0)),
                  pl.BlockSpec((RS, LANES), lambda bi, s: (rf(bi, s), 0)),
                  pl.BlockSpec(ssd_sb_block, lambda bi, s: (bi, s, 0, 0, 0, 0)),
                  const(alog.shape), const(dtb.shape), const(dsk.shape), const(ssd_nw.shape),
                  const(ex.shape), const(exl.shape)],
        out_specs=[pl.BlockSpec((RS, RET_W), lambda bi, s: (rf(bi, s), 0)),
                   pl.BlockSpec((RS, SSD_W), lambda bi, s: (rf(bi, s), 0))],
        out_shape=[jax.ShapeDtypeStruct((geom.rows, RET_W), BF16),
                   jax.ShapeDtypeStruct((geom.rows, SSD_W), BF16)],
        scratch_shapes=[ret_state, ssd_state],
        compiler_params=_cparams(("arbitrary", "arbitrary")),
        name="scan_fwd",
    )(p, p, p, p, ret_sb, ld, ret_nw, p, xs, bcm, pdt, ssd_sb, alog, dtb, dsk, ssd_nw, ex, exl)


def _rope_tables(t):
    half = ATT_HD // 2
    freqs = ROPE_BASE ** (-jnp.arange(0, half, 2, dtype=F32) / half)
    pos = jnp.arange(t)
    ang_r = (pos // GRID_W).astype(F32)[:, None] * freqs[None, :]
    ang_c = (pos % GRID_W).astype(F32)[:, None] * freqs[None, :]
    cos = jnp.concatenate([jnp.cos(ang_r), jnp.cos(ang_r), jnp.cos(ang_c), jnp.cos(ang_c)], axis=-1)
    sin = jnp.concatenate([-jnp.sin(ang_r), jnp.sin(ang_r), -jnp.sin(ang_c), jnp.sin(ang_c)], axis=-1)
    return cos, sin


def _pad_lanes(v):
    v = v.reshape(1, -1)
    return jnp.pad(v, ((0, 0), (0, LANES - v.shape[1])))


def _mixer(p, pdt, geom, layer, ctx_out, cos, sin, ret_log_decay, ret_norm_w, attn_sink, ssd_conv_w,
           ssd_conv_b, ssd_a_log, ssd_dt_bias, ssd_d, ssd_norm_w):
    ld = jnp.broadcast_to(ret_log_decay[layer][:, :, None], (2, RET_HEADS, LANES))
    ra, ss = _scan_mixers(p, pdt, geom, ld, ret_norm_w[layer].reshape(1, RET_W),
                          ssd_conv_w[layer], ssd_conv_b[layer].reshape(1, -1),
                          _pad_lanes(ssd_a_log[layer]), _pad_lanes(ssd_dt_bias[layer]),
                          jnp.repeat(ssd_d[layer], SSD_HD).reshape(1, SSD_W),
                          ssd_norm_w[layer].reshape(1, SSD_W))
    at_l, at_c = _attention(p, geom, attn_sink[layer], cos, sin, ctx_out)
    return ra, at_l, at_c, ss


def _forward(x, c, ctx, c_ctx, w_ada, b_ada, norm_w, ffn1_gu, ffn1_down, ffn2_gu, ffn2_down,
             w_in, w_out, ret_log_decay, ret_norm_w, attn_sink, ssd_conv_w, ssd_conv_b,
             ssd_a_log, ssd_dt_bias, ssd_d, ssd_norm_w):
    b, t, d = x.shape
    lc = ctx.shape[1]
    depth = w_ada.shape[0]
    geom = _Geom(b, t, lc)
    n_lat = b * t
    tm = math.gcd(512, math.gcd(t, b * lc))

    def mod_row(i):
        r0 = i * tm
        return jnp.where(r0 < n_lat, 1 + r0 // t, 0)

    cond = jnp.concatenate([c_ctx[None, :], c, jnp.zeros((SUBLANES - 1 - b, d), F32)], axis=0)
    mod = _adaln(cond, w_ada, b_ada)
    mod = mod.reshape(SUBLANES, depth, 3, 3, d).transpose(1, 2, 0, 3, 4).reshape(depth * 3, SUBLANES, 3, d)

    w1gu = ffn1_gu.astype(BF16)
    w1d = ffn1_down.astype(BF16)
    w2gu = ffn2_gu.astype(BF16)
    w2d = ffn2_down.astype(BF16)
    wi = w_in.astype(BF16)
    wi_dt = jnp.pad(wi[:, :, IN_MAIN:], ((0, 0), (0, 0), (0, LANES - N_DH)))
    wo = w_out.astype(BF16)

    cos, sin = _rope_tables(t)
    srcs = [x.reshape(n_lat, d), ctx.reshape(b * lc, d)]
    rows_all = n_lat + b * lc
    for layer in range(depth):
        last = layer == depth - 1
        xs = _ffn(srcs, rows_all, mod, norm_w, w1gu, w1d, layer=layer, sub=0, tm=tm, mod_row=mod_row)
        p, pdt = _proj_in(xs, mod, norm_w, wi, wi_dt, layer=layer, tm=tm, mod_row=mod_row)
        ra, at_l, at_c, ss = _mixer(p, pdt, geom, layer, not last, cos, sin, ret_log_decay, ret_norm_w,
                                    attn_sink, ssd_conv_w, ssd_conv_b, ssd_a_log, ssd_dt_bias, ssd_d,
                                    ssd_norm_w)
        at = at_l if last else jnp.concatenate([at_l, at_c], axis=0)
        rows_out = n_lat if last else rows_all
        xs = _proj_out(xs, rows_out, ra, at, ss, wo, mod, norm_w, layer=layer, tm=tm, mod_row=mod_row)
        xs = _ffn([xs], rows_out, mod, norm_w, w2gu, w2d, layer=layer, sub=2, tm=tm, mod_row=mod_row)
        srcs = [xs]
    return xs.reshape(b, t, d)


def kernel(x, c, ctx, c_ctx, w_ada, b_ada, norm_w, ffn1_gu, ffn1_down, ffn2_gu, ffn2_down,
           w_in, w_out, ret_log_decay, ret_norm_w, attn_sink, ssd_conv_w, ssd_conv_b,
           ssd_a_log, ssd_dt_bias, ssd_d, ssd_norm_w):
    return _forward(x, c, ctx, c_ctx, w_ada, b_ada, norm_w, ffn1_gu, ffn1_down, ffn2_gu, ffn2_down,
                    w_in, w_out, ret_log_decay, ret_norm_w, attn_sink, ssd_conv_w, ssd_conv_b,
                    ssd_a_log, ssd_dt_bias, ssd_d, ssd_norm_w)
```

```python
import functools
import math

import jax
import jax.numpy as jnp
import numpy as np
from jax import lax
from jax.experimental import pallas as pl
from jax.experimental.pallas import tpu as pltpu

F32 = jnp.float32
BF16 = jnp.bfloat16

D_MODEL = 2048
GRID_W = 64
RET_HEADS = 4
RET_DK = 128
RET_W = 512
ATT_HEADS = 4
ATT_KV_HEADS = 2
ATT_HD = 128
ATT_W = 512
ATT_KV_W = 256
WINDOW = 128
SSD_HEADS = 16
SSD_HD = 64
SSD_W = 1024
SSD_GROUPS = 2
SSD_STATE = 128
SSD_BC_W = 256
CHUNK = 128
D_FF = 5632
FFN_RES = 0.5
FFN_TF = 512
ROPE_BASE = 10000.0
NORM_EPS = 1e-6
N_MOD = 9
IN_MAIN = 5632
NEG_BIG = -1e30
LOG2E = 1.4426950408889634

CPS = 2
ATT_QB = 2
RS = CHUNK * CPS
LANES = 128
SUBLANES = 8

VMEM_LIMIT = 56 * 1024 * 1024


def _cparams(sem):
    return pltpu.CompilerParams(dimension_semantics=sem, vmem_limit_bytes=VMEM_LIMIT)


def _sigmoid(v):
    return 1.0 / (1.0 + jnp.exp(-v))


def _rms(v):
    return v * lax.rsqrt(jnp.mean(v * v, axis=-1, keepdims=True) + NORM_EPS)


def _dot(a, b):
    return jnp.dot(a, b, preferred_element_type=F32)


def _dot_nt(a, b):
    return lax.dot_general(a, b, (((1,), (1,)), ((), ())), preferred_element_type=F32)


def _dot_tn(a, b):
    return lax.dot_general(a, b, (((0,), (0,)), ((), ())), preferred_element_type=F32)


def _iota2(shape, dim):
    return lax.broadcasted_iota(jnp.int32, shape, dim)


SLAB = 16
SLAB_UNROLL = 4


def _row_slabs(n_rows, body):
    def step(r, carry):
        body(pl.ds(pl.multiple_of(r * SLAB, SLAB), SLAB))
        return carry

    lax.fori_loop(0, n_rows // SLAB, step, 0, unroll=SLAB_UNROLL)


def _modnorm_rows(x_ref, h_ref, nw_row, mod_ref, zero_ref=None):
    gain = nw_row * (1.0 + mod_ref[1:2, :])
    shift = mod_ref[0:1, :]

    def body(rows):
        h_ref[rows, :] = (_rms(x_ref[rows, :]) * gain + shift).astype(BF16)
        if zero_ref is not None:
            zero_ref[rows, :] = jnp.zeros((SLAB, zero_ref.shape[1]), zero_ref.dtype)

    _row_slabs(x_ref.shape[0], body)


def _residual_rows(x_ref, y_ref, gain, o_ref):
    def body(rows):
        o_ref[rows, :] = x_ref[rows, :] + _rms(y_ref[rows, :]) * gain

    _row_slabs(x_ref.shape[0], body)


def _adaln_kernel(c_ref, w_ref, b_ref, o_ref):
    cnd = c_ref[...]
    s = (cnd * _sigmoid(cnd)).astype(BF16)
    o_ref[...] = _dot(s, w_ref[...].astype(BF16)) + b_ref[...]


def _adaln(cond, w_ada, b_ada):
    depth, d, n = w_ada.shape
    tn = 1024
    nt = n // tn
    return pl.pallas_call(
        _adaln_kernel,
        grid=(depth, nt),
        in_specs=[
            pl.BlockSpec((SUBLANES, d), lambda l, j: (0, 0)),
            pl.BlockSpec((None, d, tn), lambda l, j: (l, 0, j)),
            pl.BlockSpec((None, 1, tn), lambda l, j: (l, 0, j)),
        ],
        out_specs=pl.BlockSpec((SUBLANES, tn), lambda l, j: (0, l * nt + j)),
        out_shape=jax.ShapeDtypeStruct((SUBLANES, depth * n), F32),
        compiler_params=_cparams(("arbitrary", "arbitrary")),
        name="adaln",
    )(cond, w_ada, b_ada.reshape(depth, 1, n))


def _ffn_kernel(*refs, sub, split):
    j = pl.program_id(1)
    is_first = j == 0
    is_last = j == pl.num_programs(1) - 1
    if split is None:
        x_ref, mod_ref, nw_ref, wg_ref, wu_ref, wd_ref, o_ref, h_ref, acc_ref = refs
        sources = [(x_ref, None)]
    else:
        xa_ref, xb_ref, mod_ref, nw_ref, wg_ref, wu_ref, wd_ref, o_ref, h_ref, acc_ref = refs
        from_a = pl.program_id(0) < split
        sources = [(xa_ref, from_a), (xb_ref, jnp.logical_not(from_a))]

    def prologue(x_ref):
        _modnorm_rows(x_ref, h_ref, nw_ref[2 * sub:2 * sub + 1, :], mod_ref, zero_ref=acc_ref)

    def epilogue(x_ref):
        gain = (FFN_RES * mod_ref[2:3, :]) * nw_ref[2 * sub + 1:2 * sub + 2, :]
        _residual_rows(x_ref, acc_ref, gain, o_ref)

    for x_ref, active in sources:
        pl.when(is_first if active is None else is_first & active)(functools.partial(prologue, x_ref))

    h = h_ref[...]
    g = _dot(h, wg_ref[...])
    u = _dot(h, wu_ref[...])
    a = (g * _sigmoid(g) * u).astype(BF16)
    acc_ref[...] += _dot(a, wd_ref[...])

    for x_ref, active in sources:
        pl.when(is_last if active is None else is_last & active)(functools.partial(epilogue, x_ref))


def _ffn(srcs, rows_out, mod, nw, w_gu, w_down, *, layer, sub, tm, mod_row):
    d = srcs[0].shape[1]
    tf = FFN_TF
    nf = D_FF // tf
    if len(srcs) == 1:
        split = None
        x_specs = [pl.BlockSpec((tm, d), lambda i, j: (i, 0))]
    else:
        split = srcs[0].shape[0] // tm
        x_specs = [pl.BlockSpec((tm, d), lambda i, j: (jnp.minimum(i, split - 1), 0)),
                   pl.BlockSpec((tm, d), lambda i, j: (jnp.maximum(i - split, 0), 0))]
    return pl.pallas_call(
        functools.partial(_ffn_kernel, sub=sub, split=split),
        grid=(rows_out // tm, nf),
        in_specs=x_specs + [
            pl.BlockSpec((None, None, 3, d), lambda i, j: (layer * 3 + sub, mod_row(i), 0, 0)),
            pl.BlockSpec((None, 6, d), lambda i, j: (layer, 0, 0)),
            pl.BlockSpec((None, d, tf), lambda i, j: (layer, 0, j)),
            pl.BlockSpec((None, d, tf), lambda i, j: (layer, 0, j + nf)),
            pl.BlockSpec((None, tf, d), lambda i, j: (layer, j, 0)),
        ],
        out_specs=pl.BlockSpec((tm, d), lambda i, j: (i, 0)),
        out_shape=jax.ShapeDtypeStruct((rows_out, d), F32),
        scratch_shapes=[pltpu.VMEM((tm, d), BF16), pltpu.VMEM((tm, d), F32)],
        compiler_params=_cparams(("arbitrary", "arbitrary")),
        name=f"ffn{sub}",
    )(*srcs, mod, nw, w_gu, w_gu, w_down)


def _proj_in_kernel(x_ref, mod_ref, nw_ref, w_ref, wdt_ref, o_ref, odt_ref, h_ref, *, tn):
    j = pl.program_id(1)

    @pl.when(j == 0)
    def _():
        _modnorm_rows(x_ref, h_ref, nw_ref[2:3, :], mod_ref)
        odt_ref[...] = _dot(h_ref[...], wdt_ref[...])

    def project(jj):
        o_ref[...] = _dot(h_ref[...], w_ref[:, jj * tn:(jj + 1) * tn])

    for jj in range(IN_MAIN // tn):
        pl.when(j == jj)(functools.partial(project, jj))


def _proj_in(xs, mod, nw, w_in, w_dt, *, layer, tm, mod_row):
    rows, d = xs.shape
    n = IN_MAIN
    tn = n // 2
    nfull = w_in.shape[2]
    return pl.pallas_call(
        functools.partial(_proj_in_kernel, tn=tn),
        grid=(rows // tm, n // tn),
        in_specs=[
            pl.BlockSpec((tm, d), lambda i, j: (i, 0)),
            pl.BlockSpec((None, None, 3, d), lambda i, j: (layer * 3 + 1, mod_row(i), 0, 0)),
            pl.BlockSpec((None, 6, d), lambda i, j: (layer, 0, 0)),
            pl.BlockSpec((None, d, nfull), lambda i, j: (layer, 0, 0), pipeline_mode=pl.Buffered(1)),
            pl.BlockSpec((None, d, LANES), lambda i, j: (layer, 0, 0)),
        ],
        out_specs=[pl.BlockSpec((tm, tn), lambda i, j: (i, j)),
                   pl.BlockSpec((tm, LANES), lambda i, j: (i, 0))],
        out_shape=[jax.ShapeDtypeStruct((rows, n), F32), jax.ShapeDtypeStruct((rows, LANES), F32)],
        scratch_shapes=[pltpu.VMEM((tm, d), BF16)],
        compiler_params=_cparams(("arbitrary", "arbitrary")),
        name="proj_in",
    )(xs, mod, nw, w_in, w_dt)


def _proj_out_kernel(x_ref, ra_ref, at_ref, ss_ref, w_ref, mod_ref, nw_ref, o_ref):
    y = _dot(ra_ref[...], w_ref[0:RET_W, :])
    y = y + _dot(at_ref[...], w_ref[RET_W:RET_W + ATT_W, :])
    y = y + _dot(ss_ref[...], w_ref[RET_W + ATT_W:, :])
    o_ref[...] = x_ref[...] + _rms(y) * (mod_ref[2:3, :] * nw_ref[3:4, :])


def _proj_out(xs, rows_out, ra, at, ss, w_out, mod, nw, *, layer, tm, mod_row):
    d = xs.shape[1]
    return pl.pallas_call(
        _proj_out_kernel,
        grid=(rows_out // tm,),
        in_specs=[
            pl.BlockSpec((tm, d), lambda i: (i, 0)),
            pl.BlockSpec((tm, RET_W), lambda i: (i, 0)),
            pl.BlockSpec((tm, ATT_W), lambda i: (i, 0)),
            pl.BlockSpec((tm, SSD_W), lambda i: (i, 0)),
            pl.BlockSpec((None, d, d), lambda i: (layer, 0, 0)),
            pl.BlockSpec((None, None, 3, d), lambda i: (layer * 3 + 1, mod_row(i), 0, 0)),
            pl.BlockSpec((None, 6, d), lambda i: (layer, 0, 0)),
        ],
        out_specs=pl.BlockSpec((tm, d), lambda i: (i, 0)),
        out_shape=jax.ShapeDtypeStruct((rows_out, d), F32),
        compiler_params=_cparams(("arbitrary",)),
        name="proj_out",
    )(xs, ra, at, ss, w_out, mod, nw)


class _Geom:
    def __init__(self, b, t, lc):
        assert t % RS == 0 and lc % RS == 0
        self.b, self.t, self.lc = b, t, lc
        self.nlat = t // RS
        self.nctx = lc // RS
        self.nst = self.nlat + self.nctx
        self.rows = b * (t + lc)

    def row_block(self, bi, j):
        return jnp.where(j < self.nctx,
                         self.b * self.nlat + bi * self.nctx + j,
                         bi * self.nlat + (j - self.nctx))

    def bwd_step(self, s):
        return jnp.where(s < self.nctx, self.nctx - 1 - s, self.nst + self.nctx - 1 - s)

    def seg_first(self, j):
        return (j == 0) | (j == self.nctx)

    def seg_last(self, j):
        return (j == self.nctx - 1) | (j == self.nst - 1)


def _ret_bstate_body(k_ref, v_ref, ld_ref, sb_ref, s_ref):
    jrow = _iota2((CHUNK, CHUNK), 0).astype(F32)
    for h in range(RET_HEADS):
        hs = slice(h * RET_DK, (h + 1) * RET_DK)
        lgb = -jnp.abs(ld_ref[1, h:h + 1, :])
        kdec = jnp.exp(lgb * jrow)
        cdec = jnp.exp(lgb * float(CHUNK))
        for c in reversed(range(CPS)):
            rows = slice(c * CHUNK, (c + 1) * CHUNK)
            sb_ref[c, h] = s_ref[h]
            kd = (k_ref[rows, hs] * kdec).astype(BF16)
            s_ref[h] = s_ref[h] * cdec + _dot_tn(kd, v_ref[rows, hs].astype(BF16))


def _ret_fwd_body(q_ref, k_ref, v_ref, g_ref, sb_ref, ld_ref, nw_ref, o_ref, s_ref):
    irow = _iota2((CHUNK, CHUNK), 0).astype(F32)
    rel = irow - _iota2((CHUNK, CHUNK), 1).astype(F32)
    for h in range(RET_HEADS):
        hs = slice(h * RET_DK, (h + 1) * RET_DK)
        lgf = -jnp.abs(ld_ref[0, h:h + 1, :])
        lgb = -jnp.abs(ld_ref[1, h:h + 1, :])
        dmask = (jnp.where(rel >= 0, jnp.exp(lgf * jnp.maximum(rel, 0.0)), 0.0)
                 + jnp.where(rel <= 0, jnp.exp(lgb * jnp.maximum(-rel, 0.0)), 0.0))
        qdec_f = jnp.exp(lgf * (irow + 1.0))
        qdec_b = jnp.exp(lgb * (float(CHUNK) - irow))
        kdec_f = jnp.exp(lgf * (float(CHUNK) - 1.0 - irow))
        cdec_f = jnp.exp(lgf * float(CHUNK))
        for c in range(CPS):
            rows = slice(c * CHUNK, (c + 1) * CHUNK)
            q = q_ref[rows, hs] * (RET_DK ** -0.5)
            k = k_ref[rows, hs]
            vb = v_ref[rows, hs].astype(BF16)
            inner = _dot_nt(q.astype(BF16), k.astype(BF16)) * dmask
            y = _dot(inner.astype(BF16), vb)
            y = y + _dot((q * qdec_f).astype(BF16), s_ref[h].astype(BF16))
            y = y + _dot((q * qdec_b).astype(BF16), sb_ref[c, h].astype(BF16))
            s_ref[h] = s_ref[h] * cdec_f + _dot_tn((k * kdec_f).astype(BF16), vb)
            mu = jnp.mean(y, axis=-1, keepdims=True)
            yc = y - mu
            var = jnp.mean(yc * yc, axis=-1, keepdims=True)
            gate = g_ref[rows, hs]
            yn = yc * lax.rsqrt(var + NORM_EPS) * nw_ref[:, hs]
            o_ref[rows, hs] = (yn * (gate * _sigmoid(gate))).astype(BF16)


def _rope(x, cos, sin):
    lane = _iota2(x.shape, 1)
    swapped = jnp.where((lane // 32) % 2 == 0, pltpu.roll(x, 96, axis=1), pltpu.roll(x, 32, axis=1))
    return x * cos + swapped * sin


def _att_body(q_ref, kl_ref, vl_ref, kc_ref, vc_ref, cos_ref, sin_ref, sink_ref, o_ref, kr_ref, vb_ref, *, geom):
    s = pl.program_id(1)
    j = geom.bwd_step(s)
    pl.when(s == 0)(functools.partial(_att_prepare, kl_ref, vl_ref, cos_ref, sin_ref, kr_ref, vb_ref, geom.t))
    pl.when(j < geom.nctx)(functools.partial(_att_ctx_rows, q_ref, kc_ref, vc_ref, sink_ref, o_ref))
    pl.when(j >= geom.nctx)(functools.partial(_att_latent_rows, j - geom.nctx, geom.t // CHUNK, q_ref, kc_ref,
                                              vc_ref, cos_ref, sin_ref, sink_ref, o_ref, kr_ref, vb_ref))


def _att_prepare(kl_ref, vl_ref, cos_ref, sin_ref, kr_ref, vb_ref, t):
    blk = CHUNK
    zeros = jnp.zeros((blk, ATT_KV_W), BF16)
    kr_ref[0:blk, :] = zeros
    kr_ref[t + blk:t + 2 * blk, :] = zeros
    vb_ref[0:blk, :] = zeros
    vb_ref[t + blk:t + 2 * blk, :] = zeros
    for h in range(ATT_KV_HEADS):
        hs = slice(h * ATT_HD, (h + 1) * ATT_HD)
        kr_ref[blk:t + blk, hs] = _rope(kl_ref[:, hs], cos_ref[...], sin_ref[...]).astype(BF16)
    vb_ref[blk:t + blk, :] = vl_ref[...].astype(BF16)


def _att_latent_rows(step, nb, q_ref, kc_ref, vc_ref, cos_ref, sin_ref, sink_ref, o_ref, kr_ref, vb_ref):
    blk = CHUNK
    scale = ATT_HD ** -0.5
    qi = _iota2((2 * blk, 3 * blk), 0) % blk
    kj = _iota2((2 * blk, 3 * blk), 1)
    in_window = jnp.abs(kj - blk - qi) <= WINDOW
    first_head = _iota2((2 * blk, 1), 0) < blk

    for qb in range(ATT_QB):
        n = step * ATT_QB + qb
        qs = slice(qb * blk, (qb + 1) * blk)
        row0 = pl.multiple_of(n * blk, blk)
        cos = cos_ref[pl.ds(row0, blk), :]
        sin = sin_ref[pl.ds(row0, blk), :]
        valid = in_window & ((kj >= blk) | (n > 0)) & ((kj < 2 * blk) | (n < nb - 1))
        for h in range(ATT_KV_HEADS):
            hs = slice(h * ATT_HD, (h + 1) * ATT_HD)
            q0 = _rope(q_ref[qs, 2 * h * ATT_HD:(2 * h + 1) * ATT_HD], cos, sin)
            q1 = _rope(q_ref[qs, (2 * h + 1) * ATT_HD:(2 * h + 2) * ATT_HD], cos, sin)
            q2 = jnp.concatenate([q0, q1], axis=0).astype(BF16)
            kw = kr_ref[pl.ds(row0, 3 * blk), hs]
            vw = vb_ref[pl.ds(row0, 3 * blk), hs]
            s_loc = jnp.where(valid, _dot_nt(q2, kw) * scale, NEG_BIG)
            s_cx = _dot_nt(q2, kc_ref[:, hs].astype(BF16)) * scale
            sink = jnp.where(first_head, sink_ref[2 * h], sink_ref[2 * h + 1])
            m = jnp.maximum(jnp.maximum(jnp.max(s_loc, axis=-1, keepdims=True),
                                        jnp.max(s_cx, axis=-1, keepdims=True)), sink)
            e_loc = jnp.exp(s_loc - m)
            e_cx = jnp.exp(s_cx - m)
            den = (jnp.sum(e_loc, axis=-1, keepdims=True) + jnp.sum(e_cx, axis=-1, keepdims=True)
                   + jnp.exp(sink - m))
            o = _dot(e_loc.astype(BF16), vw) + _dot(e_cx.astype(BF16), vc_ref[:, hs].astype(BF16))
            o = o / den
            o_ref[qs, 2 * h * ATT_HD:(2 * h + 1) * ATT_HD] = o[0:blk].astype(BF16)
            o_ref[qs, (2 * h + 1) * ATT_HD:(2 * h + 2) * ATT_HD] = o[blk:2 * blk].astype(BF16)


def _att_ctx_rows(q_ref, kc_ref, vc_ref, sink_ref, o_ref):
    lc = q_ref.shape[0]
    scale = ATT_HD ** -0.5
    first_head = _iota2((2 * lc, 1), 0) < lc
    for h in range(ATT_KV_HEADS):
        hs = slice(h * ATT_HD, (h + 1) * ATT_HD)
        q2 = jnp.concatenate([q_ref[:, 2 * h * ATT_HD:(2 * h + 1) * ATT_HD],
                              q_ref[:, (2 * h + 1) * ATT_HD:(2 * h + 2) * ATT_HD]], axis=0).astype(BF16)
        s = _dot_nt(q2, kc_ref[:, hs].astype(BF16)) * scale
        sink = jnp.where(first_head, sink_ref[2 * h], sink_ref[2 * h + 1])
        m = jnp.maximum(jnp.max(s, axis=-1, keepdims=True), sink)
        e = jnp.exp(s - m)
        den = jnp.sum(e, axis=-1, keepdims=True) + jnp.exp(sink - m)
        o = _dot(e.astype(BF16), vc_ref[:, hs].astype(BF16)) / den
        o_ref[:, 2 * h * ATT_HD:(2 * h + 1) * ATT_HD] = o[0:lc].astype(BF16)
        o_ref[:, (2 * h + 1) * ATT_HD:(2 * h + 2) * ATT_HD] = o[lc:2 * lc].astype(BF16)


N_DH = 2 * SSD_HEADS


def _pack3(v):
    hi = v.astype(BF16).astype(F32)
    r = v - hi
    mid = r.astype(BF16).astype(F32)
    lo = r - mid
    return (hi + pltpu.roll(mid, N_DH, axis=1) + pltpu.roll(lo, 2 * N_DH, axis=1)).astype(BF16)


def _unpack3(r):
    return r + pltpu.roll(r, LANES - N_DH, axis=1) + pltpu.roll(r, LANES - 2 * N_DH, axis=1)


def _conv_silu(u, prev_row, next_row, w, bias):
    n = u.shape[0]
    edge = _iota2((SUBLANES, u.shape[1]), 0)
    up = pltpu.roll(u, 1, axis=0)
    un = pltpu.roll(u, n - 1, axis=0)
    up = jnp.concatenate([jnp.where(edge == 0, prev_row, up[0:SUBLANES]), up[SUBLANES:]], axis=0)
    un = jnp.concatenate([un[0:n - SUBLANES], jnp.where(edge == SUBLANES - 1, next_row, un[n - SUBLANES:])], axis=0)
    y = w[0:1, :] * up + w[1:2, :] * u + w[2:3, :] * un + bias
    return y * _sigmoid(y)


def _softplus(v):
    return jnp.maximum(v, 0.0) + jnp.log1p(jnp.exp(-jnp.abs(v)))


def _ssd_dt_la(dt_raw, alog_ref, dtb_ref):
    lane = _iota2((1, LANES), 1)
    live = lane < N_DH
    a_row = jnp.where(live, -jnp.exp(alog_ref[...]) * LOG2E, 0.0)
    dt = jnp.where(live, _softplus(dt_raw + dtb_ref[...]), 0.0)
    return dt, dt * a_row


def _tri(upper):
    r = _iota2((CHUNK, CHUNK), 0)
    c = _iota2((CHUNK, CHUNK), 1)
    return jnp.where((c >= r) if upper else (c <= r), 1.0, 0.0).astype(BF16)


def _ssd_bstate_body(x_ref, xp_ref, xn_ref, bc_ref, bp_ref, bn_ref, dt_ref, cw_ref, cb_ref,
                     alog_ref, dtb_ref, exp_ref, sb_ref, xs_ref, bco_ref, s_ref, *, geom):
    j = geom.bwd_step(pl.program_id(1))
    first = geom.seg_first(j)
    last = geom.seg_last(j)
    hl = SUBLANES - 1
    xs_all = _conv_silu(x_ref[...],
                        jnp.where(first, 0.0, xp_ref[hl:hl + 1, :]), jnp.where(last, 0.0, xn_ref[0:1, :]),
                        cw_ref[:, 0:SSD_W], cb_ref[:, 0:SSD_W])
    bc_all = _conv_silu(bc_ref[...],
                        jnp.where(first, 0.0, bp_ref[hl:hl + 1, :]), jnp.where(last, 0.0, bn_ref[0:1, :]),
                        cw_ref[:, SSD_W:], cb_ref[:, SSD_W:]).astype(BF16)
    xs_ref[...] = xs_all
    bco_ref[...] = bc_all
    dt_all, la_all = _ssd_dt_la(dt_ref[...], alog_ref, dtb_ref)
    lane = _iota2((CHUNK, LANES), 1)
    ex = exp_ref[:, SSD_W:2 * SSD_W]
    gw = SSD_W // SSD_GROUPS
    tri_u = _tri(True)
    for c in reversed(range(CPS)):
        rows = slice(c * CHUNK, (c + 1) * CHUNK)
        sb_ref[c] = s_ref[...]
        rb = jnp.where(lane < N_DH, _unpack3(_dot(tri_u, _pack3(la_all[rows]))), 0.0)
        rbx = _dot(_pack3(rb), ex)
        dtx = _dot(_pack3(dt_all[rows]), ex)
        xw = (xs_all[rows] * dtx * jnp.exp2(rbx[0:1, :] - rbx)).astype(BF16)
        cdec = jnp.exp2(rbx[0:1, :])
        for g in range(SSD_GROUPS):
            gs = slice(g * gw, (g + 1) * gw)
            bg = bc_all[rows, g * SSD_STATE:(g + 1) * SSD_STATE]
            s_ref[g] = s_ref[g] * cdec[:, gs] + _dot_tn(bg, xw[:, gs])


def _ssd_fwd_body(z_ref, xs_ref, bc_ref, dt_ref, sb_ref, alog_ref, dtb_ref, dsk_ref, nw_ref,
                  exp_ref, expl_ref, o_ref, s_ref):
    dt_all, la_all = _ssd_dt_la(dt_ref[...], alog_ref, dtb_ref)
    lane = _iota2((CHUNK, LANES), 1)
    r = _iota2((CHUNK, CHUNK), 0)
    cidx = _iota2((CHUNK, CHUNK), 1)
    lower = r >= cidx
    diag = cidx == r
    lane64 = lane < SSD_HD
    fwd_lane = (lane % N_DH) < SSD_HEADS
    gw = SSD_W // SSD_GROUPS
    hpg = SSD_HEADS // SSD_GROUPS
    tri_l = _tri(False)
    tri_u = _tri(True)
    for c in range(CPS):
        rows = slice(c * CHUNK, (c + 1) * CHUNK)
        xs = xs_ref[rows, :]
        lap = _pack3(la_all[rows])
        cum = jnp.where(lane < N_DH,
                        _unpack3(jnp.where(fwd_lane, _dot(tri_l, lap), _dot(tri_u, lap))), 0.0)
        cum_t = cum.T
        cump = _pack3(cum)
        cumx = _dot(cump, exp_ref[...])
        dt = dt_all[rows]
        dt_t = dt.T
        xdt_f = xs * _dot(_pack3(dt), exp_ref[:, 0:SSD_W])

        cgs = []
        cbs = []
        for g in range(SSD_GROUPS):
            bg = bc_ref[rows, g * SSD_STATE:(g + 1) * SSD_STATE]
            cg = bc_ref[rows, SSD_BC_W + g * SSD_STATE:SSD_BC_W + (g + 1) * SSD_STATE]
            cgs.append((bg, cg))
            cbs.append(_dot_nt(cg, bg))
        col_f = _dot(cump, expl_ref[:, 0:SSD_HEADS * LANES])
        col_b = _dot(cump, expl_ref[:, SSD_HEADS * LANES:])
        y_parts = []
        for hp in range(SSD_HEADS // 2):
            g = (2 * hp) // hpg
            ps = slice(hp * LANES, (hp + 1) * LANES)
            lhs = []
            for e in range(2):
                h = 2 * hp + e
                hb = SSD_HEADS + h
                hs = slice(h * LANES, (h + 1) * LANES)
                dec = jnp.exp2(jnp.where(lower, col_f[:, hs] - cum_t[h:h + 1, :],
                                        col_b[:, hs] - cum_t[hb:hb + 1, :]))
                dtf = dt_t[h:h + 1, :]
                dtb = dt_t[hb:hb + 1, :]
                dts = jnp.where(diag, dtf + dtb, jnp.where(lower, dtf, dtb))
                lhs.append((cbs[g] * dec * dts).astype(BF16))
            xp = xs[:, ps]
            rhs = jnp.concatenate([jnp.where(lane64, xp, 0.0).astype(BF16),
                                   jnp.where(lane64, 0.0, xp).astype(BF16)], axis=0)
            y_parts.append(_dot(jnp.concatenate(lhs, axis=1), rhs))
        y = jnp.concatenate(y_parts, axis=1)

        e_f = jnp.exp2(cumx[:, 0:SSD_W])
        e_b = jnp.exp2(cumx[:, SSD_W:])
        yoff_f = jnp.concatenate([_dot(cgs[g][1], s_ref[g].astype(BF16)) for g in range(SSD_GROUPS)], axis=1)
        yoff_b = jnp.concatenate([_dot(cgs[g][1], sb_ref[c, g].astype(BF16)) for g in range(SSD_GROUPS)], axis=1)
        y = y + yoff_f * e_f + yoff_b * e_b

        tot = cumx[CHUNK - 1:CHUNK, 0:SSD_W]
        xw = (xdt_f * jnp.exp2(tot - cumx[:, 0:SSD_W])).astype(BF16)
        cdec = jnp.exp2(tot)
        for g in range(SSD_GROUPS):
            gs = slice(g * gw, (g + 1) * gw)
            s_ref[g] = s_ref[g] * cdec[:, gs] + _dot_tn(cgs[g][0], xw[:, gs])

        z = z_ref[rows, :]
        y = (y + dsk_ref[...] * xs) * (z * _sigmoid(z))
        o_ref[rows, :] = (_rms(y) * nw_ref[...]).astype(BF16)


def _expand_consts():
    kk = np.arange(N_DH)
    ex = np.zeros((LANES, 2 * SSD_W), np.float32)
    exl = np.zeros((LANES, N_DH * LANES), np.float32)
    for part in range(3):
        for p in range(SSD_HD):
            ex[part * N_DH + kk, kk * SSD_HD + p] = 1.0
        for q in range(LANES):
            exl[part * N_DH + kk, kk * LANES + q] = 1.0
    return jnp.asarray(ex, BF16), jnp.asarray(exl, BF16)


N_RET_BWD_IN = 3
N_SSD_BWD_IN = 12
N_RET_FWD_IN = 7


def _scan_bwd_kernel(*refs, geom):
    ret_in = refs[:N_RET_BWD_IN]
    ssd_in = refs[N_RET_BWD_IN:N_RET_BWD_IN + N_SSD_BWD_IN]
    att_in = refs[N_RET_BWD_IN + N_SSD_BWD_IN:-9]
    ret_sb, ssd_sb, xs_ref, bco_ref, att_o, ret_s, ssd_s, kr_ref, vb_ref = refs[-9:]

    @pl.when(pl.program_id(1) == 0)
    def _():
        ret_s[...] = jnp.zeros_like(ret_s)
        ssd_s[...] = jnp.zeros_like(ssd_s)

    _ret_bstate_body(*ret_in, ret_sb, ret_s)
    _ssd_bstate_body(*ssd_in, ssd_sb, xs_ref, bco_ref, ssd_s, geom=geom)
    _att_body(*att_in, att_o, kr_ref, vb_ref, geom=geom)


def _scan_fwd_kernel(*refs):
    ret_in, ssd_in = refs[:N_RET_FWD_IN], refs[N_RET_FWD_IN:-4]
    ret_o, ssd_o, ret_s, ssd_s = refs[-4:]

    @pl.when(pl.program_id(1) == 0)
    def _():
        ret_s[...] = jnp.zeros_like(ret_s)
        ssd_s[...] = jnp.zeros_like(ssd_s)

    _ret_fwd_body(*ret_in, ret_o, ret_s)
    _ssd_fwd_body(*ssd_in, ssd_o, ssd_s)


def _scan_mixers(p, pdt, geom, ld, ret_nw, sink, cos, sin, conv_w, conv_b, alog, dtb, dsk, ssd_nw):
    assert ATT_QB == CPS
    b, nst, t, lc = geom.b, geom.nst, geom.t, geom.lc
    ctx0 = (b * t) // lc
    qcol = 2048 // ATT_W
    kcol = (2048 + ATT_W) // ATT_KV_W
    vcol = kcol + 1
    ex, exl = _expand_consts()
    zcol = 3072 // SSD_W
    xcol = 4096 // SSD_W
    bccol = (4096 + SSD_W) // (2 * SSD_BC_W)
    n8 = geom.rows // SUBLANES
    sub = RS // SUBLANES
    gw = SSD_W // SSD_GROUPS
    ret_sb_block = (None, None, CPS, RET_HEADS, RET_DK, RET_DK)
    ssd_sb_block = (None, None, CPS, SSD_GROUPS, SSD_STATE, gw)
    ret_state = pltpu.VMEM((RET_HEADS, RET_DK, RET_DK), F32)
    ssd_state = pltpu.VMEM((SSD_GROUPS, SSD_STATE, gw), F32)

    def const(shape):
        return pl.BlockSpec(shape, lambda bi, s: (0,) * len(shape))

    def rows_of(step_of):
        return lambda bi, s: geom.row_block(bi, step_of(s))

    def ret_blk(col, rb):
        return pl.BlockSpec((RS, RET_W), lambda bi, s: (rb(bi, s), col))

    ld_spec = const((2, RET_HEADS, LANES))

    rb = rows_of(geom.bwd_step)
    prev8 = lambda bi, s: jnp.maximum(rb(bi, s) * sub - 1, 0)
    next8 = lambda bi, s: jnp.minimum(rb(bi, s) * sub + sub, n8 - 1)
    ret_sb, ssd_sb, xs, bcm, at = pl.pallas_call(
        functools.partial(_scan_bwd_kernel, geom=geom),
        grid=(b, nst),
        in_specs=[ret_blk(1, rb), ret_blk(2, rb), ld_spec,
                  pl.BlockSpec((RS, SSD_W), lambda bi, s: (rb(bi, s), xcol)),
                  pl.BlockSpec((SUBLANES, SSD_W), lambda bi, s: (prev8(bi, s), xcol)),
                  pl.BlockSpec((SUBLANES, SSD_W), lambda bi, s: (next8(bi, s), xcol)),
                  pl.BlockSpec((RS, 2 * SSD_BC_W), lambda bi, s: (rb(bi, s), bccol)),
                  pl.BlockSpec((SUBLANES, 2 * SSD_BC_W), lambda bi, s: (prev8(bi, s), bccol)),
                  pl.BlockSpec((SUBLANES, 2 * SSD_BC_W), lambda bi, s: (next8(bi, s), bccol)),
                  pl.BlockSpec((RS, LANES), lambda bi, s: (rb(bi, s), 0)),
                  const(conv_w.shape), const(conv_b.shape), const(alog.shape), const(dtb.shape),
                  const(ex.shape),
                  pl.BlockSpec((RS, ATT_W), lambda bi, s: (rb(bi, s), qcol)),
                  pl.BlockSpec((t, ATT_KV_W), lambda bi, s: (bi, kcol)),
                  pl.BlockSpec((t, ATT_KV_W), lambda bi, s: (bi, vcol)),
                  pl.BlockSpec((lc, ATT_KV_W), lambda bi, s: (ctx0 + bi, kcol)),
                  pl.BlockSpec((lc, ATT_KV_W), lambda bi, s: (ctx0 + bi, vcol)),
                  const(cos.shape), const(sin.shape),
                  pl.BlockSpec(memory_space=pltpu.SMEM)],
        out_specs=[pl.BlockSpec(ret_sb_block, lambda bi, s: (bi, geom.bwd_step(s), 0, 0, 0, 0)),
                   pl.BlockSpec(ssd_sb_block, lambda bi, s: (bi, geom.bwd_step(s), 0, 0, 0, 0)),
                   pl.BlockSpec((RS, SSD_W), lambda bi, s: (rb(bi, s), 0)),
                   pl.BlockSpec((RS, 2 * SSD_BC_W), lambda bi, s: (rb(bi, s), 0)),
                   pl.BlockSpec((RS, ATT_W), lambda bi, s: (rb(bi, s), 0))],
        out_shape=[jax.ShapeDtypeStruct((b, nst, CPS, RET_HEADS, RET_DK, RET_DK), F32),
                   jax.ShapeDtypeStruct((b, nst, CPS, SSD_GROUPS, SSD_STATE, gw), F32),
                   jax.ShapeDtypeStruct((geom.rows, SSD_W), F32),
                   jax.ShapeDtypeStruct((geom.rows, 2 * SSD_BC_W), BF16),
                   jax.ShapeDtypeStruct((geom.rows, ATT_W), BF16)],
        scratch_shapes=[ret_state, ssd_state,
                        pltpu.VMEM((t + 2 * CHUNK, ATT_KV_W), BF16), pltpu.VMEM((t + 2 * CHUNK, ATT_KV_W), BF16)],
        compiler_params=_cparams(("arbitrary", "arbitrary")),
        name="scan_bwd",
    )(p, p, ld, p, p, p, p, p, p, pdt, conv_w, conv_b, alog, dtb, ex, p, p, p, p, p, cos, sin, sink)

    rf = rows_of(lambda s: s)
    ra, ss = pl.pallas_call(
        _scan_fwd_kernel,
        grid=(b, nst),
        in_specs=[ret_blk(0, rf), ret_blk(1, rf), ret_blk(2, rf), ret_blk(3, rf),
                  pl.BlockSpec(ret_sb_block, lambda bi, s: (bi, s, 0, 0, 0, 0)),
                  ld_spec, const(ret_nw.shape),
                  pl.BlockSpec((RS, SSD_W), lambda bi, s: (rf(bi, s), zcol)),
                  pl.BlockSpec((RS, SSD_W), lambda bi, s: (rf(bi, s), 0)),
                  pl.BlockSpec((RS, 2 * SSD_BC_W), lambda bi, s: (rf(bi, s), 0)),
                  pl.BlockSpec((RS, LANES), lambda bi, s: (rf(bi, s), 0)),
                  pl.BlockSpec(ssd_sb_block, lambda bi, s: (bi, s, 0, 0, 0, 0)),
                  const(alog.shape), const(dtb.shape), const(dsk.shape), const(ssd_nw.shape),
                  const(ex.shape), const(exl.shape)],
        out_specs=[pl.BlockSpec((RS, RET_W), lambda bi, s: (rf(bi, s), 0)),
                   pl.BlockSpec((RS, SSD_W), lambda bi, s: (rf(bi, s), 0))],
        out_shape=[jax.ShapeDtypeStruct((geom.rows, RET_W), BF16),
                   jax.ShapeDtypeStruct((geom.rows, SSD_W), BF16)],
        scratch_shapes=[ret_state, ssd_state],
        compiler_params=_cparams(("arbitrary", "arbitrary")),
        name="scan_fwd",
    )(p, p, p, p, ret_sb, ld, ret_nw, p, xs, bcm, pdt, ssd_sb, alog, dtb, dsk, ssd_nw, ex, exl)
    return ra, at, ss


def _rope_tables(t):
    half = ATT_HD // 2
    freqs = ROPE_BASE ** (-jnp.arange(0, half, 2, dtype=F32) / half)
    pos = jnp.arange(t)
    ang_r = (pos // GRID_W).astype(F32)[:, None] * freqs[None, :]
    ang_c = (pos % GRID_W).astype(F32)[:, None] * freqs[None, :]
    cos = jnp.concatenate([jnp.cos(ang_r), jnp.cos(ang_r), jnp.cos(ang_c), jnp.cos(ang_c)], axis=-1)
    sin = jnp.concatenate([-jnp.sin(ang_r), jnp.sin(ang_r), -jnp.sin(ang_c), jnp.sin(ang_c)], axis=-1)
    return cos, sin


def _pad_lanes(v):
    v = v.reshape(1, -1)
    return jnp.pad(v, ((0, 0), (0, LANES - v.shape[1])))


def _mixer(p, pdt, geom, layer, cos, sin, ret_log_decay, ret_norm_w, attn_sink, ssd_conv_w,
           ssd_conv_b, ssd_a_log, ssd_dt_bias, ssd_d, ssd_norm_w):
    ld = jnp.broadcast_to(ret_log_decay[layer][:, :, None], (2, RET_HEADS, LANES))
    return _scan_mixers(p, pdt, geom, ld, ret_norm_w[layer].reshape(1, RET_W), attn_sink[layer], cos, sin,
                        ssd_conv_w[layer], ssd_conv_b[layer].reshape(1, -1),
                        _pad_lanes(ssd_a_log[layer]), _pad_lanes(ssd_dt_bias[layer]),
                        jnp.repeat(ssd_d[layer], SSD_HD).reshape(1, SSD_W),
                        ssd_norm_w[layer].reshape(1, SSD_W))


def _forward(x, c, ctx, c_ctx, w_ada, b_ada, norm_w, ffn1_gu, ffn1_down, ffn2_gu, ffn2_down,
             w_in, w_out, ret_log_decay, ret_norm_w, attn_sink, ssd_conv_w, ssd_conv_b,
             ssd_a_log, ssd_dt_bias, ssd_d, ssd_norm_w):
    b, t, d = x.shape
    lc = ctx.shape[1]
    depth = w_ada.shape[0]
    geom = _Geom(b, t, lc)
    n_lat = b * t
    tm = math.gcd(512, math.gcd(t, b * lc))

    def mod_row(i):
        r0 = i * tm
        return jnp.where(r0 < n_lat, 1 + r0 // t, 0)

    cond = jnp.concatenate([c_ctx[None, :], c, jnp.zeros((SUBLANES - 1 - b, d), F32)], axis=0)
    mod = _adaln(cond, w_ada, b_ada)
    mod = mod.reshape(SUBLANES, depth, 3, 3, d).transpose(1, 2, 0, 3, 4).reshape(depth * 3, SUBLANES, 3, d)

    w1gu = ffn1_gu.astype(BF16)
    w1d = ffn1_down.astype(BF16)
    w2gu = ffn2_gu.astype(BF16)
    w2d = ffn2_down.astype(BF16)
    wi = w_in.astype(BF16)
    wi_dt = jnp.pad(wi[:, :, IN_MAIN:], ((0, 0), (0, 0), (0, LANES - N_DH)))
    wo = w_out.astype(BF16)

    cos, sin = _rope_tables(t)
    srcs = [x.reshape(n_lat, d), ctx.reshape(b * lc, d)]
    rows_all = n_lat + b * lc
    for layer in range(depth):
        last = layer == depth - 1
        xs = _ffn(srcs, rows_all, mod, norm_w, w1gu, w1d, layer=layer, sub=0, tm=tm, mod_row=mod_row)
        p, pdt = _proj_in(xs, mod, norm_w, wi, wi_dt, layer=layer, tm=tm, mod_row=mod_row)
        ra, at, ss = _mixer(p, pdt, geom, layer, cos, sin, ret_log_decay, ret_norm_w, attn_sink, ssd_conv_w,
                            ssd_conv_b, ssd_a_log, ssd_dt_bias, ssd_d, ssd_norm_w)
        rows_out = n_lat if last else rows_all
        xs = _proj_out(xs, rows_out, ra, at, ss, wo, mod, norm_w, layer=layer, tm=tm, mod_row=mod_row)
        xs = _ffn([xs], rows_out, mod, norm_w, w2gu, w2d, layer=layer, sub=2, tm=tm, mod_row=mod_row)
        srcs = [xs]
    return xs.reshape(b, t, d)


def kernel(x, c, ctx, c_ctx, w_ada, b_ada, norm_w, ffn1_gu, ffn1_down, ffn2_gu, ffn2_down,
           w_in, w_out, ret_log_decay, ret_norm_w, attn_sink, ssd_conv_w, ssd_conv_b,
           ssd_a_log, ssd_dt_bias, ssd_d, ssd_norm_w):
    return _forward(x, c, ctx, c_ctx, w_ada, b_ada, norm_w, ffn1_gu, ffn1_down, ffn2_gu, ffn2_down,
                    w_in, w_out, ret_log_decay, ret_norm_w, attn_sink, ssd_conv_w, ssd_conv_b,
                    ssd_a_log, ssd_dt_bias, ssd_d, ssd_norm_w)
```

```python
import functools
import math

import jax
import jax.numpy as jnp
import numpy as np
from jax import lax
from jax.experimental import pallas as pl
from jax.experimental.pallas import tpu as pltpu

F32 = jnp.float32
BF16 = jnp.bfloat16

D_MODEL = 2048
GRID_W = 64
RET_HEADS = 4
RET_DK = 128
RET_W = 512
ATT_HEADS = 4
ATT_KV_HEADS = 2
ATT_HD = 128
ATT_W = 512
ATT_KV_W = 256
WINDOW = 128
SSD_HEADS = 16
SSD_HD = 64
SSD_W = 1024
SSD_GROUPS = 2
SSD_STATE = 128
SSD_BC_W = 256
CHUNK = 128
D_FF = 5632
FFN_RES = 0.5
FFN_TF = 512
ROPE_BASE = 10000.0
NORM_EPS = 1e-6
N_MOD = 9
IN_MAIN = 5632
NEG_BIG = -1e30
LOG2E = 1.4426950408889634

CPS = 2
ATT_QB = 2
RS = CHUNK * CPS
LANES = 128
SUBLANES = 8

VMEM_LIMIT = 56 * 1024 * 1024


def _cparams(sem):
    return pltpu.CompilerParams(dimension_semantics=sem, vmem_limit_bytes=VMEM_LIMIT)


def _sigmoid(v):
    return 1.0 / (1.0 + jnp.exp(-v))


def _rms(v):
    return v * lax.rsqrt(jnp.mean(v * v, axis=-1, keepdims=True) + NORM_EPS)


def _dot(a, b):
    return jnp.dot(a, b, preferred_element_type=F32)


def _dot_nt(a, b):
    return lax.dot_general(a, b, (((1,), (1,)), ((), ())), preferred_element_type=F32)


def _dot_tn(a, b):
    return lax.dot_general(a, b, (((0,), (0,)), ((), ())), preferred_element_type=F32)


def _iota2(shape, dim):
    return lax.broadcasted_iota(jnp.int32, shape, dim)


SLAB = 16
SLAB_UNROLL = 16


def _row_slabs(n_rows, body):
    def step(r, carry):
        body(pl.ds(pl.multiple_of(r * SLAB, SLAB), SLAB))
        return carry

    lax.fori_loop(0, n_rows // SLAB, step, 0, unroll=SLAB_UNROLL)


def _modnorm_rows(x_ref, h_ref, nw_row, mod_ref, zero_ref=None):
    gain = nw_row * (1.0 + mod_ref[1:2, :])
    shift = mod_ref[0:1, :]

    def body(rows):
        h_ref[rows, :] = (_rms(x_ref[rows, :]) * gain + shift).astype(BF16)
        if zero_ref is not None:
            zero_ref[rows, :] = jnp.zeros((SLAB, zero_ref.shape[1]), zero_ref.dtype)

    _row_slabs(x_ref.shape[0], body)


def _residual_rows(x_ref, y_ref, gain, o_ref):
    def body(rows):
        o_ref[rows, :] = x_ref[rows, :] + _rms(y_ref[rows, :]) * gain

    _row_slabs(x_ref.shape[0], body)


def _adaln_kernel(c_ref, w_ref, b_ref, o_ref):
    cnd = c_ref[...]
    s = (cnd * _sigmoid(cnd)).astype(BF16)
    o_ref[...] = _dot(s, w_ref[...].astype(BF16)) + b_ref[...]


def _adaln(cond, w_ada, b_ada):
    depth, d, n = w_ada.shape
    tn = 1024
    nt = n // tn
    return pl.pallas_call(
        _adaln_kernel,
        grid=(depth, nt),
        in_specs=[
            pl.BlockSpec((SUBLANES, d), lambda l, j: (0, 0)),
            pl.BlockSpec((None, d, tn), lambda l, j: (l, 0, j)),
            pl.BlockSpec((None, 1, tn), lambda l, j: (l, 0, j)),
        ],
        out_specs=pl.BlockSpec((SUBLANES, tn), lambda l, j: (0, l * nt + j)),
        out_shape=jax.ShapeDtypeStruct((SUBLANES, depth * n), F32),
        compiler_params=_cparams(("arbitrary", "arbitrary")),
        name="adaln",
    )(cond, w_ada, b_ada.reshape(depth, 1, n))


def _ffn_kernel(*refs, sub, split):
    j = pl.program_id(1)
    is_first = j == 0
    is_last = j == pl.num_programs(1) - 1
    if split is None:
        x_ref, mod_ref, nw_ref, wg_ref, wu_ref, wd_ref, o_ref, h_ref, acc_ref = refs
        sources = [(x_ref, None)]
    else:
        xa_ref, xb_ref, mod_ref, nw_ref, wg_ref, wu_ref, wd_ref, o_ref, h_ref, acc_ref = refs
        from_a = pl.program_id(0) < split
        sources = [(xa_ref, from_a), (xb_ref, jnp.logical_not(from_a))]

    def prologue(x_ref):
        _modnorm_rows(x_ref, h_ref, nw_ref[2 * sub:2 * sub + 1, :], mod_ref, zero_ref=acc_ref)

    def epilogue(x_ref):
        gain = (FFN_RES * mod_ref[2:3, :]) * nw_ref[2 * sub + 1:2 * sub + 2, :]
        _residual_rows(x_ref, acc_ref, gain, o_ref)

    for x_ref, active in sources:
        pl.when(is_first if active is None else is_first & active)(functools.partial(prologue, x_ref))

    h = h_ref[...]
    g = _dot(h, wg_ref[...])
    u = _dot(h, wu_ref[...])
    a = (g * _sigmoid(g) * u).astype(BF16)
    acc_ref[...] += _dot(a, wd_ref[...])

    for x_ref, active in sources:
        pl.when(is_last if active is None else is_last & active)(functools.partial(epilogue, x_ref))


def _ffn(srcs, rows_out, mod, nw, w_gu, w_down, *, layer, sub, tm, mod_row):
    d = srcs[0].shape[1]
    tf = FFN_TF
    nf = D_FF // tf
    if len(srcs) == 1:
        split = None
        x_specs = [pl.BlockSpec((tm, d), lambda i, j: (i, 0))]
    else:
        split = srcs[0].shape[0] // tm
        x_specs = [pl.BlockSpec((tm, d), lambda i, j: (jnp.minimum(i, split - 1), 0)),
                   pl.BlockSpec((tm, d), lambda i, j: (jnp.maximum(i - split, 0), 0))]
    return pl.pallas_call(
        functools.partial(_ffn_kernel, sub=sub, split=split),
        grid=(rows_out // tm, nf),
        in_specs=x_specs + [
            pl.BlockSpec((None, None, 3, d), lambda i, j: (layer * 3 + sub, mod_row(i), 0, 0)),
            pl.BlockSpec((None, 6, d), lambda i, j: (layer, 0, 0)),
            pl.BlockSpec((None, d, tf), lambda i, j: (layer, 0, j)),
            pl.BlockSpec((None, d, tf), lambda i, j: (layer, 0, j + nf)),
            pl.BlockSpec((None, tf, d), lambda i, j: (layer, j, 0)),
        ],
        out_specs=pl.BlockSpec((tm, d), lambda i, j: (i, 0)),
        out_shape=jax.ShapeDtypeStruct((rows_out, d), F32),
        scratch_shapes=[pltpu.VMEM((tm, d), BF16), pltpu.VMEM((tm, d), F32)],
        compiler_params=_cparams(("arbitrary", "arbitrary")),
        name=f"ffn{sub}",
    )(*srcs, mod, nw, w_gu, w_gu, w_down)


def _proj_in_kernel(x_ref, mod_ref, nw_ref, w_ref, wdt_ref, o_ref, odt_ref, h_ref, *, tn):
    j = pl.program_id(1)

    @pl.when(j == 0)
    def _():
        _modnorm_rows(x_ref, h_ref, nw_ref[2:3, :], mod_ref)
        odt_ref[...] = _dot(h_ref[...], wdt_ref[...])

    def project(jj):
        o_ref[...] = _dot(h_ref[...], w_ref[:, jj * tn:(jj + 1) * tn])

    for jj in range(IN_MAIN // tn):
        pl.when(j == jj)(functools.partial(project, jj))


def _proj_in(xs, mod, nw, w_in, w_dt, *, layer, tm, mod_row):
    rows, d = xs.shape
    n = IN_MAIN
    tn = n // 2
    nfull = w_in.shape[2]
    return pl.pallas_call(
        functools.partial(_proj_in_kernel, tn=tn),
        grid=(rows // tm, n // tn),
        in_specs=[
            pl.BlockSpec((tm, d), lambda i, j: (i, 0)),
            pl.BlockSpec((None, None, 3, d), lambda i, j: (layer * 3 + 1, mod_row(i), 0, 0)),
            pl.BlockSpec((None, 6, d), lambda i, j: (layer, 0, 0)),
            pl.BlockSpec((None, d, nfull), lambda i, j: (layer, 0, 0), pipeline_mode=pl.Buffered(1)),
            pl.BlockSpec((None, d, LANES), lambda i, j: (layer, 0, 0)),
        ],
        out_specs=[pl.BlockSpec((tm, tn), lambda i, j: (i, j)),
                   pl.BlockSpec((tm, LANES), lambda i, j: (i, 0))],
        out_shape=[jax.ShapeDtypeStruct((rows, n), F32), jax.ShapeDtypeStruct((rows, LANES), F32)],
        scratch_shapes=[pltpu.VMEM((tm, d), BF16)],
        compiler_params=_cparams(("arbitrary", "arbitrary")),
        name="proj_in",
    )(xs, mod, nw, w_in, w_dt)


def _proj_out_kernel(x_ref, ra_ref, at_ref, ss_ref, w_ref, mod_ref, nw_ref, o_ref):
    y = _dot(ra_ref[...], w_ref[0:RET_W, :])
    y = y + _dot(at_ref[...], w_ref[RET_W:RET_W + ATT_W, :])
    y = y + _dot(ss_ref[...], w_ref[RET_W + ATT_W:, :])
    o_ref[...] = x_ref[...] + _rms(y) * (mod_ref[2:3, :] * nw_ref[3:4, :])


def _proj_out(xs, rows_out, ra, at, ss, w_out, mod, nw, *, layer, tm, mod_row):
    d = xs.shape[1]
    return pl.pallas_call(
        _proj_out_kernel,
        grid=(rows_out // tm,),
        in_specs=[
            pl.BlockSpec((tm, d), lambda i: (i, 0)),
            pl.BlockSpec((tm, RET_W), lambda i: (i, 0)),
            pl.BlockSpec((tm, ATT_W), lambda i: (i, 0)),
            pl.BlockSpec((tm, SSD_W), lambda i: (i, 0)),
            pl.BlockSpec((None, d, d), lambda i: (layer, 0, 0)),
            pl.BlockSpec((None, None, 3, d), lambda i: (layer * 3 + 1, mod_row(i), 0, 0)),
            pl.BlockSpec((None, 6, d), lambda i: (layer, 0, 0)),
        ],
        out_specs=pl.BlockSpec((tm, d), lambda i: (i, 0)),
        out_shape=jax.ShapeDtypeStruct((rows_out, d), F32),
        compiler_params=_cparams(("arbitrary",)),
        name="proj_out",
    )(xs, ra, at, ss, w_out, mod, nw)


class _Geom:
    def __init__(self, b, t, lc):
        assert t % RS == 0 and lc % RS == 0
        self.b, self.t, self.lc = b, t, lc
        self.nlat = t // RS
        self.nctx = lc // RS
        self.nst = self.nlat + self.nctx
        self.rows = b * (t + lc)

    def row_block(self, bi, j):
        return jnp.where(j < self.nctx,
                         self.b * self.nlat + bi * self.nctx + j,
                         bi * self.nlat + (j - self.nctx))

    def bwd_step(self, s):
        return jnp.where(s < self.nctx, self.nctx - 1 - s, self.nst + self.nctx - 1 - s)

    def seg_first(self, j):
        return (j == 0) | (j == self.nctx)

    def seg_last(self, j):
        return (j == self.nctx - 1) | (j == self.nst - 1)


def _ret_bstate_body(k_ref, v_ref, ld_ref, sb_ref, s_ref):
    jrow = _iota2((CHUNK, CHUNK), 0).astype(F32)
    for h in range(RET_HEADS):
        hs = slice(h * RET_DK, (h + 1) * RET_DK)
        lgb = -jnp.abs(ld_ref[1, h:h + 1, :])
        kdec = jnp.exp(lgb * jrow)
        cdec = jnp.exp(lgb * float(CHUNK))
        for c in reversed(range(CPS)):
            rows = slice(c * CHUNK, (c + 1) * CHUNK)
            sb_ref[c, h] = s_ref[h]
            kd = (k_ref[rows, hs] * kdec).astype(BF16)
            s_ref[h] = s_ref[h] * cdec + _dot_tn(kd, v_ref[rows, hs].astype(BF16))


def _ret_fwd_body(q_ref, k_ref, v_ref, g_ref, sb_ref, ld_ref, nw_ref, o_ref, s_ref):
    irow = _iota2((CHUNK, CHUNK), 0).astype(F32)
    rel = irow - _iota2((CHUNK, CHUNK), 1).astype(F32)
    for h in range(RET_HEADS):
        hs = slice(h * RET_DK, (h + 1) * RET_DK)
        lgf = -jnp.abs(ld_ref[0, h:h + 1, :])
        lgb = -jnp.abs(ld_ref[1, h:h + 1, :])
        dmask = (jnp.where(rel >= 0, jnp.exp(lgf * jnp.maximum(rel, 0.0)), 0.0)
                 + jnp.where(rel <= 0, jnp.exp(lgb * jnp.maximum(-rel, 0.0)), 0.0))
        qdec_f = jnp.exp(lgf * (irow + 1.0))
        qdec_b = jnp.exp(lgb * (float(CHUNK) - irow))
        kdec_f = jnp.exp(lgf * (float(CHUNK) - 1.0 - irow))
        cdec_f = jnp.exp(lgf * float(CHUNK))
        for c in range(CPS):
            rows = slice(c * CHUNK, (c + 1) * CHUNK)
            q = q_ref[rows, hs] * (RET_DK ** -0.5)
            k = k_ref[rows, hs]
            vb = v_ref[rows, hs].astype(BF16)
            inner = _dot_nt(q.astype(BF16), k.astype(BF16)) * dmask
            y = _dot(inner.astype(BF16), vb)
            y = y + _dot((q * qdec_f).astype(BF16), s_ref[h].astype(BF16))
            y = y + _dot((q * qdec_b).astype(BF16), sb_ref[c, h].astype(BF16))
            s_ref[h] = s_ref[h] * cdec_f + _dot_tn((k * kdec_f).astype(BF16), vb)
            mu = jnp.mean(y, axis=-1, keepdims=True)
            yc = y - mu
            var = jnp.mean(yc * yc, axis=-1, keepdims=True)
            gate = g_ref[rows, hs]
            yn = yc * lax.rsqrt(var + NORM_EPS) * nw_ref[:, hs]
            o_ref[rows, hs] = (yn * (gate * _sigmoid(gate))).astype(BF16)


def _rope(x, cos, sin):
    lane = _iota2(x.shape, 1)
    swapped = jnp.where((lane // 32) % 2 == 0, pltpu.roll(x, 96, axis=1), pltpu.roll(x, 32, axis=1))
    return x * cos + swapped * sin


def _att_body(q_ref, kl_ref, vl_ref, kc_ref, vc_ref, cos_ref, sin_ref, sink_ref, o_ref, kr_ref, vb_ref, *, geom):
    s = pl.program_id(1)
    j = geom.bwd_step(s)
    pl.when(s == 0)(functools.partial(_att_prepare, kl_ref, vl_ref, cos_ref, sin_ref, kr_ref, vb_ref, geom.t))
    pl.when(j < geom.nctx)(functools.partial(_att_ctx_rows, q_ref, kc_ref, vc_ref, sink_ref, o_ref))
    pl.when(j >= geom.nctx)(functools.partial(_att_latent_rows, j - geom.nctx, geom.t // CHUNK, q_ref, kc_ref,
                                              vc_ref, cos_ref, sin_ref, sink_ref, o_ref, kr_ref, vb_ref))


def _att_prepare(kl_ref, vl_ref, cos_ref, sin_ref, kr_ref, vb_ref, t):
    blk = CHUNK
    zeros = jnp.zeros((blk, ATT_KV_W), BF16)
    kr_ref[0:blk, :] = zeros
    kr_ref[t + blk:t + 2 * blk, :] = zeros
    vb_ref[0:blk, :] = zeros
    vb_ref[t + blk:t + 2 * blk, :] = zeros
    for h in range(ATT_KV_HEADS):
        hs = slice(h * ATT_HD, (h + 1) * ATT_HD)
        kr_ref[blk:t + blk, hs] = _rope(kl_ref[:, hs], cos_ref[...], sin_ref[...]).astype(BF16)
    vb_ref[blk:t + blk, :] = vl_ref[...].astype(BF16)


def _att_latent_rows(step, nb, q_ref, kc_ref, vc_ref, cos_ref, sin_ref, sink_ref, o_ref, kr_ref, vb_ref):
    blk = CHUNK
    scale = ATT_HD ** -0.5
    qi = _iota2((2 * blk, 3 * blk), 0) % blk
    kj = _iota2((2 * blk, 3 * blk), 1)
    in_window = jnp.abs(kj - blk - qi) <= WINDOW
    first_head = _iota2((2 * blk, 1), 0) < blk

    for qb in range(ATT_QB):
        n = step * ATT_QB + qb
        qs = slice(qb * blk, (qb + 1) * blk)
        row0 = pl.multiple_of(n * blk, blk)
        cos = cos_ref[pl.ds(row0, blk), :]
        sin = sin_ref[pl.ds(row0, blk), :]
        valid = in_window & ((kj >= blk) | (n > 0)) & ((kj < 2 * blk) | (n < nb - 1))
        for h in range(ATT_KV_HEADS):
            hs = slice(h * ATT_HD, (h + 1) * ATT_HD)
            q0 = _rope(q_ref[qs, 2 * h * ATT_HD:(2 * h + 1) * ATT_HD], cos, sin)
            q1 = _rope(q_ref[qs, (2 * h + 1) * ATT_HD:(2 * h + 2) * ATT_HD], cos, sin)
            q2 = jnp.concatenate([q0, q1], axis=0).astype(BF16)
            kw = kr_ref[pl.ds(row0, 3 * blk), hs]
            vw = vb_ref[pl.ds(row0, 3 * blk), hs]
            s_loc = jnp.where(valid, _dot_nt(q2, kw) * scale, NEG_BIG)
            s_cx = _dot_nt(q2, kc_ref[:, hs].astype(BF16)) * scale
            sink = jnp.where(first_head, sink_ref[2 * h], sink_ref[2 * h + 1])
            m = jnp.maximum(jnp.maximum(jnp.max(s_loc, axis=-1, keepdims=True),
                                        jnp.max(s_cx, axis=-1, keepdims=True)), sink)
            e_loc = jnp.exp(s_loc - m)
            e_cx = jnp.exp(s_cx - m)
            den = (jnp.sum(e_loc, axis=-1, keepdims=True) + jnp.sum(e_cx, axis=-1, keepdims=True)
                   + jnp.exp(sink - m))
            o = _dot(e_loc.astype(BF16), vw) + _dot(e_cx.astype(BF16), vc_ref[:, hs].astype(BF16))
            o = o / den
            o_ref[qs, 2 * h * ATT_HD:(2 * h + 1) * ATT_HD] = o[0:blk].astype(BF16)
            o_ref[qs, (2 * h + 1) * ATT_HD:(2 * h + 2) * ATT_HD] = o[blk:2 * blk].astype(BF16)


def _att_ctx_rows(q_ref, kc_ref, vc_ref, sink_ref, o_ref):
    lc = q_ref.shape[0]
    scale = ATT_HD ** -0.5
    first_head = _iota2((2 * lc, 1), 0) < lc
    for h in range(ATT_KV_HEADS):
        hs = slice(h * ATT_HD, (h + 1) * ATT_HD)
        q2 = jnp.concatenate([q_ref[:, 2 * h * ATT_HD:(2 * h + 1) * ATT_HD],
                              q_ref[:, (2 * h + 1) * ATT_HD:(2 * h + 2) * ATT_HD]], axis=0).astype(BF16)
        s = _dot_nt(q2, kc_ref[:, hs].astype(BF16)) * scale
        sink = jnp.where(first_head, sink_ref[2 * h], sink_ref[2 * h + 1])
        m = jnp.maximum(jnp.max(s, axis=-1, keepdims=True), sink)
        e = jnp.exp(s - m)
        den = jnp.sum(e, axis=-1, keepdims=True) + jnp.exp(sink - m)
        o = _dot(e.astype(BF16), vc_ref[:, hs].astype(BF16)) / den
        o_ref[:, 2 * h * ATT_HD:(2 * h + 1) * ATT_HD] = o[0:lc].astype(BF16)
        o_ref[:, (2 * h + 1) * ATT_HD:(2 * h + 2) * ATT_HD] = o[lc:2 * lc].astype(BF16)


N_DH = 2 * SSD_HEADS


def _pack3(v):
    hi = v.astype(BF16).astype(F32)
    r = v - hi
    mid = r.astype(BF16).astype(F32)
    lo = r - mid
    return (hi + pltpu.roll(mid, N_DH, axis=1) + pltpu.roll(lo, 2 * N_DH, axis=1)).astype(BF16)


def _unpack3(r):
    return r + pltpu.roll(r, LANES - N_DH, axis=1) + pltpu.roll(r, LANES - 2 * N_DH, axis=1)


def _conv_silu(u, prev_row, next_row, w, bias):
    n = u.shape[0]
    edge = _iota2((SUBLANES, u.shape[1]), 0)
    up = pltpu.roll(u, 1, axis=0)
    un = pltpu.roll(u, n - 1, axis=0)
    up = jnp.concatenate([jnp.where(edge == 0, prev_row, up[0:SUBLANES]), up[SUBLANES:]], axis=0)
    un = jnp.concatenate([un[0:n - SUBLANES], jnp.where(edge == SUBLANES - 1, next_row, un[n - SUBLANES:])], axis=0)
    y = w[0:1, :] * up + w[1:2, :] * u + w[2:3, :] * un + bias
    return y * _sigmoid(y)


def _softplus(v):
    return jnp.maximum(v, 0.0) + jnp.log1p(jnp.exp(-jnp.abs(v)))


def _ssd_dt_la(dt_raw, alog_ref, dtb_ref):
    lane = _iota2((1, LANES), 1)
    live = lane < N_DH
    a_row = jnp.where(live, -jnp.exp(alog_ref[...]) * LOG2E, 0.0)
    dt = jnp.where(live, _softplus(dt_raw + dtb_ref[...]), 0.0)
    return dt, dt * a_row


def _tri(upper):
    r = _iota2((CHUNK, CHUNK), 0)
    c = _iota2((CHUNK, CHUNK), 1)
    return jnp.where((c >= r) if upper else (c <= r), 1.0, 0.0).astype(BF16)


def _ssd_bstate_body(x_ref, xp_ref, xn_ref, bc_ref, bp_ref, bn_ref, dt_ref, cw_ref, cb_ref,
                     alog_ref, dtb_ref, exp_ref, sb_ref, xs_ref, bco_ref, s_ref, *, geom):
    j = geom.bwd_step(pl.program_id(1))
    first = geom.seg_first(j)
    last = geom.seg_last(j)
    hl = SUBLANES - 1
    xs_all = _conv_silu(x_ref[...],
                        jnp.where(first, 0.0, xp_ref[hl:hl + 1, :]), jnp.where(last, 0.0, xn_ref[0:1, :]),
                        cw_ref[:, 0:SSD_W], cb_ref[:, 0:SSD_W])
    bc_all = _conv_silu(bc_ref[...],
                        jnp.where(first, 0.0, bp_ref[hl:hl + 1, :]), jnp.where(last, 0.0, bn_ref[0:1, :]),
                        cw_ref[:, SSD_W:], cb_ref[:, SSD_W:]).astype(BF16)
    xs_ref[...] = xs_all
    bco_ref[...] = bc_all
    dt_all, la_all = _ssd_dt_la(dt_ref[...], alog_ref, dtb_ref)
    lane = _iota2((CHUNK, LANES), 1)
    ex = exp_ref[:, SSD_W:2 * SSD_W]
    gw = SSD_W // SSD_GROUPS
    tri_u = _tri(True)
    for c in reversed(range(CPS)):
        rows = slice(c * CHUNK, (c + 1) * CHUNK)
        sb_ref[c] = s_ref[...]
        rb = jnp.where(lane < N_DH, _unpack3(_dot(tri_u, _pack3(la_all[rows]))), 0.0)
        rbx = _dot(_pack3(rb), ex)
        dtx = _dot(_pack3(dt_all[rows]), ex)
        xw = (xs_all[rows] * dtx * jnp.exp2(rbx[0:1, :] - rbx)).astype(BF16)
        cdec = jnp.exp2(rbx[0:1, :])
        for g in range(SSD_GROUPS):
            gs = slice(g * gw, (g + 1) * gw)
            bg = bc_all[rows, g * SSD_STATE:(g + 1) * SSD_STATE]
            s_ref[g] = s_ref[g] * cdec[:, gs] + _dot_tn(bg, xw[:, gs])


def _ssd_fwd_body(z_ref, xs_ref, bc_ref, dt_ref, sb_ref, alog_ref, dtb_ref, dsk_ref, nw_ref,
                  exp_ref, expl_ref, o_ref, s_ref):
    dt_all, la_all = _ssd_dt_la(dt_ref[...], alog_ref, dtb_ref)
    lane = _iota2((CHUNK, LANES), 1)
    r = _iota2((CHUNK, CHUNK), 0)
    cidx = _iota2((CHUNK, CHUNK), 1)
    lower = r >= cidx
    diag = cidx == r
    lane64 = lane < SSD_HD
    fwd_lane = (lane % N_DH) < SSD_HEADS
    gw = SSD_W // SSD_GROUPS
    hpg = SSD_HEADS // SSD_GROUPS
    tri_l = _tri(False)
    tri_u = _tri(True)
    for c in range(CPS):
        rows = slice(c * CHUNK, (c + 1) * CHUNK)
        xs = xs_ref[rows, :]
        lap = _pack3(la_all[rows])
        cum = jnp.where(lane < N_DH,
                        _unpack3(jnp.where(fwd_lane, _dot(tri_l, lap), _dot(tri_u, lap))), 0.0)
        cum_t = cum.T
        cump = _pack3(cum)
        cumx = _dot(cump, exp_ref[...])
        dt = dt_all[rows]
        dt_t = dt.T
        xdt_f = xs * _dot(_pack3(dt), exp_ref[:, 0:SSD_W])

        cgs = []
        cbs = []
        for g in range(SSD_GROUPS):
            bg = bc_ref[rows, g * SSD_STATE:(g + 1) * SSD_STATE]
            cg = bc_ref[rows, SSD_BC_W + g * SSD_STATE:SSD_BC_W + (g + 1) * SSD_STATE]
            cgs.append((bg, cg))
            cbs.append(_dot_nt(cg, bg))
        col_f = _dot(cump, expl_ref[:, 0:SSD_HEADS * LANES])
        col_b = _dot(cump, expl_ref[:, SSD_HEADS * LANES:])
        y_parts = []
        for hp in range(SSD_HEADS // 2):
            g = (2 * hp) // hpg
            ps = slice(hp * LANES, (hp + 1) * LANES)
            lhs = []
            for e in range(2):
                h = 2 * hp + e
                hb = SSD_HEADS + h
                hs = slice(h * LANES, (h + 1) * LANES)
                dec = jnp.exp2(jnp.where(lower, col_f[:, hs] - cum_t[h:h + 1, :],
                                        col_b[:, hs] - cum_t[hb:hb + 1, :]))
                dtf = dt_t[h:h + 1, :]
                dtb = dt_t[hb:hb + 1, :]
                dts = jnp.where(diag, dtf + dtb, jnp.where(lower, dtf, dtb))
                lhs.append((cbs[g] * dec * dts).astype(BF16))
            xp = xs[:, ps]
            rhs = jnp.concatenate([jnp.where(lane64, xp, 0.0).astype(BF16),
                                   jnp.where(lane64, 0.0, xp).astype(BF16)], axis=0)
            y_parts.append(_dot(jnp.concatenate(lhs, axis=1), rhs))
        y = jnp.concatenate(y_parts, axis=1)

        e_f = jnp.exp2(cumx[:, 0:SSD_W])
        e_b = jnp.exp2(cumx[:, SSD_W:])
        yoff_f = jnp.concatenate([_dot(cgs[g][1], s_ref[g].astype(BF16)) for g in range(SSD_GROUPS)], axis=1)
        yoff_b = jnp.concatenate([_dot(cgs[g][1], sb_ref[c, g].astype(BF16)) for g in range(SSD_GROUPS)], axis=1)
        y = y + yoff_f * e_f + yoff_b * e_b

        tot = cumx[CHUNK - 1:CHUNK, 0:SSD_W]
        xw = (xdt_f * jnp.exp2(tot - cumx[:, 0:SSD_W])).astype(BF16)
        cdec = jnp.exp2(tot)
        for g in range(SSD_GROUPS):
            gs = slice(g * gw, (g + 1) * gw)
            s_ref[g] = s_ref[g] * cdec[:, gs] + _dot_tn(cgs[g][0], xw[:, gs])

        z = z_ref[rows, :]
        y = (y + dsk_ref[...] * xs) * (z * _sigmoid(z))
        o_ref[rows, :] = (_rms(y) * nw_ref[...]).astype(BF16)


def _expand_consts():
    kk = np.arange(N_DH)
    ex = np.zeros((LANES, 2 * SSD_W), np.float32)
    exl = np.zeros((LANES, N_DH * LANES), np.float32)
    for part in range(3):
        for p in range(SSD_HD):
            ex[part * N_DH + kk, kk * SSD_HD + p] = 1.0
        for q in range(LANES):
            exl[part * N_DH + kk, kk * LANES + q] = 1.0
    return jnp.asarray(ex, BF16), jnp.asarray(exl, BF16)


N_RET_BWD_IN = 3
N_SSD_BWD_IN = 12
N_RET_FWD_IN = 7


def _scan_bwd_kernel(*refs, geom):
    ret_in = refs[:N_RET_BWD_IN]
    ssd_in = refs[N_RET_BWD_IN:N_RET_BWD_IN + N_SSD_BWD_IN]
    att_in = refs[N_RET_BWD_IN + N_SSD_BWD_IN:-9]
    ret_sb, ssd_sb, xs_ref, bco_ref, att_o, ret_s, ssd_s, kr_ref, vb_ref = refs[-9:]

    @pl.when(pl.program_id(1) == 0)
    def _():
        ret_s[...] = jnp.zeros_like(ret_s)
        ssd_s[...] = jnp.zeros_like(ssd_s)

    _ret_bstate_body(*ret_in, ret_sb, ret_s)
    _ssd_bstate_body(*ssd_in, ssd_sb, xs_ref, bco_ref, ssd_s, geom=geom)
    _att_body(*att_in, att_o, kr_ref, vb_ref, geom=geom)


def _scan_fwd_kernel(*refs):
    ret_in, ssd_in = refs[:N_RET_FWD_IN], refs[N_RET_FWD_IN:-4]
    ret_o, ssd_o, ret_s, ssd_s = refs[-4:]

    @pl.when(pl.program_id(1) == 0)
    def _():
        ret_s[...] = jnp.zeros_like(ret_s)
        ssd_s[...] = jnp.zeros_like(ssd_s)

    _ret_fwd_body(*ret_in, ret_o, ret_s)
    _ssd_fwd_body(*ssd_in, ssd_o, ssd_s)


def _scan_mixers(p, pdt, geom, ld, ret_nw, sink, cos, sin, conv_w, conv_b, alog, dtb, dsk, ssd_nw):
    assert ATT_QB == CPS
    b, nst, t, lc = geom.b, geom.nst, geom.t, geom.lc
    ctx0 = (b * t) // lc
    qcol = 2048 // ATT_W
    kcol = (2048 + ATT_W) // ATT_KV_W
    vcol = kcol + 1
    ex, exl = _expand_consts()
    zcol = 3072 // SSD_W
    xcol = 4096 // SSD_W
    bccol = (4096 + SSD_W) // (2 * SSD_BC_W)
    n8 = geom.rows // SUBLANES
    sub = RS // SUBLANES
    gw = SSD_W // SSD_GROUPS
    ret_sb_block = (None, None, CPS, RET_HEADS, RET_DK, RET_DK)
    ssd_sb_block = (None, None, CPS, SSD_GROUPS, SSD_STATE, gw)
    ret_state = pltpu.VMEM((RET_HEADS, RET_DK, RET_DK), F32)
    ssd_state = pltpu.VMEM((SSD_GROUPS, SSD_STATE, gw), F32)

    def const(shape):
        return pl.BlockSpec(shape, lambda bi, s: (0,) * len(shape))

    def rows_of(step_of):
        return lambda bi, s: geom.row_block(bi, step_of(s))

    def ret_blk(col, rb):
        return pl.BlockSpec((RS, RET_W), lambda bi, s: (rb(bi, s), col))

    ld_spec = const((2, RET_HEADS, LANES))

    rb = rows_of(geom.bwd_step)
    prev8 = lambda bi, s: jnp.maximum(rb(bi, s) * sub - 1, 0)
    next8 = lambda bi, s: jnp.minimum(rb(bi, s) * sub + sub, n8 - 1)
    ret_sb, ssd_sb, xs, bcm, at = pl.pallas_call(
        functools.partial(_scan_bwd_kernel, geom=geom),
        grid=(b, nst),
        in_specs=[ret_blk(1, rb), ret_blk(2, rb), ld_spec,
                  pl.BlockSpec((RS, SSD_W), lambda bi, s: (rb(bi, s), xcol)),
                  pl.BlockSpec((SUBLANES, SSD_W), lambda bi, s: (prev8(bi, s), xcol)),
                  pl.BlockSpec((SUBLANES, SSD_W), lambda bi, s: (next8(bi, s), xcol)),
                  pl.BlockSpec((RS, 2 * SSD_BC_W), lambda bi, s: (rb(bi, s), bccol)),
                  pl.BlockSpec((SUBLANES, 2 * SSD_BC_W), lambda bi, s: (prev8(bi, s), bccol)),
                  pl.BlockSpec((SUBLANES, 2 * SSD_BC_W), lambda bi, s: (next8(bi, s), bccol)),
                  pl.BlockSpec((RS, LANES), lambda bi, s: (rb(bi, s), 0)),
                  const(conv_w.shape), const(conv_b.shape), const(alog.shape), const(dtb.shape),
                  const(ex.shape),
                  pl.BlockSpec((RS, ATT_W), lambda bi, s: (rb(bi, s), qcol)),
                  pl.BlockSpec((t, ATT_KV_W), lambda bi, s: (bi, kcol)),
                  pl.BlockSpec((t, ATT_KV_W), lambda bi, s: (bi, vcol)),
                  pl.BlockSpec((lc, ATT_KV_W), lambda bi, s: (ctx0 + bi, kcol)),
                  pl.BlockSpec((lc, ATT_KV_W), lambda bi, s: (ctx0 + bi, vcol)),
                  const(cos.shape), const(sin.shape),
                  pl.BlockSpec(memory_space=pltpu.SMEM)],
        out_specs=[pl.BlockSpec(ret_sb_block, lambda bi, s: (bi, geom.bwd_step(s), 0, 0, 0, 0)),
                   pl.BlockSpec(ssd_sb_block, lambda bi, s: (bi, geom.bwd_step(s), 0, 0, 0, 0)),
                   pl.BlockSpec((RS, SSD_W), lambda bi, s: (rb(bi, s), 0)),
                   pl.BlockSpec((RS, 2 * SSD_BC_W), lambda bi, s: (rb(bi, s), 0)),
                   pl.BlockSpec((RS, ATT_W), lambda bi, s: (rb(bi, s), 0))],
        out_shape=[jax.ShapeDtypeStruct((b, nst, CPS, RET_HEADS, RET_DK, RET_DK), F32),
                   jax.ShapeDtypeStruct((b, nst, CPS, SSD_GROUPS, SSD_STATE, gw), F32),
                   jax.ShapeDtypeStruct((geom.rows, SSD_W), F32),
                   jax.ShapeDtypeStruct((geom.rows, 2 * SSD_BC_W), BF16),
                   jax.ShapeDtypeStruct((geom.rows, ATT_W), BF16)],
        scratch_shapes=[ret_state, ssd_state,
                        pltpu.VMEM((t + 2 * CHUNK, ATT_KV_W), BF16), pltpu.VMEM((t + 2 * CHUNK, ATT_KV_W), BF16)],
        compiler_params=_cparams(("arbitrary", "arbitrary")),
        name="scan_bwd",
    )(p, p, ld, p, p, p, p, p, p, pdt, conv_w, conv_b, alog, dtb, ex, p, p, p, p, p, cos, sin, sink)

    rf = rows_of(lambda s: s)
    ra, ss = pl.pallas_call(
        _scan_fwd_kernel,
        grid=(b, nst),
        in_specs=[ret_blk(0, rf), ret_blk(1, rf), ret_blk(2, rf), ret_blk(3, rf),
                  pl.BlockSpec(ret_sb_block, lambda bi, s: (bi, s, 0, 0, 0, 0)),
                  ld_spec, const(ret_nw.shape),
                  pl.BlockSpec((RS, SSD_W), lambda bi, s: (rf(bi, s), zcol)),
                  pl.BlockSpec((RS, SSD_W), lambda bi, s: (rf(bi, s), 0)),
                  pl.BlockSpec((RS, 2 * SSD_BC_W), lambda bi, s: (rf(bi, s), 0)),
                  pl.BlockSpec((RS, LANES), lambda bi, s: (rf(bi, s), 0)),
                  pl.BlockSpec(ssd_sb_block, lambda bi, s: (bi, s, 0, 0, 0, 0)),
                  const(alog.shape), const(dtb.shape), const(dsk.shape), const(ssd_nw.shape),
                  const(ex.shape), const(exl.shape)],
        out_specs=[pl.BlockSpec((RS, RET_W), lambda bi, s: (rf(bi, s), 0)),
                   pl.BlockSpec((RS, SSD_W), lambda bi, s: (rf(bi, s), 0))],
        out_shape=[jax.ShapeDtypeStruct((geom.rows, RET_W), BF16),
                   jax.ShapeDtypeStruct((geom.rows, SSD_W), BF16)],
        scratch_shapes=[ret_state, ssd_state],
        compiler_params=_cparams(("arbitrary", "arbitrary")),
        name="scan_fwd",
    )(p, p, p, p, ret_sb, ld, ret_nw, p, xs, bcm, pdt, ssd_sb, alog, dtb, dsk, ssd_nw, ex, exl)
    return ra, at, ss


def _rope_tables(t):
    half = ATT_HD // 2
    freqs = ROPE_BASE ** (-jnp.arange(0, half, 2, dtype=F32) / half)
    pos = jnp.arange(t)
    ang_r = (pos // GRID_W).astype(F32)[:, None] * freqs[None, :]
    ang_c = (pos % GRID_W).astype(F32)[:, None] * freqs[None, :]
    cos = jnp.concatenate([jnp.cos(ang_r), jnp.cos(ang_r), jnp.cos(ang_c), jnp.cos(ang_c)], axis=-1)
    sin = jnp.concatenate([-jnp.sin(ang_r), jnp.sin(ang_r), -jnp.sin(ang_c), jnp.sin(ang_c)], axis=-1)
    return cos, sin


def _pad_lanes(v):
    v = v.reshape(1, -1)
    return jnp.pad(v, ((0, 0), (0, LANES - v.shape[1])))


def _mixer(p, pdt, geom, layer, cos, sin, ret_log_decay, ret_norm_w, attn_sink, ssd_conv_w,
           ssd_conv_b, ssd_a_log, ssd_dt_bias, ssd_d, ssd_norm_w):
    ld = jnp.broadcast_to(ret_log_decay[layer][:, :, None], (2, RET_HEADS, LANES))
    return _scan_mixers(p, pdt, geom, ld, ret_norm_w[layer].reshape(1, RET_W), attn_sink[layer], cos, sin,
                        ssd_conv_w[layer], ssd_conv_b[layer].reshape(1, -1),
                        _pad_lanes(ssd_a_log[layer]), _pad_lanes(ssd_dt_bias[layer]),
                        jnp.repeat(ssd_d[layer], SSD_HD).reshape(1, SSD_W),
                        ssd_norm_w[layer].reshape(1, SSD_W))


def _forward(x, c, ctx, c_ctx, w_ada, b_ada, norm_w, ffn1_gu, ffn1_down, ffn2_gu, ffn2_down,
             w_in, w_out, ret_log_decay, ret_norm_w, attn_sink, ssd_conv_w, ssd_conv_b,
             ssd_a_log, ssd_dt_bias, ssd_d, ssd_norm_w):
    b, t, d = x.shape
    lc = ctx.shape[1]
    depth = w_ada.shape[0]
    geom = _Geom(b, t, lc)
    n_lat = b * t
    tm = math.gcd(512, math.gcd(t, b * lc))

    def mod_row(i):
        r0 = i * tm
        return jnp.where(r0 < n_lat, 1 + r0 // t, 0)

    cond = jnp.concatenate([c_ctx[None, :], c, jnp.zeros((SUBLANES - 1 - b, d), F32)], axis=0)
    mod = _adaln(cond, w_ada, b_ada)
    mod = mod.reshape(SUBLANES, depth, 3, 3, d).transpose(1, 2, 0, 3, 4).reshape(depth * 3, SUBLANES, 3, d)

    w1gu = ffn1_gu.astype(BF16)
    w1d = ffn1_down.astype(BF16)
    w2gu = ffn2_gu.astype(BF16)
    w2d = ffn2_down.astype(BF16)
    wi = w_in.astype(BF16)
    wi_dt = jnp.pad(wi[:, :, IN_MAIN:], ((0, 0), (0, 0), (0, LANES - N_DH)))
    wo = w_out.astype(BF16)

    cos, sin = _rope_tables(t)
    srcs = [x.reshape(n_lat, d), ctx.reshape(b * lc, d)]
    rows_all = n_lat + b * lc
    for layer in range(depth):
        last = layer == depth - 1
        xs = _ffn(srcs, rows_all, mod, norm_w, w1gu, w1d, layer=layer, sub=0, tm=tm, mod_row=mod_row)
        p, pdt = _proj_in(xs, mod, norm_w, wi, wi_dt, layer=layer, tm=tm, mod_row=mod_row)
        ra, at, ss = _mixer(p, pdt, geom, layer, cos, sin, ret_log_decay, ret_norm_w, attn_sink, ssd_conv_w,
                            ssd_conv_b, ssd_a_log, ssd_dt_bias, ssd_d, ssd_norm_w)
        rows_out = n_lat if last else rows_all
        xs = _proj_out(xs, rows_out, ra, at, ss, wo, mod, norm_w, layer=layer, tm=tm, mod_row=mod_row)
        xs = _ffn([xs], rows_out, mod, norm_w, w2gu, w2d, layer=layer, sub=2, tm=tm, mod_row=mod_row)
        srcs = [xs]
    return xs.reshape(b, t, d)


def kernel(x, c, ctx, c_ctx, w_ada, b_ada, norm_w, ffn1_gu, ffn1_down, ffn2_gu, ffn2_down,
           w_in, w_out, ret_log_decay, ret_norm_w, attn_sink, ssd_conv_w, ssd_conv_b,
           ssd_a_log, ssd_dt_bias, ssd_d, ssd_norm_w):
    return _forward(x, c, ctx, c_ctx, w_ada, b_ada, norm_w, ffn1_gu, ffn1_down, ffn2_gu, ffn2_down,
                    w_in, w_out, ret_log_decay, ret_norm_w, attn_sink, ssd_conv_w, ssd_conv_b,
                    ssd_a_log, ssd_dt_bias, ssd_d, ssd_norm_w)
```

```python
import functools
import math

import jax
import jax.numpy as jnp
import numpy as np
from jax import lax
from jax.experimental import pallas as pl
from jax.experimental.pallas import tpu as pltpu

F32 = jnp.float32
BF16 = jnp.bfloat16

D_MODEL = 2048
GRID_W = 64
RET_HEADS = 4
RET_DK = 128
RET_W = 512
ATT_HEADS = 4
ATT_KV_HEADS = 2
ATT_HD = 128
ATT_W = 512
ATT_KV_W = 256
WINDOW = 128
SSD_HEADS = 16
SSD_HD = 64
SSD_W = 1024
SSD_GROUPS = 2
SSD_STATE = 128
SSD_BC_W = 256
CHUNK = 128
D_FF = 5632
FFN_RES = 0.5
FFN_TF = 512
ROPE_BASE = 10000.0
NORM_EPS = 1e-6
N_MOD = 9
IN_MAIN = 5632
NEG_BIG = -1e30
LOG2E = 1.4426950408889634

CPS = 2
ATT_QB = 2
RS = CHUNK * CPS
LANES = 128
SUBLANES = 8

VMEM_LIMIT = 56 * 1024 * 1024


def _cparams(sem):
    return pltpu.CompilerParams(dimension_semantics=sem, vmem_limit_bytes=VMEM_LIMIT)


def _sigmoid(v):
    return 1.0 / (1.0 + jnp.exp(-v))


def _rms(v):
    return v * lax.rsqrt(jnp.mean(v * v, axis=-1, keepdims=True) + NORM_EPS)


def _dot(a, b):
    return jnp.dot(a, b, preferred_element_type=F32)


def _dot_nt(a, b):
    return lax.dot_general(a, b, (((1,), (1,)), ((), ())), preferred_element_type=F32)


def _dot_tn(a, b):
    return lax.dot_general(a, b, (((0,), (0,)), ((), ())), preferred_element_type=F32)


def _iota2(shape, dim):
    return lax.broadcasted_iota(jnp.int32, shape, dim)


SLAB = 16
SLAB_UNROLL = 16


def _row_slabs(n_rows, body):
    def step(r, carry):
        body(pl.ds(pl.multiple_of(r * SLAB, SLAB), SLAB))
        return carry

    lax.fori_loop(0, n_rows // SLAB, step, 0, unroll=SLAB_UNROLL)


def _modnorm_rows(x_ref, h_ref, nw_row, mod_ref, zero_ref=None):
    gain = nw_row * (1.0 + mod_ref[1:2, :])
    shift = mod_ref[0:1, :]

    def body(rows):
        h_ref[rows, :] = (_rms(x_ref[rows, :]) * gain + shift).astype(BF16)
        if zero_ref is not None:
            zero_ref[rows, :] = jnp.zeros((SLAB, zero_ref.shape[1]), zero_ref.dtype)

    _row_slabs(x_ref.shape[0], body)


def _residual_rows(x_ref, y_ref, gain, o_ref):
    def body(rows):
        o_ref[rows, :] = x_ref[rows, :] + _rms(y_ref[rows, :]) * gain

    _row_slabs(x_ref.shape[0], body)


def _adaln_kernel(c_ref, w_ref, b_ref, o_ref):
    cnd = c_ref[...]
    s = (cnd * _sigmoid(cnd)).astype(BF16)
    o_ref[...] = _dot(s, w_ref[...].astype(BF16)) + b_ref[...]


def _adaln(cond, w_ada, b_ada):
    depth, d, n = w_ada.shape
    tn = 1024
    nt = n // tn
    return pl.pallas_call(
        _adaln_kernel,
        grid=(depth, nt),
        in_specs=[
            pl.BlockSpec((SUBLANES, d), lambda l, j: (0, 0)),
            pl.BlockSpec((None, d, tn), lambda l, j: (l, 0, j)),
            pl.BlockSpec((None, 1, tn), lambda l, j: (l, 0, j)),
        ],
        out_specs=pl.BlockSpec((SUBLANES, tn), lambda l, j: (0, l * nt + j)),
        out_shape=jax.ShapeDtypeStruct((SUBLANES, depth * n), F32),
        compiler_params=_cparams(("arbitrary", "arbitrary")),
        name="adaln",
    )(cond, w_ada, b_ada.reshape(depth, 1, n))


def _ffn_kernel(*refs, sub, split):
    j = pl.program_id(1)
    is_first = j == 0
    is_last = j == pl.num_programs(1) - 1
    if split is None:
        x_ref, mod_ref, nw_ref, wg_ref, wu_ref, wd_ref, o_ref, h_ref, acc_ref = refs
        sources = [(x_ref, None)]
    else:
        xa_ref, xb_ref, mod_ref, nw_ref, wg_ref, wu_ref, wd_ref, o_ref, h_ref, acc_ref = refs
        from_a = pl.program_id(0) < split
        sources = [(xa_ref, from_a), (xb_ref, jnp.logical_not(from_a))]

    def prologue(x_ref):
        _modnorm_rows(x_ref, h_ref, nw_ref[2 * sub:2 * sub + 1, :], mod_ref, zero_ref=acc_ref)

    def epilogue(x_ref):
        gain = (FFN_RES * mod_ref[2:3, :]) * nw_ref[2 * sub + 1:2 * sub + 2, :]
        _residual_rows(x_ref, acc_ref, gain, o_ref)

    for x_ref, active in sources:
        pl.when(is_first if active is None else is_first & active)(functools.partial(prologue, x_ref))

    h = h_ref[...]
    g = _dot(h, wg_ref[...])
    u = _dot(h, wu_ref[...])
    a = (g * _sigmoid(g) * u).astype(BF16)
    acc_ref[...] += _dot(a, wd_ref[...])

    for x_ref, active in sources:
        pl.when(is_last if active is None else is_last & active)(functools.partial(epilogue, x_ref))


W_RING = 3


def _ffn_ring_kernel(x_ref, mod_ref, nw_ref, wgu_hbm, wd_hbm, o_ref, h_ref, acc_ref,
                     wg_buf, wu_buf, wd_buf, sems, *, sub, layer, nf):
    i = pl.program_id(0)
    j = pl.program_id(1)
    step = i * nf + j
    n_steps = pl.num_programs(0) * nf
    tf = FFN_TF

    def copies(chunk, slot):
        c0 = pl.multiple_of(chunk * tf, tf)
        u0 = pl.multiple_of((chunk + nf) * tf, tf)
        return (pltpu.make_async_copy(wgu_hbm.at[layer, :, pl.ds(c0, tf)], wg_buf.at[slot], sems.at[0, slot]),
                pltpu.make_async_copy(wgu_hbm.at[layer, :, pl.ds(u0, tf)], wu_buf.at[slot], sems.at[1, slot]),
                pltpu.make_async_copy(wd_hbm.at[layer, pl.ds(c0, tf), :], wd_buf.at[slot], sems.at[2, slot]))

    def start(s, slot):
        for cp in copies(lax.rem(s, nf), slot):
            cp.start()

    @pl.when(step == 0)
    def _():
        start(0, 0)
        start(1, 1)

    @pl.when(j == 0)
    def _():
        _modnorm_rows(x_ref, h_ref, nw_ref[2 * sub:2 * sub + 1, :], mod_ref, zero_ref=acc_ref)

    def body(slot):
        @pl.when(step + 2 < n_steps)
        def _():
            start(step + 2, (slot + 2) % W_RING)

        for cp in copies(j, slot):
            cp.wait()
        h = h_ref[...]
        g = _dot(h, wg_buf[slot])
        u = _dot(h, wu_buf[slot])
        a = (g * _sigmoid(g) * u).astype(BF16)
        acc_ref[...] += _dot(a, wd_buf[slot])

    ring_slot = lax.rem(step, W_RING)
    for slot in range(W_RING):
        pl.when(ring_slot == slot)(functools.partial(body, slot))

    @pl.when(j == nf - 1)
    def _():
        gain = (FFN_RES * mod_ref[2:3, :]) * nw_ref[2 * sub + 1:2 * sub + 2, :]
        _residual_rows(x_ref, acc_ref, gain, o_ref)


def _ffn_ring(xs, rows_out, mod, nw, w_gu, w_down, *, layer, sub, tm, mod_row):
    d = xs.shape[1]
    tf = FFN_TF
    nf = D_FF // tf
    assert (rows_out // tm) * nf >= 2
    return pl.pallas_call(
        functools.partial(_ffn_ring_kernel, sub=sub, layer=layer, nf=nf),
        grid=(rows_out // tm, nf),
        in_specs=[
            pl.BlockSpec((tm, d), lambda i, j: (i, 0)),
            pl.BlockSpec((None, None, 3, d), lambda i, j: (layer * 3 + sub, mod_row(i), 0, 0)),
            pl.BlockSpec((None, 6, d), lambda i, j: (layer, 0, 0)),
            pl.BlockSpec(memory_space=pl.ANY),
            pl.BlockSpec(memory_space=pl.ANY),
        ],
        out_specs=pl.BlockSpec((tm, d), lambda i, j: (i, 0)),
        out_shape=jax.ShapeDtypeStruct((rows_out, d), F32),
        scratch_shapes=[pltpu.VMEM((tm, d), BF16), pltpu.VMEM((tm, d), F32),
                        pltpu.VMEM((W_RING, d, tf), BF16), pltpu.VMEM((W_RING, d, tf), BF16),
                        pltpu.VMEM((W_RING, tf, d), BF16), pltpu.SemaphoreType.DMA((3, W_RING))],
        compiler_params=_cparams(("arbitrary", "arbitrary")),
        name=f"ffn{sub}",
    )(xs, mod, nw, w_gu, w_down)


def _ffn(srcs, rows_out, mod, nw, w_gu, w_down, *, layer, sub, tm, mod_row):
    if len(srcs) == 1:
        return _ffn_ring(srcs[0], rows_out, mod, nw, w_gu, w_down, layer=layer, sub=sub, tm=tm, mod_row=mod_row)
    d = srcs[0].shape[1]
    tf = FFN_TF
    nf = D_FF // tf
    if len(srcs) == 1:
        split = None
        x_specs = [pl.BlockSpec((tm, d), lambda i, j: (i, 0))]
    else:
        split = srcs[0].shape[0] // tm
        x_specs = [pl.BlockSpec((tm, d), lambda i, j: (jnp.minimum(i, split - 1), 0)),
                   pl.BlockSpec((tm, d), lambda i, j: (jnp.maximum(i - split, 0), 0))]
    return pl.pallas_call(
        functools.partial(_ffn_kernel, sub=sub, split=split),
        grid=(rows_out // tm, nf),
        in_specs=x_specs + [
            pl.BlockSpec((None, None, 3, d), lambda i, j: (layer * 3 + sub, mod_row(i), 0, 0)),
            pl.BlockSpec((None, 6, d), lambda i, j: (layer, 0, 0)),
            pl.BlockSpec((None, d, tf), lambda i, j: (layer, 0, j)),
            pl.BlockSpec((None, d, tf), lambda i, j: (layer, 0, j + nf)),
            pl.BlockSpec((None, tf, d), lambda i, j: (layer, j, 0)),
        ],
        out_specs=pl.BlockSpec((tm, d), lambda i, j: (i, 0)),
        out_shape=jax.ShapeDtypeStruct((rows_out, d), F32),
        scratch_shapes=[pltpu.VMEM((tm, d), BF16), pltpu.VMEM((tm, d), F32)],
        compiler_params=_cparams(("arbitrary", "arbitrary")),
        name=f"ffn{sub}",
    )(*srcs, mod, nw, w_gu, w_gu, w_down)


def _proj_in_kernel(x_ref, mod_ref, nw_ref, w_ref, wdt_ref, o_ref, odt_ref, h_ref, *, tn):
    j = pl.program_id(1)

    @pl.when(j == 0)
    def _():
        _modnorm_rows(x_ref, h_ref, nw_ref[2:3, :], mod_ref)
        odt_ref[...] = _dot(h_ref[...], wdt_ref[...])

    def project(jj):
        o_ref[...] = _dot(h_ref[...], w_ref[:, jj * tn:(jj + 1) * tn])

    for jj in range(IN_MAIN // tn):
        pl.when(j == jj)(functools.partial(project, jj))


def _proj_in(xs, mod, nw, w_in, w_dt, *, layer, tm, mod_row):
    rows, d = xs.shape
    n = IN_MAIN
    tn = n // 2
    nfull = w_in.shape[2]
    return pl.pallas_call(
        functools.partial(_proj_in_kernel, tn=tn),
        grid=(rows // tm, n // tn),
        in_specs=[
            pl.BlockSpec((tm, d), lambda i, j: (i, 0)),
            pl.BlockSpec((None, None, 3, d), lambda i, j: (layer * 3 + 1, mod_row(i), 0, 0)),
            pl.BlockSpec((None, 6, d), lambda i, j: (layer, 0, 0)),
            pl.BlockSpec((None, d, nfull), lambda i, j: (layer, 0, 0), pipeline_mode=pl.Buffered(1)),
            pl.BlockSpec((None, d, LANES), lambda i, j: (layer, 0, 0)),
        ],
        out_specs=[pl.BlockSpec((tm, tn), lambda i, j: (i, j)),
                   pl.BlockSpec((tm, LANES), lambda i, j: (i, 0))],
        out_shape=[jax.ShapeDtypeStruct((rows, n), F32), jax.ShapeDtypeStruct((rows, LANES), F32)],
        scratch_shapes=[pltpu.VMEM((tm, d), BF16)],
        compiler_params=_cparams(("arbitrary", "arbitrary")),
        name="proj_in",
    )(xs, mod, nw, w_in, w_dt)


def _proj_out_kernel(x_ref, ra_ref, at_ref, ss_ref, w_ref, mod_ref, nw_ref, o_ref):
    y = _dot(ra_ref[...], w_ref[0:RET_W, :])
    y = y + _dot(at_ref[...], w_ref[RET_W:RET_W + ATT_W, :])
    y = y + _dot(ss_ref[...], w_ref[RET_W + ATT_W:, :])
    o_ref[...] = x_ref[...] + _rms(y) * (mod_ref[2:3, :] * nw_ref[3:4, :])


def _proj_out(xs, rows_out, ra, at, ss, w_out, mod, nw, *, layer, tm, mod_row):
    d = xs.shape[1]
    return pl.pallas_call(
        _proj_out_kernel,
        grid=(rows_out // tm,),
        in_specs=[
            pl.BlockSpec((tm, d), lambda i: (i, 0)),
            pl.BlockSpec((tm, RET_W), lambda i: (i, 0)),
            pl.BlockSpec((tm, ATT_W), lambda i: (i, 0)),
            pl.BlockSpec((tm, SSD_W), lambda i: (i, 0)),
            pl.BlockSpec((None, d, d), lambda i: (layer, 0, 0)),
            pl.BlockSpec((None, None, 3, d), lambda i: (layer * 3 + 1, mod_row(i), 0, 0)),
            pl.BlockSpec((None, 6, d), lambda i: (layer, 0, 0)),
        ],
        out_specs=pl.BlockSpec((tm, d), lambda i: (i, 0)),
        out_shape=jax.ShapeDtypeStruct((rows_out, d), F32),
        compiler_params=_cparams(("arbitrary",)),
        name="proj_out",
    )(xs, ra, at, ss, w_out, mod, nw)


class _Geom:
    def __init__(self, b, t, lc):
        assert t % RS == 0 and lc % RS == 0
        self.b, self.t, self.lc = b, t, lc
        self.nlat = t // RS
        self.nctx = lc // RS
        self.nst = self.nlat + self.nctx
        self.rows = b * (t + lc)

    def row_block(self, bi, j):
        return jnp.where(j < self.nctx,
                         self.b * self.nlat + bi * self.nctx + j,
                         bi * self.nlat + (j - self.nctx))

    def bwd_step(self, s):
        return jnp.where(s < self.nctx, self.nctx - 1 - s, self.nst + self.nctx - 1 - s)

    def seg_first(self, j):
        return (j == 0) | (j == self.nctx)

    def seg_last(self, j):
        return (j == self.nctx - 1) | (j == self.nst - 1)


def _ret_bstate_body(k_ref, v_ref, ld_ref, sb_ref, s_ref):
    jrow = _iota2((CHUNK, CHUNK), 0).astype(F32)
    for h in range(RET_HEADS):
        hs = slice(h * RET_DK, (h + 1) * RET_DK)
        lgb = -jnp.abs(ld_ref[1, h:h + 1, :])
        kdec = jnp.exp(lgb * jrow)
        cdec = jnp.exp(lgb * float(CHUNK))
        for c in reversed(range(CPS)):
            rows = slice(c * CHUNK, (c + 1) * CHUNK)
            sb_ref[c, h] = s_ref[h]
            kd = (k_ref[rows, hs] * kdec).astype(BF16)
            s_ref[h] = s_ref[h] * cdec + _dot_tn(kd, v_ref[rows, hs].astype(BF16))


def _ret_fwd_body(q_ref, k_ref, v_ref, g_ref, sb_ref, ld_ref, nw_ref, o_ref, s_ref):
    irow = _iota2((CHUNK, CHUNK), 0).astype(F32)
    rel = irow - _iota2((CHUNK, CHUNK), 1).astype(F32)
    for h in range(RET_HEADS):
        hs = slice(h * RET_DK, (h + 1) * RET_DK)
        lgf = -jnp.abs(ld_ref[0, h:h + 1, :])
        lgb = -jnp.abs(ld_ref[1, h:h + 1, :])
        dmask = (jnp.where(rel >= 0, jnp.exp(lgf * jnp.maximum(rel, 0.0)), 0.0)
                 + jnp.where(rel <= 0, jnp.exp(lgb * jnp.maximum(-rel, 0.0)), 0.0))
        qdec_f = jnp.exp(lgf * (irow + 1.0))
        qdec_b = jnp.exp(lgb * (float(CHUNK) - irow))
        kdec_f = jnp.exp(lgf * (float(CHUNK) - 1.0 - irow))
        cdec_f = jnp.exp(lgf * float(CHUNK))
        for c in range(CPS):
            rows = slice(c * CHUNK, (c + 1) * CHUNK)
            q = q_ref[rows, hs] * (RET_DK ** -0.5)
            k = k_ref[rows, hs]
            vb = v_ref[rows, hs].astype(BF16)
            inner = _dot_nt(q.astype(BF16), k.astype(BF16)) * dmask
            y = _dot(inner.astype(BF16), vb)
            y = y + _dot((q * qdec_f).astype(BF16), s_ref[h].astype(BF16))
            y = y + _dot((q * qdec_b).astype(BF16), sb_ref[c, h].astype(BF16))
            s_ref[h] = s_ref[h] * cdec_f + _dot_tn((k * kdec_f).astype(BF16), vb)
            mu = jnp.mean(y, axis=-1, keepdims=True)
            yc = y - mu
            var = jnp.mean(yc * yc, axis=-1, keepdims=True)
            gate = g_ref[rows, hs]
            yn = yc * lax.rsqrt(var + NORM_EPS) * nw_ref[:, hs]
            o_ref[rows, hs] = (yn * (gate * _sigmoid(gate))).astype(BF16)


def _rope(x, cos, sin):
    lane = _iota2(x.shape, 1)
    swapped = jnp.where((lane // 32) % 2 == 0, pltpu.roll(x, 96, axis=1), pltpu.roll(x, 32, axis=1))
    return x * cos + swapped * sin


def _att_body(q_ref, kl_ref, vl_ref, kc_ref, vc_ref, cos_ref, sin_ref, sink_ref, o_ref, kr_ref, vb_ref, *, geom):
    s = pl.program_id(1)
    j = geom.bwd_step(s)
    pl.when(s == 0)(functools.partial(_att_prepare, kl_ref, vl_ref, cos_ref, sin_ref, kr_ref, vb_ref, geom.t))
    pl.when(j < geom.nctx)(functools.partial(_att_ctx_rows, q_ref, kc_ref, vc_ref, sink_ref, o_ref))
    pl.when(j >= geom.nctx)(functools.partial(_att_latent_rows, j - geom.nctx, geom.t // CHUNK, q_ref, kc_ref,
                                              vc_ref, cos_ref, sin_ref, sink_ref, o_ref, kr_ref, vb_ref))


def _att_prepare(kl_ref, vl_ref, cos_ref, sin_ref, kr_ref, vb_ref, t):
    blk = CHUNK
    zeros = jnp.zeros((blk, ATT_KV_W), BF16)
    kr_ref[0:blk, :] = zeros
    kr_ref[t + blk:t + 2 * blk, :] = zeros
    vb_ref[0:blk, :] = zeros
    vb_ref[t + blk:t + 2 * blk, :] = zeros
    for h in range(ATT_KV_HEADS):
        hs = slice(h * ATT_HD, (h + 1) * ATT_HD)
        kr_ref[blk:t + blk, hs] = _rope(kl_ref[:, hs], cos_ref[...], sin_ref[...]).astype(BF16)
    vb_ref[blk:t + blk, :] = vl_ref[...].astype(BF16)


def _att_latent_rows(step, nb, q_ref, kc_ref, vc_ref, cos_ref, sin_ref, sink_ref, o_ref, kr_ref, vb_ref):
    blk = CHUNK
    scale = ATT_HD ** -0.5
    qi = _iota2((2 * blk, 3 * blk), 0) % blk
    kj = _iota2((2 * blk, 3 * blk), 1)
    in_window = jnp.abs(kj - blk - qi) <= WINDOW
    first_head = _iota2((2 * blk, 1), 0) < blk

    for qb in range(ATT_QB):
        n = step * ATT_QB + qb
        qs = slice(qb * blk, (qb + 1) * blk)
        row0 = pl.multiple_of(n * blk, blk)
        cos = cos_ref[pl.ds(row0, blk), :]
        sin = sin_ref[pl.ds(row0, blk), :]
        valid = in_window & ((kj >= blk) | (n > 0)) & ((kj < 2 * blk) | (n < nb - 1))
        for h in range(ATT_KV_HEADS):
            hs = slice(h * ATT_HD, (h + 1) * ATT_HD)
            q0 = _rope(q_ref[qs, 2 * h * ATT_HD:(2 * h + 1) * ATT_HD], cos, sin)
            q1 = _rope(q_ref[qs, (2 * h + 1) * ATT_HD:(2 * h + 2) * ATT_HD], cos, sin)
            q2 = jnp.concatenate([q0, q1], axis=0).astype(BF16)
            kw = kr_ref[pl.ds(row0, 3 * blk), hs]
            vw = vb_ref[pl.ds(row0, 3 * blk), hs]
            s_loc = jnp.where(valid, _dot_nt(q2, kw) * scale, NEG_BIG)
            s_cx = _dot_nt(q2, kc_ref[:, hs].astype(BF16)) * scale
            sink = jnp.where(first_head, sink_ref[2 * h], sink_ref[2 * h + 1])
            m = jnp.maximum(jnp.maximum(jnp.max(s_loc, axis=-1, keepdims=True),
                                        jnp.max(s_cx, axis=-1, keepdims=True)), sink)
            e_loc = jnp.exp(s_loc - m)
            e_cx = jnp.exp(s_cx - m)
            den = (jnp.sum(e_loc, axis=-1, keepdims=True) + jnp.sum(e_cx, axis=-1, keepdims=True)
                   + jnp.exp(sink - m))
            o = _dot(e_loc.astype(BF16), vw) + _dot(e_cx.astype(BF16), vc_ref[:, hs].astype(BF16))
            o = o / den
            o_ref[qs, 2 * h * ATT_HD:(2 * h + 1) * ATT_HD] = o[0:blk].astype(BF16)
            o_ref[qs, (2 * h + 1) * ATT_HD:(2 * h + 2) * ATT_HD] = o[blk:2 * blk].astype(BF16)


def _att_ctx_rows(q_ref, kc_ref, vc_ref, sink_ref, o_ref):
    lc = q_ref.shape[0]
    scale = ATT_HD ** -0.5
    first_head = _iota2((2 * lc, 1), 0) < lc
    for h in range(ATT_KV_HEADS):
        hs = slice(h * ATT_HD, (h + 1) * ATT_HD)
        q2 = jnp.concatenate([q_ref[:, 2 * h * ATT_HD:(2 * h + 1) * ATT_HD],
                              q_ref[:, (2 * h + 1) * ATT_HD:(2 * h + 2) * ATT_HD]], axis=0).astype(BF16)
        s = _dot_nt(q2, kc_ref[:, hs].astype(BF16)) * scale
        sink = jnp.where(first_head, sink_ref[2 * h], sink_ref[2 * h + 1])
        m = jnp.maximum(jnp.max(s, axis=-1, keepdims=True), sink)
        e = jnp.exp(s - m)
        den = jnp.sum(e, axis=-1, keepdims=True) + jnp.exp(sink - m)
        o = _dot(e.astype(BF16), vc_ref[:, hs].astype(BF16)) / den
        o_ref[:, 2 * h * ATT_HD:(2 * h + 1) * ATT_HD] = o[0:lc].astype(BF16)
        o_ref[:, (2 * h + 1) * ATT_HD:(2 * h + 2) * ATT_HD] = o[lc:2 * lc].astype(BF16)


N_DH = 2 * SSD_HEADS


def _pack3(v):
    hi = v.astype(BF16).astype(F32)
    r = v - hi
    mid = r.astype(BF16).astype(F32)
    lo = r - mid
    return (hi + pltpu.roll(mid, N_DH, axis=1) + pltpu.roll(lo, 2 * N_DH, axis=1)).astype(BF16)


def _unpack3(r):
    return r + pltpu.roll(r, LANES - N_DH, axis=1) + pltpu.roll(r, LANES - 2 * N_DH, axis=1)


def _conv_silu(u, prev_row, next_row, w, bias):
    n = u.shape[0]
    edge = _iota2((SUBLANES, u.shape[1]), 0)
    up = pltpu.roll(u, 1, axis=0)
    un = pltpu.roll(u, n - 1, axis=0)
    up = jnp.concatenate([jnp.where(edge == 0, prev_row, up[0:SUBLANES]), up[SUBLANES:]], axis=0)
    un = jnp.concatenate([un[0:n - SUBLANES], jnp.where(edge == SUBLANES - 1, next_row, un[n - SUBLANES:])], axis=0)
    y = w[0:1, :] * up + w[1:2, :] * u + w[2:3, :] * un + bias
    return y * _sigmoid(y)


def _softplus(v):
    return jnp.maximum(v, 0.0) + jnp.log1p(jnp.exp(-jnp.abs(v)))


def _ssd_dt_la(dt_raw, alog_ref, dtb_ref):
    lane = _iota2((1, LANES), 1)
    live = lane < N_DH
    a_row = jnp.where(live, -jnp.exp(alog_ref[...]) * LOG2E, 0.0)
    dt = jnp.where(live, _softplus(dt_raw + dtb_ref[...]), 0.0)
    return dt, dt * a_row


def _tri(upper):
    r = _iota2((CHUNK, CHUNK), 0)
    c = _iota2((CHUNK, CHUNK), 1)
    return jnp.where((c >= r) if upper else (c <= r), 1.0, 0.0).astype(BF16)


def _ssd_bstate_body(x_ref, xp_ref, xn_ref, bc_ref, bp_ref, bn_ref, dt_ref, cw_ref, cb_ref,
                     alog_ref, dtb_ref, exp_ref, sb_ref, xs_ref, bco_ref, s_ref, *, geom):
    j = geom.bwd_step(pl.program_id(1))
    first = geom.seg_first(j)
    last = geom.seg_last(j)
    hl = SUBLANES - 1
    xs_all = _conv_silu(x_ref[...],
                        jnp.where(first, 0.0, xp_ref[hl:hl + 1, :]), jnp.where(last, 0.0, xn_ref[0:1, :]),
                        cw_ref[:, 0:SSD_W], cb_ref[:, 0:SSD_W])
    bc_all = _conv_silu(bc_ref[...],
                        jnp.where(first, 0.0, bp_ref[hl:hl + 1, :]), jnp.where(last, 0.0, bn_ref[0:1, :]),
                        cw_ref[:, SSD_W:], cb_ref[:, SSD_W:]).astype(BF16)
    xs_ref[...] = xs_all
    bco_ref[...] = bc_all
    dt_all, la_all = _ssd_dt_la(dt_ref[...], alog_ref, dtb_ref)
    lane = _iota2((CHUNK, LANES), 1)
    ex = exp_ref[:, SSD_W:2 * SSD_W]
    gw = SSD_W // SSD_GROUPS
    tri_u = _tri(True)
    for c in reversed(range(CPS)):
        rows = slice(c * CHUNK, (c + 1) * CHUNK)
        sb_ref[c] = s_ref[...]
        rb = jnp.where(lane < N_DH, _unpack3(_dot(tri_u, _pack3(la_all[rows]))), 0.0)
        rbx = _dot(_pack3(rb), ex)
        dtx = _dot(_pack3(dt_all[rows]), ex)
        xw = (xs_all[rows] * dtx * jnp.exp2(rbx[0:1, :] - rbx)).astype(BF16)
        cdec = jnp.exp2(rbx[0:1, :])
        for g in range(SSD_GROUPS):
            gs = slice(g * gw, (g + 1) * gw)
            bg = bc_all[rows, g * SSD_STATE:(g + 1) * SSD_STATE]
            s_ref[g] = s_ref[g] * cdec[:, gs] + _dot_tn(bg, xw[:, gs])


def _ssd_fwd_body(z_ref, xs_ref, bc_ref, dt_ref, sb_ref, alog_ref, dtb_ref, dsk_ref, nw_ref,
                  exp_ref, expl_ref, o_ref, s_ref):
    dt_all, la_all = _ssd_dt_la(dt_ref[...], alog_ref, dtb_ref)
    lane = _iota2((CHUNK, LANES), 1)
    r = _iota2((CHUNK, CHUNK), 0)
    cidx = _iota2((CHUNK, CHUNK), 1)
    lower = r >= cidx
    diag = cidx == r
    lane64 = lane < SSD_HD
    fwd_lane = (lane % N_DH) < SSD_HEADS
    gw = SSD_W // SSD_GROUPS
    hpg = SSD_HEADS // SSD_GROUPS
    tri_l = _tri(False)
    tri_u = _tri(True)
    for c in range(CPS):
        rows = slice(c * CHUNK, (c + 1) * CHUNK)
        xs = xs_ref[rows, :]
        lap = _pack3(la_all[rows])
        cum = jnp.where(lane < N_DH,
                        _unpack3(jnp.where(fwd_lane, _dot(tri_l, lap), _dot(tri_u, lap))), 0.0)
        cum_t = cum.T
        cump = _pack3(cum)
        cumx = _dot(cump, exp_ref[...])
        dt = dt_all[rows]
        dt_t = dt.T
        xdt_f = xs * _dot(_pack3(dt), exp_ref[:, 0:SSD_W])

        cgs = []
        cbs = []
        for g in range(SSD_GROUPS):
            bg = bc_ref[rows, g * SSD_STATE:(g + 1) * SSD_STATE]
            cg = bc_ref[rows, SSD_BC_W + g * SSD_STATE:SSD_BC_W + (g + 1) * SSD_STATE]
            cgs.append((bg, cg))
            cbs.append(_dot_nt(cg, bg))
        col_f = _dot(cump, expl_ref[:, 0:SSD_HEADS * LANES])
        col_b = _dot(cump, expl_ref[:, SSD_HEADS * LANES:])
        y_parts = []
        for hp in range(SSD_HEADS // 2):
            g = (2 * hp) // hpg
            ps = slice(hp * LANES, (hp + 1) * LANES)
            lhs = []
            for e in range(2):
                h = 2 * hp + e
                hb = SSD_HEADS + h
                hs = slice(h * LANES, (h + 1) * LANES)
                dec = jnp.exp2(jnp.where(lower, col_f[:, hs] - cum_t[h:h + 1, :],
                                        col_b[:, hs] - cum_t[hb:hb + 1, :]))
                dtf = dt_t[h:h + 1, :]
                dtb = dt_t[hb:hb + 1, :]
                dts = jnp.where(diag, dtf + dtb, jnp.where(lower, dtf, dtb))
                lhs.append((cbs[g] * dec * dts).astype(BF16))
            xp = xs[:, ps]
            rhs = jnp.concatenate([jnp.where(lane64, xp, 0.0).astype(BF16),
                                   jnp.where(lane64, 0.0, xp).astype(BF16)], axis=0)
            y_parts.append(_dot(jnp.concatenate(lhs, axis=1), rhs))
        y = jnp.concatenate(y_parts, axis=1)

        e_f = jnp.exp2(cumx[:, 0:SSD_W])
        e_b = jnp.exp2(cumx[:, SSD_W:])
        yoff_f = jnp.concatenate([_dot(cgs[g][1], s_ref[g].astype(BF16)) for g in range(SSD_GROUPS)], axis=1)
        yoff_b = jnp.concatenate([_dot(cgs[g][1], sb_ref[c, g].astype(BF16)) for g in range(SSD_GROUPS)], axis=1)
        y = y + yoff_f * e_f + yoff_b * e_b

        tot = cumx[CHUNK - 1:CHUNK, 0:SSD_W]
        xw = (xdt_f * jnp.exp2(tot - cumx[:, 0:SSD_W])).astype(BF16)
        cdec = jnp.exp2(tot)
        for g in range(SSD_GROUPS):
            gs = slice(g * gw, (g + 1) * gw)
            s_ref[g] = s_ref[g] * cdec[:, gs] + _dot_tn(cgs[g][0], xw[:, gs])

        z = z_ref[rows, :]
        y = (y + dsk_ref[...] * xs) * (z * _sigmoid(z))
        o_ref[rows, :] = (_rms(y) * nw_ref[...]).astype(BF16)


def _expand_consts():
    kk = np.arange(N_DH)
    ex = np.zeros((LANES, 2 * SSD_W), np.float32)
    exl = np.zeros((LANES, N_DH * LANES), np.float32)
    for part in range(3):
        for p in range(SSD_HD):
            ex[part * N_DH + kk, kk * SSD_HD + p] = 1.0
        for q in range(LANES):
            exl[part * N_DH + kk, kk * LANES + q] = 1.0
    return jnp.asarray(ex, BF16), jnp.asarray(exl, BF16)


N_RET_BWD_IN = 3
N_SSD_BWD_IN = 12
N_RET_FWD_IN = 7


def _scan_bwd_kernel(*refs, geom):
    ret_in = refs[:N_RET_BWD_IN]
    ssd_in = refs[N_RET_BWD_IN:N_RET_BWD_IN + N_SSD_BWD_IN]
    att_in = refs[N_RET_BWD_IN + N_SSD_BWD_IN:-9]
    ret_sb, ssd_sb, xs_ref, bco_ref, att_o, ret_s, ssd_s, kr_ref, vb_ref = refs[-9:]

    @pl.when(pl.program_id(1) == 0)
    def _():
        ret_s[...] = jnp.zeros_like(ret_s)
        ssd_s[...] = jnp.zeros_like(ssd_s)

    _ret_bstate_body(*ret_in, ret_sb, ret_s)
    _ssd_bstate_body(*ssd_in, ssd_sb, xs_ref, bco_ref, ssd_s, geom=geom)
    _att_body(*att_in, att_o, kr_ref, vb_ref, geom=geom)


def _scan_fwd_kernel(*refs):
    ret_in, ssd_in = refs[:N_RET_FWD_IN], refs[N_RET_FWD_IN:-4]
    ret_o, ssd_o, ret_s, ssd_s = refs[-4:]

    @pl.when(pl.program_id(1) == 0)
    def _():
        ret_s[...] = jnp.zeros_like(ret_s)
        ssd_s[...] = jnp.zeros_like(ssd_s)

    _ret_fwd_body(*ret_in, ret_o, ret_s)
    _ssd_fwd_body(*ssd_in, ssd_o, ssd_s)


def _scan_mixers(p, pdt, geom, ld, ret_nw, sink, cos, sin, conv_w, conv_b, alog, dtb, dsk, ssd_nw):
    assert ATT_QB == CPS
    b, nst, t, lc = geom.b, geom.nst, geom.t, geom.lc
    ctx0 = (b * t) // lc
    qcol = 2048 // ATT_W
    kcol = (2048 + ATT_W) // ATT_KV_W
    vcol = kcol + 1
    ex, exl = _expand_consts()
    zcol = 3072 // SSD_W
    xcol = 4096 // SSD_W
    bccol = (4096 + SSD_W) // (2 * SSD_BC_W)
    n8 = geom.rows // SUBLANES
    sub = RS // SUBLANES
    gw = SSD_W // SSD_GROUPS
    ret_sb_block = (None, None, CPS, RET_HEADS, RET_DK, RET_DK)
    ssd_sb_block = (None, None, CPS, SSD_GROUPS, SSD_STATE, gw)
    ret_state = pltpu.VMEM((RET_HEADS, RET_DK, RET_DK), F32)
    ssd_state = pltpu.VMEM((SSD_GROUPS, SSD_STATE, gw), F32)

    def const(shape):
        return pl.BlockSpec(shape, lambda bi, s: (0,) * len(shape))

    def rows_of(step_of):
        return lambda bi, s: geom.row_block(bi, step_of(s))

    def ret_blk(col, rb):
        return pl.BlockSpec((RS, RET_W), lambda bi, s: (rb(bi, s), col))

    ld_spec = const((2, RET_HEADS, LANES))

    rb = rows_of(geom.bwd_step)
    prev8 = lambda bi, s: jnp.maximum(rb(bi, s) * sub - 1, 0)
    next8 = lambda bi, s: jnp.minimum(rb(bi, s) * sub + sub, n8 - 1)
    ret_sb, ssd_sb, xs, bcm, at = pl.pallas_call(
        functools.partial(_scan_bwd_kernel, geom=geom),
        grid=(b, nst),
        in_specs=[ret_blk(1, rb), ret_blk(2, rb), ld_spec,
                  pl.BlockSpec((RS, SSD_W), lambda bi, s: (rb(bi, s), xcol)),
                  pl.BlockSpec((SUBLANES, SSD_W), lambda bi, s: (prev8(bi, s), xcol)),
                  pl.BlockSpec((SUBLANES, SSD_W), lambda bi, s: (next8(bi, s), xcol)),
                  pl.BlockSpec((RS, 2 * SSD_BC_W), lambda bi, s: (rb(bi, s), bccol)),
                  pl.BlockSpec((SUBLANES, 2 * SSD_BC_W), lambda bi, s: (prev8(bi, s), bccol)),
                  pl.BlockSpec((SUBLANES, 2 * SSD_BC_W), lambda bi, s: (next8(bi, s), bccol)),
                  pl.BlockSpec((RS, LANES), lambda bi, s: (rb(bi, s), 0)),
                  const(conv_w.shape), const(conv_b.shape), const(alog.shape), const(dtb.shape),
                  const(ex.shape),
                  pl.BlockSpec((RS, ATT_W), lambda bi, s: (rb(bi, s), qcol)),
                  pl.BlockSpec((t, ATT_KV_W), lambda bi, s: (bi, kcol)),
                  pl.BlockSpec((t, ATT_KV_W), lambda bi, s: (bi, vcol)),
                  pl.BlockSpec((lc, ATT_KV_W), lambda bi, s: (ctx0 + bi, kcol)),
                  pl.BlockSpec((lc, ATT_KV_W), lambda bi, s: (ctx0 + bi, vcol)),
                  const(cos.shape), const(sin.shape),
                  pl.BlockSpec(memory_space=pltpu.SMEM)],
        out_specs=[pl.BlockSpec(ret_sb_block, lambda bi, s: (bi, geom.bwd_step(s), 0, 0, 0, 0)),
                   pl.BlockSpec(ssd_sb_block, lambda bi, s: (bi, geom.bwd_step(s), 0, 0, 0, 0)),
                   pl.BlockSpec((RS, SSD_W), lambda bi, s: (rb(bi, s), 0)),
                   pl.BlockSpec((RS, 2 * SSD_BC_W), lambda bi, s: (rb(bi, s), 0)),
                   pl.BlockSpec((RS, ATT_W), lambda bi, s: (rb(bi, s), 0))],
        out_shape=[jax.ShapeDtypeStruct((b, nst, CPS, RET_HEADS, RET_DK, RET_DK), F32),
                   jax.ShapeDtypeStruct((b, nst, CPS, SSD_GROUPS, SSD_STATE, gw), F32),
                   jax.ShapeDtypeStruct((geom.rows, SSD_W), F32),
                   jax.ShapeDtypeStruct((geom.rows, 2 * SSD_BC_W), BF16),
                   jax.ShapeDtypeStruct((geom.rows, ATT_W), BF16)],
        scratch_shapes=[ret_state, ssd_state,
                        pltpu.VMEM((t + 2 * CHUNK, ATT_KV_W), BF16), pltpu.VMEM((t + 2 * CHUNK, ATT_KV_W), BF16)],
        compiler_params=_cparams(("arbitrary", "arbitrary")),
        name="scan_bwd",
    )(p, p, ld, p, p, p, p, p, p, pdt, conv_w, conv_b, alog, dtb, ex, p, p, p, p, p, cos, sin, sink)

    rf = rows_of(lambda s: s)
    ra, ss = pl.pallas_call(
        _scan_fwd_kernel,
        grid=(b, nst),
        in_specs=[ret_blk(0, rf), ret_blk(1, rf), ret_blk(2, rf), ret_blk(3, rf),
                  pl.BlockSpec(ret_sb_block, lambda bi, s: (bi, s, 0, 0, 0, 0)),
                  ld_spec, const(ret_nw.shape),
                  pl.BlockSpec((RS, SSD_W), lambda bi, s: (rf(bi, s), zcol)),
                  pl.BlockSpec((RS, SSD_W), lambda bi, s: (rf(bi, s), 0)),
                  pl.BlockSpec((RS, 2 * SSD_BC_W), lambda bi, s: (rf(bi, s), 0)),
                  pl.BlockSpec((RS, LANES), lambda bi, s: (rf(bi, s), 0)),
                  pl.BlockSpec(ssd_sb_block, lambda bi, s: (bi, s, 0, 0, 0, 0)),
                  const(alog.shape), const(dtb.shape), const(dsk.shape), const(ssd_nw.shape),
                  const(ex.shape), const(exl.shape)],
        out_specs=[pl.BlockSpec((RS, RET_W), lambda bi, s: (rf(bi, s), 0)),
                   pl.BlockSpec((RS, SSD_W), lambda bi, s: (rf(bi, s), 0))],
        out_shape=[jax.ShapeDtypeStruct((geom.rows, RET_W), BF16),
                   jax.ShapeDtypeStruct((geom.rows, SSD_W), BF16)],
        scratch_shapes=[ret_state, ssd_state],
        compiler_params=_cparams(("arbitrary", "arbitrary")),
        name="scan_fwd",
    )(p, p, p, p, ret_sb, ld, ret_nw, p, xs, bcm, pdt, ssd_sb, alog, dtb, dsk, ssd_nw, ex, exl)
    return ra, at, ss


def _rope_tables(t):
    half = ATT_HD // 2
    freqs = ROPE_BASE ** (-jnp.arange(0, half, 2, dtype=F32) / half)
    pos = jnp.arange(t)
    ang_r = (pos // GRID_W).astype(F32)[:, None] * freqs[None, :]
    ang_c = (pos % GRID_W).astype(F32)[:, None] * freqs[None, :]
    cos = jnp.concatenate([jnp.cos(ang_r), jnp.cos(ang_r), jnp.cos(ang_c), jnp.cos(ang_c)], axis=-1)
    sin = jnp.concatenate([-jnp.sin(ang_r), jnp.sin(ang_r), -jnp.sin(ang_c), jnp.sin(ang_c)], axis=-1)
    return cos, sin


def _pad_lanes(v):
    v = v.reshape(1, -1)
    return jnp.pad(v, ((0, 0), (0, LANES - v.shape[1])))


def _mixer(p, pdt, geom, layer, cos, sin, ret_log_decay, ret_norm_w, attn_sink, ssd_conv_w,
           ssd_conv_b, ssd_a_log, ssd_dt_bias, ssd_d, ssd_norm_w):
    ld = jnp.broadcast_to(ret_log_decay[layer][:, :, None], (2, RET_HEADS, LANES))
    return _scan_mixers(p, pdt, geom, ld, ret_norm_w[layer].reshape(1, RET_W), attn_sink[layer], cos, sin,
                        ssd_conv_w[layer], ssd_conv_b[layer].reshape(1, -1),
                        _pad_lanes(ssd_a_log[layer]), _pad_lanes(ssd_dt_bias[layer]),
                        jnp.repeat(ssd_d[layer], SSD_HD).reshape(1, SSD_W),
                        ssd_norm_w[layer].reshape(1, SSD_W))


def _forward(x, c, ctx, c_ctx, w_ada, b_ada, norm_w, ffn1_gu, ffn1_down, ffn2_gu, ffn2_down,
             w_in, w_out, ret_log_decay, ret_norm_w, attn_sink, ssd_conv_w, ssd_conv_b,
             ssd_a_log, ssd_dt_bias, ssd_d, ssd_norm_w):
    b, t, d = x.shape
    lc = ctx.shape[1]
    depth = w_ada.shape[0]
    geom = _Geom(b, t, lc)
    n_lat = b * t
    tm = math.gcd(512, math.gcd(t, b * lc))

    def mod_row(i):
        r0 = i * tm
        return jnp.where(r0 < n_lat, 1 + r0 // t, 0)

    cond = jnp.concatenate([c_ctx[None, :], c, jnp.zeros((SUBLANES - 1 - b, d), F32)], axis=0)
    mod = _adaln(cond, w_ada, b_ada)
    mod = mod.reshape(SUBLANES, depth, 3, 3, d).transpose(1, 2, 0, 3, 4).reshape(depth * 3, SUBLANES, 3, d)

    w1gu = ffn1_gu.astype(BF16)
    w1d = ffn1_down.astype(BF16)
    w2gu = ffn2_gu.astype(BF16)
    w2d = ffn2_down.astype(BF16)
    wi = w_in.astype(BF16)
    wi_dt = jnp.pad(wi[:, :, IN_MAIN:], ((0, 0), (0, 0), (0, LANES - N_DH)))
    wo = w_out.astype(BF16)

    cos, sin = _rope_tables(t)
    srcs = [x.reshape(n_lat, d), ctx.reshape(b * lc, d)]
    rows_all = n_lat + b * lc
    for layer in range(depth):
        last = layer == depth - 1
        xs = _ffn(srcs, rows_all, mod, norm_w, w1gu, w1d, layer=layer, sub=0, tm=tm, mod_row=mod_row)
        p, pdt = _proj_in(xs, mod, norm_w, wi, wi_dt, layer=layer, tm=tm, mod_row=mod_row)
        ra, at, ss = _mixer(p, pdt, geom, layer, cos, sin, ret_log_decay, ret_norm_w, attn_sink, ssd_conv_w,
                            ssd_conv_b, ssd_a_log, ssd_dt_bias, ssd_d, ssd_norm_w)
        rows_out = n_lat if last else rows_all
        xs = _proj_out(xs, rows_out, ra, at, ss, wo, mod, norm_w, layer=layer, tm=tm, mod_row=mod_row)
        xs = _ffn([xs], rows_out, mod, norm_w, w2gu, w2d, layer=layer, sub=2, tm=tm, mod_row=mod_row)
        srcs = [xs]
    return xs.reshape(b, t, d)


def kernel(x, c, ctx, c_ctx, w_ada, b_ada, norm_w, ffn1_gu, ffn1_down, ffn2_gu, ffn2_down,
           w_in, w_out, ret_log_decay, ret_norm_w, attn_sink, ssd_conv_w, ssd_conv_b,
           ssd_a_log, ssd_dt_bias, ssd_d, ssd_norm_w):
    return _forward(x, c, ctx, c_ctx, w_ada, b_ada, norm_w, ffn1_gu, ffn1_down, ffn2_gu, ffn2_down,
                    w_in, w_out, ret_log_decay, ret_norm_w, attn_sink, ssd_conv_w, ssd_conv_b,
                    ssd_a_log, ssd_dt_bias, ssd_d, ssd_norm_w)
```

```python
import functools
import math

import jax
import jax.numpy as jnp
import numpy as np
from jax import lax
from jax.experimental import pallas as pl
from jax.experimental.pallas import tpu as pltpu

F32 = jnp.float32
BF16 = jnp.bfloat16

D_MODEL = 2048
GRID_W = 64
RET_HEADS = 4
RET_DK = 128
RET_W = 512
ATT_HEADS = 4
ATT_KV_HEADS = 2
ATT_HD = 128
ATT_W = 512
ATT_KV_W = 256
WINDOW = 128
SSD_HEADS = 16
SSD_HD = 64
SSD_W = 1024
SSD_GROUPS = 2
SSD_STATE = 128
SSD_BC_W = 256
CHUNK = 128
D_FF = 5632
FFN_RES = 0.5
FFN_TF = 512
ROPE_BASE = 10000.0
NORM_EPS = 1e-6
N_MOD = 9
IN_MAIN = 5632
NEG_BIG = -1e30
LOG2E = 1.4426950408889634

CPS = 2
ATT_QB = 2
RS = CHUNK * CPS
LANES = 128
SUBLANES = 8

VMEM_LIMIT = 56 * 1024 * 1024


def _cparams(sem):
    return pltpu.CompilerParams(dimension_semantics=sem, vmem_limit_bytes=VMEM_LIMIT)


def _sigmoid(v):
    return 1.0 / (1.0 + jnp.exp(-v))


def _rms(v):
    return v * lax.rsqrt(jnp.mean(v * v, axis=-1, keepdims=True) + NORM_EPS)


def _dot(a, b):
    return jnp.dot(a, b, preferred_element_type=F32)


def _dot_nt(a, b):
    return lax.dot_general(a, b, (((1,), (1,)), ((), ())), preferred_element_type=F32)


def _dot_tn(a, b):
    return lax.dot_general(a, b, (((0,), (0,)), ((), ())), preferred_element_type=F32)


def _iota2(shape, dim):
    return lax.broadcasted_iota(jnp.int32, shape, dim)


SLAB = 16
SLAB_UNROLL = 16


def _row_slabs(n_rows, body):
    def step(r, carry):
        body(pl.ds(pl.multiple_of(r * SLAB, SLAB), SLAB))
        return carry

    lax.fori_loop(0, n_rows // SLAB, step, 0, unroll=SLAB_UNROLL)


def _modnorm_rows(x_ref, h_ref, nw_row, mod_ref, zero_ref=None):
    gain = nw_row * (1.0 + mod_ref[1:2, :])
    shift = mod_ref[0:1, :]

    def body(rows):
        h_ref[rows, :] = (_rms(x_ref[rows, :]) * gain + shift).astype(BF16)
        if zero_ref is not None:
            zero_ref[rows, :] = jnp.zeros((SLAB, zero_ref.shape[1]), zero_ref.dtype)

    _row_slabs(x_ref.shape[0], body)


def _residual_rows(x_ref, y_ref, gain, o_ref):
    def body(rows):
        o_ref[rows, :] = x_ref[rows, :] + _rms(y_ref[rows, :]) * gain

    _row_slabs(x_ref.shape[0], body)


def _adaln_kernel(c_ref, w_ref, b_ref, o_ref):
    cnd = c_ref[...]
    s = (cnd * _sigmoid(cnd)).astype(BF16)
    o_ref[...] = _dot(s, w_ref[...].astype(BF16)) + b_ref[...]


def _adaln(cond, w_ada, b_ada):
    depth, d, n = w_ada.shape
    tn = 1024
    nt = n // tn
    return pl.pallas_call(
        _adaln_kernel,
        grid=(depth, nt),
        in_specs=[
            pl.BlockSpec((SUBLANES, d), lambda l, j: (0, 0)),
            pl.BlockSpec((None, d, tn), lambda l, j: (l, 0, j)),
            pl.BlockSpec((None, 1, tn), lambda l, j: (l, 0, j)),
        ],
        out_specs=pl.BlockSpec((SUBLANES, tn), lambda l, j: (0, l * nt + j)),
        out_shape=jax.ShapeDtypeStruct((SUBLANES, depth * n), F32),
        compiler_params=_cparams(("arbitrary", "arbitrary")),
        name="adaln",
    )(cond, w_ada, b_ada.reshape(depth, 1, n))


def _ffn_kernel(*refs, sub, split):
    j = pl.program_id(1)
    is_first = j == 0
    is_last = j == pl.num_programs(1) - 1
    if split is None:
        x_ref, mod_ref, nw_ref, wg_ref, wu_ref, wd_ref, o_ref, h_ref, acc_ref = refs
        sources = [(x_ref, None)]
    else:
        xa_ref, xb_ref, mod_ref, nw_ref, wg_ref, wu_ref, wd_ref, o_ref, h_ref, acc_ref = refs
        from_a = pl.program_id(0) < split
        sources = [(xa_ref, from_a), (xb_ref, jnp.logical_not(from_a))]

    def prologue(x_ref):
        _modnorm_rows(x_ref, h_ref, nw_ref[2 * sub:2 * sub + 1, :], mod_ref, zero_ref=acc_ref)

    def epilogue(x_ref):
        gain = (FFN_RES * mod_ref[2:3, :]) * nw_ref[2 * sub + 1:2 * sub + 2, :]
        _residual_rows(x_ref, acc_ref, gain, o_ref)

    for x_ref, active in sources:
        pl.when(is_first if active is None else is_first & active)(functools.partial(prologue, x_ref))

    h = h_ref[...]
    half = FFN_TF // 2
    down = None
    for c in range(2):
        cs = slice(c * half, (c + 1) * half)
        g = _dot(h, wg_ref[:, cs])
        u = _dot(h, wu_ref[:, cs])
        a = (g * _sigmoid(g) * u).astype(BF16)
        part = _dot(a, wd_ref[cs, :])
        down = part if down is None else down + part
    acc_ref[...] += down

    for x_ref, active in sources:
        pl.when(is_last if active is None else is_last & active)(functools.partial(epilogue, x_ref))


def _ffn(srcs, rows_out, mod, nw, w_gu, w_down, *, layer, sub, tm, mod_row):
    d = srcs[0].shape[1]
    tf = FFN_TF
    nf = D_FF // tf
    if len(srcs) == 1:
        split = None
        x_specs = [pl.BlockSpec((tm, d), lambda i, j: (i, 0))]
    else:
        split = srcs[0].shape[0] // tm
        x_specs = [pl.BlockSpec((tm, d), lambda i, j: (jnp.minimum(i, split - 1), 0)),
                   pl.BlockSpec((tm, d), lambda i, j: (jnp.maximum(i - split, 0), 0))]
    return pl.pallas_call(
        functools.partial(_ffn_kernel, sub=sub, split=split),
        grid=(rows_out // tm, nf),
        in_specs=x_specs + [
            pl.BlockSpec((None, None, 3, d), lambda i, j: (layer * 3 + sub, mod_row(i), 0, 0)),
            pl.BlockSpec((None, 6, d), lambda i, j: (layer, 0, 0)),
            pl.BlockSpec((None, d, tf), lambda i, j: (layer, 0, j)),
            pl.BlockSpec((None, d, tf), lambda i, j: (layer, 0, j + nf)),
            pl.BlockSpec((None, tf, d), lambda i, j: (layer, j, 0)),
        ],
        out_specs=pl.BlockSpec((tm, d), lambda i, j: (i, 0)),
        out_shape=jax.ShapeDtypeStruct((rows_out, d), F32),
        scratch_shapes=[pltpu.VMEM((tm, d), BF16), pltpu.VMEM((tm, d), F32)],
        compiler_params=_cparams(("arbitrary", "arbitrary")),
        name=f"ffn{sub}",
    )(*srcs, mod, nw, w_gu, w_gu, w_down)


def _proj_in_kernel(x_ref, mod_ref, nw_ref, w_ref, wdt_ref, o_ref, odt_ref, h_ref, *, tn):
    j = pl.program_id(1)

    @pl.when(j == 0)
    def _():
        _modnorm_rows(x_ref, h_ref, nw_ref[2:3, :], mod_ref)
        odt_ref[...] = _dot(h_ref[...], wdt_ref[...])

    def project(jj):
        o_ref[...] = _dot(h_ref[...], w_ref[:, jj * tn:(jj + 1) * tn])

    for jj in range(IN_MAIN // tn):
        pl.when(j == jj)(functools.partial(project, jj))


def _proj_in(xs, mod, nw, w_in, w_dt, *, layer, tm, mod_row):
    rows, d = xs.shape
    n = IN_MAIN
    tn = n // 2
    nfull = w_in.shape[2]
    return pl.pallas_call(
        functools.partial(_proj_in_kernel, tn=tn),
        grid=(rows // tm, n // tn),
        in_specs=[
            pl.BlockSpec((tm, d), lambda i, j: (i, 0)),
            pl.BlockSpec((None, None, 3, d), lambda i, j: (layer * 3 + 1, mod_row(i), 0, 0)),
            pl.BlockSpec((None, 6, d), lambda i, j: (layer, 0, 0)),
            pl.BlockSpec((None, d, nfull), lambda i, j: (layer, 0, 0), pipeline_mode=pl.Buffered(1)),
            pl.BlockSpec((None, d, LANES), lambda i, j: (layer, 0, 0)),
        ],
        out_specs=[pl.BlockSpec((tm, tn), lambda i, j: (i, j)),
                   pl.BlockSpec((tm, LANES), lambda i, j: (i, 0))],
        out_shape=[jax.ShapeDtypeStruct((rows, n), F32), jax.ShapeDtypeStruct((rows, LANES), F32)],
        scratch_shapes=[pltpu.VMEM((tm, d), BF16)],
        compiler_params=_cparams(("arbitrary", "arbitrary")),
        name="proj_in",
    )(xs, mod, nw, w_in, w_dt)


def _proj_out_kernel(x_ref, ra_ref, at_ref, ss_ref, w_ref, mod_ref, nw_ref, o_ref):
    y = _dot(ra_ref[...], w_ref[0:RET_W, :])
    y = y + _dot(at_ref[...], w_ref[RET_W:RET_W + ATT_W, :])
    y = y + _dot(ss_ref[...], w_ref[RET_W + ATT_W:, :])
    o_ref[...] = x_ref[...] + _rms(y) * (mod_ref[2:3, :] * nw_ref[3:4, :])


def _proj_out(xs, rows_out, ra, at, ss, w_out, mod, nw, *, layer, tm, mod_row):
    d = xs.shape[1]
    return pl.pallas_call(
        _proj_out_kernel,
        grid=(rows_out // tm,),
        in_specs=[
            pl.BlockSpec((tm, d), lambda i: (i, 0)),
            pl.BlockSpec((tm, RET_W), lambda i: (i, 0)),
            pl.BlockSpec((tm, ATT_W), lambda i: (i, 0)),
            pl.BlockSpec((tm, SSD_W), lambda i: (i, 0)),
            pl.BlockSpec((None, d, d), lambda i: (layer, 0, 0)),
            pl.BlockSpec((None, None, 3, d), lambda i: (layer * 3 + 1, mod_row(i), 0, 0)),
            pl.BlockSpec((None, 6, d), lambda i: (layer, 0, 0)),
        ],
        out_specs=pl.BlockSpec((tm, d), lambda i: (i, 0)),
        out_shape=jax.ShapeDtypeStruct((rows_out, d), F32),
        compiler_params=_cparams(("arbitrary",)),
        name="proj_out",
    )(xs, ra, at, ss, w_out, mod, nw)


class _Geom:
    def __init__(self, b, t, lc):
        assert t % RS == 0 and lc % RS == 0
        self.b, self.t, self.lc = b, t, lc
        self.nlat = t // RS
        self.nctx = lc // RS
        self.nst = self.nlat + self.nctx
        self.rows = b * (t + lc)

    def row_block(self, bi, j):
        return jnp.where(j < self.nctx,
                         self.b * self.nlat + bi * self.nctx + j,
                         bi * self.nlat + (j - self.nctx))

    def bwd_step(self, s):
        return jnp.where(s < self.nctx, self.nctx - 1 - s, self.nst + self.nctx - 1 - s)

    def seg_first(self, j):
        return (j == 0) | (j == self.nctx)

    def seg_last(self, j):
        return (j == self.nctx - 1) | (j == self.nst - 1)


def _ret_bstate_body(k_ref, v_ref, ld_ref, sb_ref, s_ref):
    jrow = _iota2((CHUNK, CHUNK), 0).astype(F32)
    for h in range(RET_HEADS):
        hs = slice(h * RET_DK, (h + 1) * RET_DK)
        lgb = -jnp.abs(ld_ref[1, h:h + 1, :])
        kdec = jnp.exp(lgb * jrow)
        cdec = jnp.exp(lgb * float(CHUNK))
        for c in reversed(range(CPS)):
            rows = slice(c * CHUNK, (c + 1) * CHUNK)
            sb_ref[c, h] = s_ref[h]
            kd = (k_ref[rows, hs] * kdec).astype(BF16)
            s_ref[h] = s_ref[h] * cdec + _dot_tn(kd, v_ref[rows, hs].astype(BF16))


def _ret_fwd_body(q_ref, k_ref, v_ref, g_ref, sb_ref, ld_ref, nw_ref, o_ref, s_ref):
    irow = _iota2((CHUNK, CHUNK), 0).astype(F32)
    rel = irow - _iota2((CHUNK, CHUNK), 1).astype(F32)
    for h in range(RET_HEADS):
        hs = slice(h * RET_DK, (h + 1) * RET_DK)
        lgf = -jnp.abs(ld_ref[0, h:h + 1, :])
        lgb = -jnp.abs(ld_ref[1, h:h + 1, :])
        dmask = (jnp.where(rel >= 0, jnp.exp(lgf * jnp.maximum(rel, 0.0)), 0.0)
                 + jnp.where(rel <= 0, jnp.exp(lgb * jnp.maximum(-rel, 0.0)), 0.0))
        qdec_f = jnp.exp(lgf * (irow + 1.0))
        qdec_b = jnp.exp(lgb * (float(CHUNK) - irow))
        kdec_f = jnp.exp(lgf * (float(CHUNK) - 1.0 - irow))
        cdec_f = jnp.exp(lgf * float(CHUNK))
        for c in range(CPS):
            rows = slice(c * CHUNK, (c + 1) * CHUNK)
            q = q_ref[rows, hs] * (RET_DK ** -0.5)
            k = k_ref[rows, hs]
            vb = v_ref[rows, hs].astype(BF16)
            inner = _dot_nt(q.astype(BF16), k.astype(BF16)) * dmask
            y = _dot(inner.astype(BF16), vb)
            y = y + _dot((q * qdec_f).astype(BF16), s_ref[h].astype(BF16))
            y = y + _dot((q * qdec_b).astype(BF16), sb_ref[c, h].astype(BF16))
            s_ref[h] = s_ref[h] * cdec_f + _dot_tn((k * kdec_f).astype(BF16), vb)
            mu = jnp.mean(y, axis=-1, keepdims=True)
            yc = y - mu
            var = jnp.mean(yc * yc, axis=-1, keepdims=True)
            gate = g_ref[rows, hs]
            yn = yc * lax.rsqrt(var + NORM_EPS) * nw_ref[:, hs]
            o_ref[rows, hs] = (yn * (gate * _sigmoid(gate))).astype(BF16)


def _rope(x, cos, sin):
    lane = _iota2(x.shape, 1)
    swapped = jnp.where((lane // 32) % 2 == 0, pltpu.roll(x, 96, axis=1), pltpu.roll(x, 32, axis=1))
    return x * cos + swapped * sin


def _att_body(q_ref, kl_ref, vl_ref, kc_ref, vc_ref, cos_ref, sin_ref, sink_ref, o_ref, kr_ref, vb_ref, *, geom):
    s = pl.program_id(1)
    j = geom.bwd_step(s)
    pl.when(s == 0)(functools.partial(_att_prepare, kl_ref, vl_ref, cos_ref, sin_ref, kr_ref, vb_ref, geom.t))
    pl.when(j < geom.nctx)(functools.partial(_att_ctx_rows, q_ref, kc_ref, vc_ref, sink_ref, o_ref))
    pl.when(j >= geom.nctx)(functools.partial(_att_latent_rows, j - geom.nctx, geom.t // CHUNK, q_ref, kc_ref,
                                              vc_ref, cos_ref, sin_ref, sink_ref, o_ref, kr_ref, vb_ref))


def _att_prepare(kl_ref, vl_ref, cos_ref, sin_ref, kr_ref, vb_ref, t):
    blk = CHUNK
    zeros = jnp.zeros((blk, ATT_KV_W), BF16)
    kr_ref[0:blk, :] = zeros
    kr_ref[t + blk:t + 2 * blk, :] = zeros
    vb_ref[0:blk, :] = zeros
    vb_ref[t + blk:t + 2 * blk, :] = zeros
    for h in range(ATT_KV_HEADS):
        hs = slice(h * ATT_HD, (h + 1) * ATT_HD)
        kr_ref[blk:t + blk, hs] = _rope(kl_ref[:, hs], cos_ref[...], sin_ref[...]).astype(BF16)
    vb_ref[blk:t + blk, :] = vl_ref[...].astype(BF16)


def _att_latent_rows(step, nb, q_ref, kc_ref, vc_ref, cos_ref, sin_ref, sink_ref, o_ref, kr_ref, vb_ref):
    blk = CHUNK
    scale = ATT_HD ** -0.5
    qi = _iota2((2 * blk, 3 * blk), 0) % blk
    kj = _iota2((2 * blk, 3 * blk), 1)
    in_window = jnp.abs(kj - blk - qi) <= WINDOW
    first_head = _iota2((2 * blk, 1), 0) < blk

    for qb in range(ATT_QB):
        n = step * ATT_QB + qb
        qs = slice(qb * blk, (qb + 1) * blk)
        row0 = pl.multiple_of(n * blk, blk)
        cos = cos_ref[pl.ds(row0, blk), :]
        sin = sin_ref[pl.ds(row0, blk), :]
        valid = in_window & ((kj >= blk) | (n > 0)) & ((kj < 2 * blk) | (n < nb - 1))
        for h in range(ATT_KV_HEADS):
            hs = slice(h * ATT_HD, (h + 1) * ATT_HD)
            q0 = _rope(q_ref[qs, 2 * h * ATT_HD:(2 * h + 1) * ATT_HD], cos, sin)
            q1 = _rope(q_ref[qs, (2 * h + 1) * ATT_HD:(2 * h + 2) * ATT_HD], cos, sin)
            q2 = jnp.concatenate([q0, q1], axis=0).astype(BF16)
            kw = kr_ref[pl.ds(row0, 3 * blk), hs]
            vw = vb_ref[pl.ds(row0, 3 * blk), hs]
            s_loc = jnp.where(valid, _dot_nt(q2, kw) * scale, NEG_BIG)
            s_cx = _dot_nt(q2, kc_ref[:, hs].astype(BF16)) * scale
            sink = jnp.where(first_head, sink_ref[2 * h], sink_ref[2 * h + 1])
            m = jnp.maximum(jnp.maximum(jnp.max(s_loc, axis=-1, keepdims=True),
                                        jnp.max(s_cx, axis=-1, keepdims=True)), sink)
            e_loc = jnp.exp(s_loc - m)
            e_cx = jnp.exp(s_cx - m)
            den = (jnp.sum(e_loc, axis=-1, keepdims=True) + jnp.sum(e_cx, axis=-1, keepdims=True)
                   + jnp.exp(sink - m))
            o = _dot(e_loc.astype(BF16), vw) + _dot(e_cx.astype(BF16), vc_ref[:, hs].astype(BF16))
            o = o / den
            o_ref[qs, 2 * h * ATT_HD:(2 * h + 1) * ATT_HD] = o[0:blk].astype(BF16)
            o_ref[qs, (2 * h + 1) * ATT_HD:(2 * h + 2) * ATT_HD] = o[blk:2 * blk].astype(BF16)


def _att_ctx_rows(q_ref, kc_ref, vc_ref, sink_ref, o_ref):
    lc = q_ref.shape[0]
    scale = ATT_HD ** -0.5
    first_head = _iota2((2 * lc, 1), 0) < lc
    for h in range(ATT_KV_HEADS):
        hs = slice(h * ATT_HD, (h + 1) * ATT_HD)
        q2 = jnp.concatenate([q_ref[:, 2 * h * ATT_HD:(2 * h + 1) * ATT_HD],
                              q_ref[:, (2 * h + 1) * ATT_HD:(2 * h + 2) * ATT_HD]], axis=0).astype(BF16)
        s = _dot_nt(q2, kc_ref[:, hs].astype(BF16)) * scale
        sink = jnp.where(first_head, sink_ref[2 * h], sink_ref[2 * h + 1])
        m = jnp.maximum(jnp.max(s, axis=-1, keepdims=True), sink)
        e = jnp.exp(s - m)
        den = jnp.sum(e, axis=-1, keepdims=True) + jnp.exp(sink - m)
        o = _dot(e.astype(BF16), vc_ref[:, hs].astype(BF16)) / den
        o_ref[:, 2 * h * ATT_HD:(2 * h + 1) * ATT_HD] = o[0:lc].astype(BF16)
        o_ref[:, (2 * h + 1) * ATT_HD:(2 * h + 2) * ATT_HD] = o[lc:2 * lc].astype(BF16)


N_DH = 2 * SSD_HEADS


def _pack3(v):
    hi = v.astype(BF16).astype(F32)
    r = v - hi
    mid = r.astype(BF16).astype(F32)
    lo = r - mid
    return (hi + pltpu.roll(mid, N_DH, axis=1) + pltpu.roll(lo, 2 * N_DH, axis=1)).astype(BF16)


def _unpack3(r):
    return r + pltpu.roll(r, LANES - N_DH, axis=1) + pltpu.roll(r, LANES - 2 * N_DH, axis=1)


def _conv_silu(u, prev_row, next_row, w, bias):
    n = u.shape[0]
    edge = _iota2((SUBLANES, u.shape[1]), 0)
    up = pltpu.roll(u, 1, axis=0)
    un = pltpu.roll(u, n - 1, axis=0)
    up = jnp.concatenate([jnp.where(edge == 0, prev_row, up[0:SUBLANES]), up[SUBLANES:]], axis=0)
    un = jnp.concatenate([un[0:n - SUBLANES], jnp.where(edge == SUBLANES - 1, next_row, un[n - SUBLANES:])], axis=0)
    y = w[0:1, :] * up + w[1:2, :] * u + w[2:3, :] * un + bias
    return y * _sigmoid(y)


def _softplus(v):
    return jnp.maximum(v, 0.0) + jnp.log1p(jnp.exp(-jnp.abs(v)))


def _ssd_dt_la(dt_raw, alog_ref, dtb_ref):
    lane = _iota2((1, LANES), 1)
    live = lane < N_DH
    a_row = jnp.where(live, -jnp.exp(alog_ref[...]) * LOG2E, 0.0)
    dt = jnp.where(live, _softplus(dt_raw + dtb_ref[...]), 0.0)
    return dt, dt * a_row


def _tri(upper):
    r = _iota2((CHUNK, CHUNK), 0)
    c = _iota2((CHUNK, CHUNK), 1)
    return jnp.where((c >= r) if upper else (c <= r), 1.0, 0.0).astype(BF16)


def _ssd_bstate_body(x_ref, xp_ref, xn_ref, bc_ref, bp_ref, bn_ref, dt_ref, cw_ref, cb_ref,
                     alog_ref, dtb_ref, exp_ref, sb_ref, xs_ref, bco_ref, s_ref, *, geom):
    j = geom.bwd_step(pl.program_id(1))
    first = geom.seg_first(j)
    last = geom.seg_last(j)
    hl = SUBLANES - 1
    xs_all = _conv_silu(x_ref[...],
                        jnp.where(first, 0.0, xp_ref[hl:hl + 1, :]), jnp.where(last, 0.0, xn_ref[0:1, :]),
                        cw_ref[:, 0:SSD_W], cb_ref[:, 0:SSD_W])
    bc_all = _conv_silu(bc_ref[...],
                        jnp.where(first, 0.0, bp_ref[hl:hl + 1, :]), jnp.where(last, 0.0, bn_ref[0:1, :]),
                        cw_ref[:, SSD_W:], cb_ref[:, SSD_W:]).astype(BF16)
    xs_ref[...] = xs_all
    bco_ref[...] = bc_all
    dt_all, la_all = _ssd_dt_la(dt_ref[...], alog_ref, dtb_ref)
    lane = _iota2((CHUNK, LANES), 1)
    ex = exp_ref[:, SSD_W:2 * SSD_W]
    gw = SSD_W // SSD_GROUPS
    tri_u = _tri(True)
    for c in reversed(range(CPS)):
        rows = slice(c * CHUNK, (c + 1) * CHUNK)
        sb_ref[c] = s_ref[...]
        rb = jnp.where(lane < N_DH, _unpack3(_dot(tri_u, _pack3(la_all[rows]))), 0.0)
        rbx = _dot(_pack3(rb), ex)
        dtx = _dot(_pack3(dt_all[rows]), ex)
        xw = (xs_all[rows] * dtx * jnp.exp2(rbx[0:1, :] - rbx)).astype(BF16)
        cdec = jnp.exp2(rbx[0:1, :])
        for g in range(SSD_GROUPS):
            gs = slice(g * gw, (g + 1) * gw)
            bg = bc_all[rows, g * SSD_STATE:(g + 1) * SSD_STATE]
            s_ref[g] = s_ref[g] * cdec[:, gs] + _dot_tn(bg, xw[:, gs])


def _ssd_fwd_body(z_ref, xs_ref, bc_ref, dt_ref, sb_ref, alog_ref, dtb_ref, dsk_ref, nw_ref,
                  exp_ref, expl_ref, o_ref, s_ref):
    dt_all, la_all = _ssd_dt_la(dt_ref[...], alog_ref, dtb_ref)
    lane = _iota2((CHUNK, LANES), 1)
    r = _iota2((CHUNK, CHUNK), 0)
    cidx = _iota2((CHUNK, CHUNK), 1)
    lower = r >= cidx
    diag = cidx == r
    lane64 = lane < SSD_HD
    fwd_lane = (lane % N_DH) < SSD_HEADS
    gw = SSD_W // SSD_GROUPS
    hpg = SSD_HEADS // SSD_GROUPS
    tri_l = _tri(False)
    tri_u = _tri(True)
    for c in range(CPS):
        rows = slice(c * CHUNK, (c + 1) * CHUNK)
        xs = xs_ref[rows, :]
        lap = _pack3(la_all[rows])
        cum = jnp.where(lane < N_DH,
                        _unpack3(jnp.where(fwd_lane, _dot(tri_l, lap), _dot(tri_u, lap))), 0.0)
        cum_t = cum.T
        cump = _pack3(cum)
        cumx = _dot(cump, exp_ref[...])
        dt = dt_all[rows]
        dt_t = dt.T
        xdt_f = xs * _dot(_pack3(dt), exp_ref[:, 0:SSD_W])

        cgs = []
        cbs = []
        for g in range(SSD_GROUPS):
            bg = bc_ref[rows, g * SSD_STATE:(g + 1) * SSD_STATE]
            cg = bc_ref[rows, SSD_BC_W + g * SSD_STATE:SSD_BC_W + (g + 1) * SSD_STATE]
            cgs.append((bg, cg))
            cbs.append(_dot_nt(cg, bg))
        col_f = _dot(cump, expl_ref[:, 0:SSD_HEADS * LANES])
        col_b = _dot(cump, expl_ref[:, SSD_HEADS * LANES:])
        y_parts = []
        for hp in range(SSD_HEADS // 2):
            g = (2 * hp) // hpg
            ps = slice(hp * LANES, (hp + 1) * LANES)
            lhs = []
            for e in range(2):
                h = 2 * hp + e
                hb = SSD_HEADS + h
                hs = slice(h * LANES, (h + 1) * LANES)
                dec = jnp.exp2(jnp.where(lower, col_f[:, hs] - cum_t[h:h + 1, :],
                                        col_b[:, hs] - cum_t[hb:hb + 1, :]))
                dtf = dt_t[h:h + 1, :]
                dtb = dt_t[hb:hb + 1, :]
                dts = jnp.where(diag, dtf + dtb, jnp.where(lower, dtf, dtb))
                lhs.append((cbs[g] * dec * dts).astype(BF16))
            xp = xs[:, ps]
            rhs = jnp.concatenate([jnp.where(lane64, xp, 0.0).astype(BF16),
                                   jnp.where(lane64, 0.0, xp).astype(BF16)], axis=0)
            y_parts.append(_dot(jnp.concatenate(lhs, axis=1), rhs))
        y = jnp.concatenate(y_parts, axis=1)

        e_f = jnp.exp2(cumx[:, 0:SSD_W])
        e_b = jnp.exp2(cumx[:, SSD_W:])
        yoff_f = jnp.concatenate([_dot(cgs[g][1], s_ref[g].astype(BF16)) for g in range(SSD_GROUPS)], axis=1)
        yoff_b = jnp.concatenate([_dot(cgs[g][1], sb_ref[c, g].astype(BF16)) for g in range(SSD_GROUPS)], axis=1)
        y = y + yoff_f * e_f + yoff_b * e_b

        tot = cumx[CHUNK - 1:CHUNK, 0:SSD_W]
        xw = (xdt_f * jnp.exp2(tot - cumx[:, 0:SSD_W])).astype(BF16)
        cdec = jnp.exp2(tot)
        for g in range(SSD_GROUPS):
            gs = slice(g * gw, (g + 1) * gw)
            s_ref[g] = s_ref[g] * cdec[:, gs] + _dot_tn(cgs[g][0], xw[:, gs])

        z = z_ref[rows, :]
        y = (y + dsk_ref[...] * xs) * (z * _sigmoid(z))
        o_ref[rows, :] = (_rms(y) * nw_ref[...]).astype(BF16)


def _expand_consts():
    kk = np.arange(N_DH)
    ex = np.zeros((LANES, 2 * SSD_W), np.float32)
    exl = np.zeros((LANES, N_DH * LANES), np.float32)
    for part in range(3):
        for p in range(SSD_HD):
            ex[part * N_DH + kk, kk * SSD_HD + p] = 1.0
        for q in range(LANES):
            exl[part * N_DH + kk, kk * LANES + q] = 1.0
    return jnp.asarray(ex, BF16), jnp.asarray(exl, BF16)


N_RET_BWD_IN = 3
N_SSD_BWD_IN = 12
N_RET_FWD_IN = 7


def _scan_bwd_kernel(*refs, geom):
    ret_in = refs[:N_RET_BWD_IN]
    ssd_in = refs[N_RET_BWD_IN:N_RET_BWD_IN + N_SSD_BWD_IN]
    att_in = refs[N_RET_BWD_IN + N_SSD_BWD_IN:-9]
    ret_sb, ssd_sb, xs_ref, bco_ref, att_o, ret_s, ssd_s, kr_ref, vb_ref = refs[-9:]

    @pl.when(pl.program_id(1) == 0)
    def _():
        ret_s[...] = jnp.zeros_like(ret_s)
        ssd_s[...] = jnp.zeros_like(ssd_s)

    _ret_bstate_body(*ret_in, ret_sb, ret_s)
    _ssd_bstate_body(*ssd_in, ssd_sb, xs_ref, bco_ref, ssd_s, geom=geom)
    _att_body(*att_in, att_o, kr_ref, vb_ref, geom=geom)


def _scan_fwd_kernel(*refs):
    ret_in, ssd_in = refs[:N_RET_FWD_IN], refs[N_RET_FWD_IN:-4]
    ret_o, ssd_o, ret_s, ssd_s = refs[-4:]

    @pl.when(pl.program_id(1) == 0)
    def _():
        ret_s[...] = jnp.zeros_like(ret_s)
        ssd_s[...] = jnp.zeros_like(ssd_s)

    _ret_fwd_body(*ret_in, ret_o, ret_s)
    _ssd_fwd_body(*ssd_in, ssd_o, ssd_s)


def _scan_mixers(p, pdt, geom, ld, ret_nw, sink, cos, sin, conv_w, conv_b, alog, dtb, dsk, ssd_nw):
    assert ATT_QB == CPS
    b, nst, t, lc = geom.b, geom.nst, geom.t, geom.lc
    ctx0 = (b * t) // lc
    qcol = 2048 // ATT_W
    kcol = (2048 + ATT_W) // ATT_KV_W
    vcol = kcol + 1
    ex, exl = _expand_consts()
    zcol = 3072 // SSD_W
    xcol = 4096 // SSD_W
    bccol = (4096 + SSD_W) // (2 * SSD_BC_W)
    n8 = geom.rows // SUBLANES
    sub = RS // SUBLANES
    gw = SSD_W // SSD_GROUPS
    ret_sb_block = (None, None, CPS, RET_HEADS, RET_DK, RET_DK)
    ssd_sb_block = (None, None, CPS, SSD_GROUPS, SSD_STATE, gw)
    ret_state = pltpu.VMEM((RET_HEADS, RET_DK, RET_DK), F32)
    ssd_state = pltpu.VMEM((SSD_GROUPS, SSD_STATE, gw), F32)

    def const(shape):
        return pl.BlockSpec(shape, lambda bi, s: (0,) * len(shape))

    def rows_of(step_of):
        return lambda bi, s: geom.row_block(bi, step_of(s))

    def ret_blk(col, rb):
        return pl.BlockSpec((RS, RET_W), lambda bi, s: (rb(bi, s), col))

    ld_spec = const((2, RET_HEADS, LANES))

    rb = rows_of(geom.bwd_step)
    prev8 = lambda bi, s: jnp.maximum(rb(bi, s) * sub - 1, 0)
    next8 = lambda bi, s: jnp.minimum(rb(bi, s) * sub + sub, n8 - 1)
    ret_sb, ssd_sb, xs, bcm, at = pl.pallas_call(
        functools.partial(_scan_bwd_kernel, geom=geom),
        grid=(b, nst),
        in_specs=[ret_blk(1, rb), ret_blk(2, rb), ld_spec,
                  pl.BlockSpec((RS, SSD_W), lambda bi, s: (rb(bi, s), xcol)),
                  pl.BlockSpec((SUBLANES, SSD_W), lambda bi, s: (prev8(bi, s), xcol)),
                  pl.BlockSpec((SUBLANES, SSD_W), lambda bi, s: (next8(bi, s), xcol)),
                  pl.BlockSpec((RS, 2 * SSD_BC_W), lambda bi, s: (rb(bi, s), bccol)),
                  pl.BlockSpec((SUBLANES, 2 * SSD_BC_W), lambda bi, s: (prev8(bi, s), bccol)),
                  pl.BlockSpec((SUBLANES, 2 * SSD_BC_W), lambda bi, s: (next8(bi, s), bccol)),
                  pl.BlockSpec((RS, LANES), lambda bi, s: (rb(bi, s), 0)),
                  const(conv_w.shape), const(conv_b.shape), const(alog.shape), const(dtb.shape),
                  const(ex.shape),
                  pl.BlockSpec((RS, ATT_W), lambda bi, s: (rb(bi, s), qcol)),
                  pl.BlockSpec((t, ATT_KV_W), lambda bi, s: (bi, kcol)),
                  pl.BlockSpec((t, ATT_KV_W), lambda bi, s: (bi, vcol)),
                  pl.BlockSpec((lc, ATT_KV_W), lambda bi, s: (ctx0 + bi, kcol)),
                  pl.BlockSpec((lc, ATT_KV_W), lambda bi, s: (ctx0 + bi, vcol)),
                  const(cos.shape), const(sin.shape),
                  pl.BlockSpec(memory_space=pltpu.SMEM)],
        out_specs=[pl.BlockSpec(ret_sb_block, lambda bi, s: (bi, geom.bwd_step(s), 0, 0, 0, 0)),
                   pl.BlockSpec(ssd_sb_block, lambda bi, s: (bi, geom.bwd_step(s), 0, 0, 0, 0)),
                   pl.BlockSpec((RS, SSD_W), lambda bi, s: (rb(bi, s), 0)),
                   pl.BlockSpec((RS, 2 * SSD_BC_W), lambda bi, s: (rb(bi, s), 0)),
                   pl.BlockSpec((RS, ATT_W), lambda bi, s: (rb(bi, s), 0))],
        out_shape=[jax.ShapeDtypeStruct((b, nst, CPS, RET_HEADS, RET_DK, RET_DK), F32),
                   jax.ShapeDtypeStruct((b, nst, CPS, SSD_GROUPS, SSD_STATE, gw), F32),
                   jax.ShapeDtypeStruct((geom.rows, SSD_W), F32),
                   jax.ShapeDtypeStruct((geom.rows, 2 * SSD_BC_W), BF16),
                   jax.ShapeDtypeStruct((geom.rows, ATT_W), BF16)],
        scratch_shapes=[ret_state, ssd_state,
                        pltpu.VMEM((t + 2 * CHUNK, ATT_KV_W), BF16), pltpu.VMEM((t + 2 * CHUNK, ATT_KV_W), BF16)],
        compiler_params=_cparams(("arbitrary", "arbitrary")),
        name="scan_bwd",
    )(p, p, ld, p, p, p, p, p, p, pdt, conv_w, conv_b, alog, dtb, ex, p, p, p, p, p, cos, sin, sink)

    rf = rows_of(lambda s: s)
    ra, ss = pl.pallas_call(
        _scan_fwd_kernel,
        grid=(b, nst),
        in_specs=[ret_blk(0, rf), ret_blk(1, rf), ret_blk(2, rf), ret_blk(3, rf),
                  pl.BlockSpec(ret_sb_block, lambda bi, s: (bi, s, 0, 0, 0, 0)),
                  ld_spec, const(ret_nw.shape),
                  pl.BlockSpec((RS, SSD_W), lambda bi, s: (rf(bi, s), zcol)),
                  pl.BlockSpec((RS, SSD_W), lambda bi, s: (rf(bi, s), 0)),
                  pl.BlockSpec((RS, 2 * SSD_BC_W), lambda bi, s: (rf(bi, s), 0)),
                  pl.BlockSpec((RS, LANES), lambda bi, s: (rf(bi, s), 0)),
                  pl.BlockSpec(ssd_sb_block, lambda bi, s: (bi, s, 0, 0, 0, 0)),
                  const(alog.shape), const(dtb.shape), const(dsk.shape), const(ssd_nw.shape),
                  const(ex.shape), const(exl.shape)],
        out_specs=[pl.BlockSpec((RS, RET_W), lambda bi, s: (rf(bi, s), 0)),
                   pl.BlockSpec((RS, SSD_W), lambda bi, s: (rf(bi, s), 0))],
        out_shape=[jax.ShapeDtypeStruct((geom.rows, RET_W), BF16),
                   jax.ShapeDtypeStruct((geom.rows, SSD_W), BF16)],
        scratch_shapes=[ret_state, ssd_state],
        compiler_params=_cparams(("arbitrary", "arbitrary")),
        name="scan_fwd",
    )(p, p, p, p, ret_sb, ld, ret_nw, p, xs, bcm, pdt, ssd_sb, alog, dtb, dsk, ssd_nw, ex, exl)
    return ra, at, ss


def _rope_tables(t):
    half = ATT_HD // 2
    freqs = ROPE_BASE ** (-jnp.arange(0, half, 2, dtype=F32) / half)
    pos = jnp.arange(t)
    ang_r = (pos // GRID_W).astype(F32)[:, None] * freqs[None, :]
    ang_c = (pos % GRID_W).astype(F32)[:, None] * freqs[None, :]
    cos = jnp.concatenate([jnp.cos(ang_r), jnp.cos(ang_r), jnp.cos(ang_c), jnp.cos(ang_c)], axis=-1)
    sin = jnp.concatenate([-jnp.sin(ang_r), jnp.sin(ang_r), -jnp.sin(ang_c), jnp.sin(ang_c)], axis=-1)
    return cos, sin


def _pad_lanes(v):
    v = v.reshape(1, -1)
    return jnp.pad(v, ((0, 0), (0, LANES - v.shape[1])))


def _mixer(p, pdt, geom, layer, cos, sin, ret_log_decay, ret_norm_w, attn_sink, ssd_conv_w,
           ssd_conv_b, ssd_a_log, ssd_dt_bias, ssd_d, ssd_norm_w):
    ld = jnp.broadcast_to(ret_log_decay[layer][:, :, None], (2, RET_HEADS, LANES))
    return _scan_mixers(p, pdt, geom, ld, ret_norm_w[layer].reshape(1, RET_W), attn_sink[layer], cos, sin,
                        ssd_conv_w[layer], ssd_conv_b[layer].reshape(1, -1),
                        _pad_lanes(ssd_a_log[layer]), _pad_lanes(ssd_dt_bias[layer]),
                        jnp.repeat(ssd_d[layer], SSD_HD).reshape(1, SSD_W),
                        ssd_norm_w[layer].reshape(1, SSD_W))


def _forward(x, c, ctx, c_ctx, w_ada, b_ada, norm_w, ffn1_gu, ffn1_down, ffn2_gu, ffn2_down,
             w_in, w_out, ret_log_decay, ret_norm_w, attn_sink, ssd_conv_w, ssd_conv_b,
             ssd_a_log, ssd_dt_bias, ssd_d, ssd_norm_w):
    b, t, d = x.shape
    lc = ctx.shape[1]
    depth = w_ada.shape[0]
    geom = _Geom(b, t, lc)
    n_lat = b * t
    tm = math.gcd(512, math.gcd(t, b * lc))

    def mod_row(i):
        r0 = i * tm
        return jnp.where(r0 < n_lat, 1 + r0 // t, 0)

    cond = jnp.concatenate([c_ctx[None, :], c, jnp.zeros((SUBLANES - 1 - b, d), F32)], axis=0)
    mod = _adaln(cond, w_ada, b_ada)
    mod = mod.reshape(SUBLANES, depth, 3, 3, d).transpose(1, 2, 0, 3, 4).reshape(depth * 3, SUBLANES, 3, d)

    w1gu = ffn1_gu.astype(BF16)
    w1d = ffn1_down.astype(BF16)
    w2gu = ffn2_gu.astype(BF16)
    w2d = ffn2_down.astype(BF16)
    wi = w_in.astype(BF16)
    wi_dt = jnp.pad(wi[:, :, IN_MAIN:], ((0, 0), (0, 0), (0, LANES - N_DH)))
    wo = w_out.astype(BF16)

    cos, sin = _rope_tables(t)
    srcs = [x.reshape(n_lat, d), ctx.reshape(b * lc, d)]
    rows_all = n_lat + b * lc
    for layer in range(depth):
        last = layer == depth - 1
        xs = _ffn(srcs, rows_all, mod, norm_w, w1gu, w1d, layer=layer, sub=0, tm=tm, mod_row=mod_row)
        p, pdt = _proj_in(xs, mod, norm_w, wi, wi_dt, layer=layer, tm=tm, mod_row=mod_row)
        ra, at, ss = _mixer(p, pdt, geom, layer, cos, sin, ret_log_decay, ret_norm_w, attn_sink, ssd_conv_w,
                            ssd_conv_b, ssd_a_log, ssd_dt_bias, ssd_d, ssd_norm_w)
        rows_out = n_lat if last else rows_all
        xs = _proj_out(xs, rows_out, ra, at, ss, wo, mod, norm_w, layer=layer, tm=tm, mod_row=mod_row)
        xs = _ffn([xs], rows_out, mod, norm_w, w2gu, w2d, layer=layer, sub=2, tm=tm, mod_row=mod_row)
        srcs = [xs]
    return xs.reshape(b, t, d)


def kernel(x, c, ctx, c_ctx, w_ada, b_ada, norm_w, ffn1_gu, ffn1_down, ffn2_gu, ffn2_down,
           w_in, w_out, ret_log_decay, ret_norm_w, attn_sink, ssd_conv_w, ssd_conv_b,
           ssd_a_log, ssd_dt_bias, ssd_d, ssd_norm_w):
    return _forward(x, c, ctx, c_ctx, w_ada, b_ada, norm_w, ffn1_gu, ffn1_down, ffn2_gu, ffn2_down,
                    w_in, w_out, ret_log_decay, ret_norm_w, attn_sink, ssd_conv_w, ssd_conv_b,
                    ssd_a_log, ssd_dt_bias, ssd_d, ssd_norm_w)
```
